```python
import jax
import jax.numpy as jnp
from jax import lax
import numpy as np

D_MODEL = 1024
BATCH = 8
SEQ = 4096
DEPTH = 4

GRID_W = 64
CTX_LEN = 256
N_MIXERS = 3
CHUNK = 64
EPS = 1e-6
ALPHA = (2 * DEPTH) ** 0.25
BETA = (8 * DEPTH) ** -0.25

M_INNER = 2 * D_MODEL
M_HEADS = 4
M_DV = M_INNER // M_HEADS
M_DQK = M_DV // 2
M_CONV = 3

A_HEADS = 8
A_QLORA = 384
A_KVLORA = 256
A_DNOPE = 128
A_DROPE = 64
A_DV = 128
ROPE_BASE = 10000.0
Q_BLOCK = 128

G_HEADS = 4
G_DK = D_MODEL // (2 * G_HEADS)
G_DV = D_MODEL // G_HEADS
G_RANK = 16
G_TAU = 16.0

FFN_DIM = 2816
FFN_CONV = 3

kernel_name = "hybrid_mlstm_mla_gla_convffn_trunk"


def _count(kind):
    return len(range(kind, DEPTH, N_MIXERS))


def _layer_norm(x, g, b):
    xf = x.astype(jnp.float32)
    mu = jnp.mean(xf, -1, keepdims=True)
    var = jnp.mean(jnp.square(xf - mu), -1, keepdims=True)
    return ((xf - mu) * lax.rsqrt(var + EPS)).astype(x.dtype) * g + b


def _rms_norm(x, g):
    xf = x.astype(jnp.float32)
    return (xf * lax.rsqrt(jnp.mean(jnp.square(xf), -1, keepdims=True) + EPS)).astype(x.dtype) * g


def _head_rms_norm(x, g):
    B, T, H, d = x.shape
    return _rms_norm(x, g.reshape(H, d)).reshape(B, T, H * d)


def _dwconv(x, w, b):
    k = w.shape[0]
    y = lax.conv_general_dilated(x, w[:, None, :].astype(x.dtype), (1,), [(k // 2, k // 2)],
                                 dimension_numbers=("NWC", "WIO", "NWC"),
                                 feature_group_count=x.shape[-1])
    return y + b


def _modulation(cond, w, b):
    m = jax.nn.silu(cond) @ w + b
    return jnp.split(m[:, None, :], 6, axis=-1)


def _modulate(x, shift, scale):
    return x * (1 + scale) + shift


def _axial_angles(n_tokens):
    rows = n_tokens // GRID_W
    row = jnp.repeat(jnp.arange(rows, dtype=jnp.float32), GRID_W)
    col = jnp.tile(jnp.arange(GRID_W, dtype=jnp.float32), rows)
    n_freq = A_DROPE // 4
    inv_freq = ROPE_BASE ** (-jnp.arange(n_freq, dtype=jnp.float32) / n_freq)
    return row[:, None] * inv_freq, col[:, None] * inv_freq


def _rope_half(x, ang):
    x1, x2 = jnp.split(x, 2, axis=-1)
    cos = jnp.cos(ang).astype(x.dtype)
    sin = jnp.sin(ang).astype(x.dtype)
    return jnp.concatenate([x1 * cos - x2 * sin, x2 * cos + x1 * sin], -1)


def _axial_rope(x, ang_row, ang_col):
    xr, xc = jnp.split(x, 2, axis=-1)
    return jnp.concatenate([_rope_half(xr, ang_row), _rope_half(xc, ang_col)], -1)


def _time_flip(a, direction):
    return jnp.flip(a, axis=2) if direction == 1 else a


def _two_way(scan_fn, shared_c, gates_c, shared_l, gates_l, init):
    out_c, out_l = None, None
    for d in range(2):
        args_c = [_time_flip(a, d) for a in shared_c + tuple(g[d] for g in gates_c)]
        hc, state = scan_fn(*args_c, init)
        args_l = [_time_flip(a, d) for a in shared_l + tuple(g[d] for g in gates_l)]
        hl, _ = scan_fn(*args_l, state)
        hc, hl = _time_flip(hc, d), _time_flip(hl, d)
        out_c = hc if out_c is None else out_c + hc
        out_l = hl if out_l is None else out_l + hl
    return out_c, out_l


def _mlstm_chunk_scan(q, k, v, li, lf, state):
    B, H, T, _ = q.shape
    nc = T // CHUNK

    def chunks(a):
        a = a.astype(jnp.float32)
        return jnp.moveaxis(a.reshape(B, H, nc, CHUNK, *a.shape[3:]), 2, 0)

    tril = jnp.tril(jnp.ones((CHUNK, CHUNK), bool))

    def step(carry, inp):
        C, n, m = carry
        qc, kc, vc, ic, fc = inp
        b = jnp.cumsum(fc, axis=-1)
        dmat = jnp.where(tril, b[..., :, None] - b[..., None, :] + ic[..., None, :], -jnp.inf)
        inter = b + m[..., None]
        mj = jnp.maximum(inter, jnp.max(dmat, -1))
        wmat = jnp.exp(dmat - mj[..., None]) * jnp.einsum("bhjd,bhsd->bhjs", qc, kc)
        g = jnp.exp(inter - mj)
        num = g[..., None] * jnp.einsum("bhjd,bhde->bhje", qc, C) + jnp.einsum("bhjs,bhse->bhje", wmat, vc)
        den = g * jnp.einsum("bhjd,bhd->bhj", qc, n) + jnp.sum(wmat, -1)
        h = num / jnp.maximum(jnp.abs(den), jnp.exp(-mj))[..., None]
        bl = b[..., -1]
        ds = bl[..., None] - b + ic
        m_new = jnp.maximum(bl + m, jnp.max(ds, -1))
        kw = kc * jnp.exp(ds - m_new[..., None])[..., None]
        decay = jnp.exp(bl + m - m_new)
        C = decay[..., None, None] * C + jnp.einsum("bhsd,bhse->bhde", kw, vc)
        n = decay[..., None] * n + jnp.sum(kw, axis=2)
        return (C, n, m_new), h

    state, hs = lax.scan(step, state, tuple(chunks(a) for a in (q, k, v, li, lf)))
    return jnp.moveaxis(hs, 0, 2).reshape(B, H, T, -1), state


def _mlstm_inputs(h, w_up, conv_w, conv_b, w_qk, w_v, w_gates, b_gates):
    B, T, _ = h.shape
    x_m = h @ w_up
    x_c = jax.nn.silu(_dwconv(x_m, conv_w, conv_b))
    q, k = jnp.split(x_c @ w_qk, 2, axis=-1)
    q = q.reshape(B, T, M_HEADS, M_DQK).transpose(0, 2, 1, 3)
    k = k.reshape(B, T, M_HEADS, M_DQK).transpose(0, 2, 1, 3) * (M_DQK ** -0.5)
    v = (x_m @ w_v).reshape(B, T, M_HEADS, M_DV).transpose(0, 2, 1, 3)
    gates = jnp.einsum("bti,zig->zbgt", x_c, w_gates) + b_gates[:, None, :, None]
    li = gates[:, :, :M_HEADS].astype(jnp.float32)
    lf = jax.nn.log_sigmoid(gates[:, :, M_HEADS:].astype(jnp.float32))
    return (q, k, v), (li, lf)


def _mlstm_mixer(hl, hc, need_ctx, w_up, conv_w, conv_b, w_qk, w_v, w_gates, b_gates, w_og, norm_g, w_out):
    sh_l, gt_l = _mlstm_inputs(hl, w_up, conv_w, conv_b, w_qk, w_v, w_gates, b_gates)
    sh_c, gt_c = _mlstm_inputs(hc, w_up, conv_w, conv_b, w_qk, w_v, w_gates, b_gates)
    B = hl.shape[0]
    init = (jnp.zeros((B, M_HEADS, M_DQK, M_DV), jnp.float32),
            jnp.zeros((B, M_HEADS, M_DQK), jnp.float32),
            jnp.zeros((B, M_HEADS), jnp.float32))
    s_c, s_l = _two_way(_mlstm_chunk_scan, sh_c, gt_c, sh_l, gt_l, init)

    def out(h_in, s):
        Bq, T, _ = h_in.shape
        o = jax.nn.sigmoid(h_in @ w_og).reshape(Bq, T, M_HEADS, M_DV)
        hsum = o * s.transpose(0, 2, 1, 3).astype(h_in.dtype)
        return _head_rms_norm(hsum, norm_g) @ w_out

    return out(hl, s_l), (out(hc, s_c) if need_ctx else None)


def _mla_q(h, w_dq, q_norm, w_uq, ang):
    B, T, _ = h.shape
    q = (_rms_norm(h @ w_dq, q_norm) @ w_uq).reshape(B, T, A_HEADS, A_DNOPE + A_DROPE)
    if ang is None:
        return q
    q_nope, q_rope = jnp.split(q, [A_DNOPE], axis=-1)
    return jnp.concatenate([q_nope, _axial_rope(q_rope, ang[0][:, None], ang[1][:, None])], -1)


def _mla_kv(h, w_dkv, kv_norm, w_ukv, ang):
    B, T, _ = h.shape
    ckv, k_rope = jnp.split(h @ w_dkv, [A_KVLORA], axis=-1)
    if ang is not None:
        k_rope = _axial_rope(k_rope, ang[0], ang[1])
    kv = (_rms_norm(ckv, kv_norm) @ w_ukv).reshape(B, T, A_HEADS, A_DNOPE + A_DV)
    k_nope, v = jnp.split(kv, [A_DNOPE], axis=-1)
    k = jnp.concatenate([k_nope, jnp.broadcast_to(k_rope[:, :, None], (B, T, A_HEADS, A_DROPE))], -1)
    return k, v


def _attend(q, k, v):
    s = jnp.einsum("bqhd,bkhd->bhqk", q, k).astype(jnp.float32) * ((A_DNOPE + A_DROPE) ** -0.5)
    p = jax.nn.softmax(s, axis=-1).astype(v.dtype)
    return jnp.einsum("bhqk,bkhd->bqhd", p, v)


def _blocked_attend(q, k, v):
    B, T, H, d = q.shape
    qb = q.reshape(B, T // Q_BLOCK, Q_BLOCK, H, d).transpose(1, 0, 2, 3, 4)
    o = lax.map(lambda qi: _attend(qi, k, v), qb)
    return o.transpose(1, 0, 2, 3, 4).reshape(B, T, H, -1)


def _mla_mixer(hl, hc, need_ctx, ang, w_dq, q_norm, w_uq, w_dkv, kv_norm, w_ukv, w_out):
    B, T, _ = hl.shape
    kc, vc = _mla_kv(hc, w_dkv, kv_norm, w_ukv, None)
    kl, vl = _mla_kv(hl, w_dkv, kv_norm, w_ukv, ang)
    ql = _mla_q(hl, w_dq, q_norm, w_uq, ang)
    k_all = jnp.concatenate([kc, kl], axis=1)
    v_all = jnp.concatenate([vc, vl], axis=1)
    yl = _blocked_attend(ql, k_all, v_all).reshape(B, T, -1) @ w_out
    yc = None
    if need_ctx:
        qc = _mla_q(hc, w_dq, q_norm, w_uq, None)
        yc = _attend(qc, kc, vc).reshape(B, hc.shape[1], -1) @ w_out
    return yl, yc


def _gla_chunk_scan(q, k, v, la, S):
    B, H, T, _ = q.shape
    nc = T // CHUNK

    def chunks(a):
        a = a.astype(jnp.float32)
        return jnp.moveaxis(a.reshape(B, H, nc, CHUNK, *a.shape[3:]), 2, 0)

    tril = jnp.tril(jnp.ones((CHUNK, CHUNK), bool))

    def step(S, inp):
        qc, kc, vc, ac = inp
        b = jnp.cumsum(ac, axis=-2)
        inter = jnp.einsum("bhjd,bhde->bhje", qc * jnp.exp(b), S)
        rel = jnp.where(tril[:, :, None], b[:, :, :, None, :] - b[:, :, None, :, :], -jnp.inf)
        att = jnp.sum(qc[:, :, :, None, :] * kc[:, :, None, :, :] * jnp.exp(rel), axis=-1)
        o = inter + jnp.einsum("bhjs,bhse->bhje", att, vc)
        bl = b[:, :, -1:, :]
        S = jnp.exp(bl[:, :, 0, :, None]) * S + jnp.einsum("bhsd,bhse->bhde", kc * jnp.exp(bl - b), vc)
        return S, o

    S, os_ = lax.scan(step, S, tuple(chunks(a) for a in (q, k, v, la)))
    return jnp.moveaxis(os_, 0, 2).reshape(B, H, T, -1), S


def _gla_inputs(h, w_qk, w_v, w_a1, w_a2, b_a):
    B, T, _ = h.shape
    q, k = jnp.split(h @ w_qk, 2, axis=-1)
    q = q.reshape(B, T, G_HEADS, G_DK).transpose(0, 2, 1, 3) * (G_DK ** -0.5)
    k = k.reshape(B, T, G_HEADS, G_DK).transpose(0, 2, 1, 3)
    v = (h @ w_v).reshape(B, T, G_HEADS, G_DV).transpose(0, 2, 1, 3)
    z = jnp.einsum("zbtr,zrk->zbtk", jnp.einsum("btd,zdr->zbtr", h, w_a1), w_a2) + b_a[:, None, None, :]
    la = jax.nn.log_sigmoid(z.astype(jnp.float32)) / G_TAU
    la = la.reshape(2, B, T, G_HEADS, G_DK).transpose(0, 1, 3, 2, 4)
    return (q, k, v), (la,)


def _gla_mixer(hl, hc, need_ctx, w_qk, w_v, w_r, w_a1, w_a2, b_a, norm_g, w_out):
    sh_l, gt_l = _gla_inputs(hl, w_qk, w_v, w_a1, w_a2, b_a)
    sh_c, gt_c = _gla_inputs(hc, w_qk, w_v, w_a1, w_a2, b_a)
    init = jnp.zeros((hl.shape[0], G_HEADS, G_DK, G_DV), jnp.float32)
    s_c, s_l = _two_way(_gla_chunk_scan, sh_c, gt_c, sh_l, gt_l, init)

    def out(h_in, s):
        o = _head_rms_norm(s.transpose(0, 2, 1, 3).astype(h_in.dtype), norm_g)
        return (o * jax.nn.silu(h_in @ w_r)) @ w_out

    return out(hl, s_l), (out(hc, s_c) if need_ctx else None)


def _conv_ffn(h, w_in, conv_w, conv_b, w_out):
    gate, up = jnp.split(h @ w_in, 2, axis=-1)
    return (jax.nn.silu(_dwconv(gate, conv_w, conv_b)) * up) @ w_out


def setup_inputs(seed: int = 0) -> dict:
    key = jax.random.key(seed)
    keys = iter(jax.random.split(key, 48))
    D = D_MODEL
    nA, nB, nC = _count(0), _count(1), _count(2)

    def normal(shape):
        return jax.random.normal(next(keys), shape, jnp.float32)

    def dense(shape, fan_in, scale=1.0):
        return normal(shape) * (scale * fan_in ** -0.5)

    def gain(shape):
        return 1.0 + 0.05 * normal(shape)

    def bias(shape, s=0.02):
        return s * normal(shape)

    f_bias = jnp.asarray(np.linspace(3.0, 6.0, M_HEADS), jnp.float32)
    w_ukv = dense((nB, A_KVLORA, A_HEADS, A_DNOPE + A_DV), A_KVLORA)
    w_ukv = w_ukv.at[..., A_DNOPE:].multiply(BETA).reshape(nB, A_KVLORA, A_HEADS * (A_DNOPE + A_DV))
    return {
        "x": normal((BATCH, SEQ, D)),
        "c": normal((BATCH, D)),
        "ctx": normal((BATCH, CTX_LEN, D)),
        "c_ctx": normal((D,)),
        "ada_w": dense((DEPTH, D, 6 * D), D, 0.5),
        "ada_b": bias((DEPTH, 6 * D)),
        "ln_g": gain((DEPTH, 2, D)),
        "ln_b": bias((DEPTH, 2, D)),
        "ffn_w_in": dense((DEPTH, D, 2 * FFN_DIM), D, BETA),
        "ffn_conv_w": dense((DEPTH, FFN_CONV, FFN_DIM), FFN_CONV),
        "ffn_conv_b": bias((DEPTH, FFN_DIM)),
        "ffn_w_out": dense((DEPTH, FFN_DIM, D), FFN_DIM, BETA),
        "m_w_up": dense((nA, D, M_INNER), D),
        "m_conv_w": dense((nA, M_CONV, M_INNER), M_CONV),
        "m_conv_b": bias((nA, M_INNER)),
        "m_w_qk": dense((nA, M_INNER, 2 * M_HEADS * M_DQK), M_INNER),
        "m_w_v": dense((nA, M_INNER, M_HEADS * M_DV), M_INNER, BETA),
        "m_w_gates": dense((nA, 2, M_INNER, 2 * M_HEADS), M_INNER, 0.5),
        "m_b_gates": jnp.concatenate([0.1 * normal((nA, 2, M_HEADS)),
                                      f_bias + 0.1 * normal((nA, 2, M_HEADS))], axis=-1),
        "m_w_og": dense((nA, D, M_HEADS * M_DV), D),
        "m_norm_g": gain((nA, M_HEADS * M_DV)),
        "m_w_out": dense((nA, M_HEADS * M_DV, D), M_HEADS * M_DV, BETA),
        "a_w_dq": dense((nB, D, A_QLORA), D),
        "a_q_norm": gain((nB, A_QLORA)),
        "a_w_uq": dense((nB, A_QLORA, A_HEADS * (A_DNOPE + A_DROPE)), A_QLORA),
        "a_w_dkv": dense((nB, D, A_KVLORA + A_DROPE), D),
        "a_kv_norm": gain((nB, A_KVLORA)),
        "a_w_ukv": w_ukv,
        "a_w_out": dense((nB, A_HEADS * A_DV, D), A_HEADS * A_DV, BETA),
        "g_w_qk": dense((nC, D, 2 * G_HEADS * G_DK), D),
        "g_w_v": dense((nC, D, G_HEADS * G_DV), D, BETA),
        "g_w_r": dense((nC, D, G_HEADS * G_DV), D),
        "g_w_a1": dense((nC, 2, D, G_RANK), D),
        "g_w_a2": dense((nC, 2, G_RANK, G_HEADS * G_DK), G_RANK),
        "g_b_a": bias((nC, 2, G_HEADS * G_DK), 0.1),
        "g_norm_g": gain((nC, G_HEADS * G_DV)),
        "g_w_out": dense((nC, G_HEADS * G_DV, D), G_HEADS * G_DV, BETA),
    }


def reference(x, c, ctx, c_ctx, ada_w, ada_b, ln_g, ln_b,
              ffn_w_in, ffn_conv_w, ffn_conv_b, ffn_w_out,
              m_w_up, m_conv_w, m_conv_b, m_w_qk, m_w_v, m_w_gates, m_b_gates, m_w_og, m_norm_g, m_w_out,
              a_w_dq, a_q_norm, a_w_uq, a_w_dkv, a_kv_norm, a_w_ukv, a_w_out,
              g_w_qk, g_w_v, g_w_r, g_w_a1, g_w_a2, g_b_a, g_norm_g, g_w_out):
    ang = _axial_angles(x.shape[1])
    xc = ctx
    for i in range(DEPTH):
        need_ctx = i < DEPTH - 1
        kind, j = i % N_MIXERS, i // N_MIXERS
        ml = _modulation(c, ada_w[i], ada_b[i])
        mc = _modulation(c_ctx[None, :], ada_w[i], ada_b[i])
        hl = _modulate(x, ml[0], ml[1])
        hc = _modulate(xc, mc[0], mc[1])
        if kind == 0:
            yl, yc = _mlstm_mixer(hl, hc, need_ctx, m_w_up[j], m_conv_w[j], m_conv_b[j], m_w_qk[j], m_w_v[j],
                                  m_w_gates[j], m_b_gates[j], m_w_og[j], m_norm_g[j], m_w_out[j])
        elif kind == 1:
            yl, yc = _mla_mixer(hl, hc, need_ctx, ang, a_w_dq[j], a_q_norm[j], a_w_uq[j], a_w_dkv[j],
                                a_kv_norm[j], a_w_ukv[j], a_w_out[j])
        else:
            yl, yc = _gla_mixer(hl, hc, need_ctx, g_w_qk[j], g_w_v[j], g_w_r[j], g_w_a1[j], g_w_a2[j],
                                g_b_a[j], g_norm_g[j], g_w_out[j])
        x = _layer_norm(ALPHA * x + ml[2] * yl, ln_g[i, 0], ln_b[i, 0])
        f = _conv_ffn(_modulate(x, ml[3], ml[4]), ffn_w_in[i], ffn_conv_w[i], ffn_conv_b[i], ffn_w_out[i])
        x = _layer_norm(ALPHA * x + ml[5] * f, ln_g[i, 1], ln_b[i, 1])
        if need_ctx:
            xc = _layer_norm(ALPHA * xc + mc[2] * yc, ln_g[i, 0], ln_b[i, 0])
            fc = _conv_ffn(_modulate(xc, mc[3], mc[4]), ffn_w_in[i], ffn_conv_w[i], ffn_conv_b[i], ffn_w_out[i])
            xc = _layer_norm(ALPHA * xc + mc[5] * fc, ln_g[i, 1], ln_b[i, 1])
    return x
```

```python
import functools

import numpy as np
import jax
import jax.numpy as jnp
from jax import lax
from jax.experimental import pallas as pl
from jax.experimental.pallas import tpu as pltpu

F32 = jnp.float32
MXU = jnp.bfloat16

V7X_VMEM_BYTES = 64 * 2**20
SUBLANES = 8
LANES = 128

TM = 256
HALO = SUBLANES
GRID_W = 64
EPS = 1e-6
ROPE_BASE = 10000.0
G_TAU = 16.0

M_HEADS, A_HEADS, G_HEADS = 4, 8, 4
A_DNOPE, A_DROPE, A_DV, A_KVLORA = 128, 64, 128, 256
G_RANK = 16
M_CHUNK = 256
G_CHUNK = 64


def _vmem_limit(nbytes):
    return int(min(max(nbytes, 16 * 2**20), V7X_VMEM_BYTES - 8 * 2**20))


def _const_spec(shape):
    nd = len(shape)
    return pl.BlockSpec(shape, lambda *_: (0,) * nd, pipeline_mode=pl.Buffered(1))


def _dot(a, b):
    return jnp.dot(a, b, preferred_element_type=F32)


def _dot_nt(a, b):
    return lax.dot_general(a, b, (((1,), (1,)), ((), ())), preferred_element_type=F32)


def _dot_tn(a, b):
    return lax.dot_general(a, b, (((0,), (0,)), ((), ())), preferred_element_type=F32)


def _split3(x):
    hi = x.astype(MXU)
    r1 = x - hi.astype(F32)
    mid = r1.astype(MXU)
    lo = (r1 - mid.astype(F32)).astype(MXU)
    return hi, mid, lo


def _sigmoid(x):
    return 1.0 / (1.0 + jnp.exp(-x))


def _silu(x):
    return x * _sigmoid(x)


def _log_sigmoid(x):
    return jnp.minimum(x, 0.0) - jnp.log1p(jnp.exp(-jnp.abs(x)))


def _layer_norm(z, g, b):
    mu = jnp.mean(z, -1, keepdims=True)
    zc = z - mu
    var = jnp.mean(zc * zc, -1, keepdims=True)
    return zc * lax.rsqrt(var + EPS) * g + b


def _rms(x, g):
    return x * lax.rsqrt(jnp.mean(x * x, -1, keepdims=True) + EPS) * g


def _head_rms(x, g, heads):
    d = x.shape[-1] // heads
    return jnp.concatenate([_rms(x[:, h * d:(h + 1) * d], g[:, h * d:(h + 1) * d]) for h in range(heads)], axis=-1)


def _mod(x, mod_ref, k, d):
    return x * (1.0 + mod_ref[:, (k + 1) * d:(k + 2) * d]) + mod_ref[:, k * d:(k + 1) * d]


def _seq_flags(i, n_lat_tiles, tiles_per_seq):
    is_ctx = i >= n_lat_tiles
    pos = i % tiles_per_seq
    return jnp.logical_or(is_ctx, pos == 0), jnp.logical_or(is_ctx, pos == tiles_per_seq - 1)


def _halo_rows(xp_ref, x_ref, xn_ref, mod_ref, k, d, geom):
    first, last = _seq_flags(pl.program_id(0), *geom)
    hp = jnp.where(first, 0.0, _mod(xp_ref[...], mod_ref, k, d))
    hn = jnp.where(last, 0.0, _mod(xn_ref[...], mod_ref, k, d))
    return jnp.concatenate([hp, _mod(x_ref[...], mod_ref, k, d), hn], axis=0)


def _dwconv3(g_ext, w_ref, b_ref, c0, c1):
    n = g_ext.shape[0] - 2 * HALO
    return (w_ref[0:1, c0:c1] * g_ext[HALO - 1:HALO - 1 + n] + w_ref[1:2, c0:c1] * g_ext[HALO:HALO + n]
            + w_ref[2:3, c0:c1] * g_ext[HALO + 1:HALO + 1 + n] + b_ref[:, c0:c1])


class _Rows:
    def __init__(self, batch, seq, ctx_len):
        assert ctx_len == TM and seq % TM == 0
        self.batch, self.seq = batch, seq
        self.tiles_per_seq = seq // TM
        self.n_lat_tiles = batch * self.tiles_per_seq
        self.n_tiles = self.n_lat_tiles + batch
        self.rows = self.n_tiles * TM
        self.geom = (self.n_lat_tiles, self.tiles_per_seq)

    def tile(self, width):
        return pl.BlockSpec((TM, width), lambda i: (i, 0))

    def halo_prev(self, width):
        per = TM // HALO
        return pl.BlockSpec((HALO, width), lambda i: (jnp.maximum(i * per - 1, 0), 0))

    def halo_next(self, width, n_rows):
        per, nblk = TM // HALO, n_rows // HALO
        return pl.BlockSpec((HALO, width), lambda i: (jnp.minimum((i + 1) * per, nblk - 1), 0))

    def mod(self, width):
        nl, tps, b = self.n_lat_tiles, self.tiles_per_seq, self.batch
        return pl.BlockSpec((None, 1, width), lambda i: (jnp.where(i >= nl, b, i // tps), 0, 0))


def _row_call(body, rows, n_tiles, in_specs, out_specs, out_shape, scratch=(), vmem=0, name=None):
    return pl.pallas_call(
        body, grid=(n_tiles,), in_specs=in_specs, out_specs=out_specs, out_shape=out_shape,
        scratch_shapes=list(scratch), name=name,
        compiler_params=pltpu.CompilerParams(dimension_semantics=("arbitrary",), vmem_limit_bytes=_vmem_limit(vmem)))


def _modulation_kernel(c_ref, w_ref, b_ref, o_ref):
    o_ref[...] = _dot(_silu(c_ref[...]).astype(MXU), w_ref[...]) + b_ref[...]


def _modulation(cond, ada_w, ada_b):
    depth, d, n = ada_w.shape
    tn = d
    return pl.pallas_call(
        _modulation_kernel, grid=(depth, n // tn),
        in_specs=[pl.BlockSpec(cond.shape, lambda l, j: (0, 0)),
                  pl.BlockSpec((None, d, tn), lambda l, j: (l, 0, j)),
                  pl.BlockSpec((None, 1, tn), lambda l, j: (l, 0, j))],
        out_specs=pl.BlockSpec((None, cond.shape[0], tn), lambda l, j: (l, 0, j)),
        out_shape=jax.ShapeDtypeStruct((depth, cond.shape[0], n), F32), name="modulation",
    )(cond, ada_w, ada_b.reshape(depth, 1, n))


def _ffn_kernel(xp_ref, x_ref, xn_ref, mod_ref, win_ref, cw_ref, cb_ref, wout_ref, g_ref, b_ref, o_ref, a_scr,
                *, geom, alpha, ffn, tf):
    d = x_ref.shape[1]
    hext = _halo_rows(xp_ref, x_ref, xn_ref, mod_ref, 3, d, geom).astype(MXU)
    for f0 in range(0, ffn, tf):
        g_ext = _dot(hext, win_ref[:, f0:f0 + tf])
        up = _dot(hext, win_ref[:, ffn + f0:ffn + f0 + tf])[HALO:HALO + TM]
        a_scr[:, f0:f0 + tf] = (_silu(_dwconv3(g_ext, cw_ref, cb_ref, f0, f0 + tf)) * up).astype(MXU)
    f = _dot(a_scr[...], wout_ref[...])
    z = alpha * x_ref[...] + mod_ref[:, 5 * d:6 * d] * f
    o_ref[...] = _layer_norm(z, g_ref[...], b_ref[...])


def _ffn(rs, n_tiles, x, mod, w_in, conv_w, conv_b, w_out, ln_g, ln_b, alpha):
    d, ffn = w_out.shape[1], w_out.shape[0]
    tf = 256
    assert ffn % tf == 0
    body = functools.partial(_ffn_kernel, geom=rs.geom, alpha=alpha, ffn=ffn, tf=tf)
    vmem = 2 * (w_in.size + w_out.size) + 6 * TM * d * 4 + TM * ffn * 2 + 24 * 2**20
    return _row_call(
        body, rs, n_tiles,
        [rs.halo_prev(d), rs.tile(d), rs.halo_next(d, x.shape[0]), rs.mod(6 * d), _const_spec(w_in.shape), _const_spec(conv_w.shape),
         _const_spec((1, ffn)), _const_spec(w_out.shape), _const_spec((1, d)), _const_spec((1, d))],
        rs.tile(d), jax.ShapeDtypeStruct((n_tiles * TM, d), F32),
        scratch=[pltpu.VMEM((TM, ffn), MXU)], vmem=vmem, name="conv_ffn",
    )(x, x, x, mod, w_in, conv_w, conv_b.reshape(1, ffn), w_out, ln_g.reshape(1, d), ln_b.reshape(1, d))


def _residual_ln(x_ref, mod_ref, y, g_ref, b_ref, o_ref, alpha):
    d = x_ref.shape[1]
    z = alpha * x_ref[...] + mod_ref[:, 2 * d:3 * d] * y
    o_ref[...] = _layer_norm(z, g_ref[...], b_ref[...])


def _mlstm_pre_kernel(xp_ref, x_ref, xn_ref, mod_ref, wup_ref, cw_ref, cb_ref, wqk_ref, wv_ref, wg_ref, bg_ref,
                      wgt_ref, bgt_ref, q_ref, k_ref, v_ref, gates_ref, gatest_ref, *, geom, k_scale):
    d = x_ref.shape[1]
    hext = _halo_rows(xp_ref, x_ref, xn_ref, mod_ref, 0, d, geom).astype(MXU)
    xm_ext = _dot(hext, wup_ref[...])
    inner = xm_ext.shape[1]
    xc = _silu(_dwconv3(xm_ext, cw_ref, cb_ref, 0, inner)).astype(MXU)
    qk = _dot(xc, wqk_ref[...])
    half = qk.shape[1] // 2
    q_ref[...] = qk[:, :half].astype(q_ref.dtype)
    k_ref[...] = (qk[:, half:] * k_scale).astype(k_ref.dtype)
    v_ref[...] = _dot(xm_ext[HALO:HALO + TM].astype(MXU), wv_ref[...]).astype(v_ref.dtype)
    two_h = 2 * M_HEADS
    gates = _dot(xc, wg_ref[...]) + bg_ref[...]
    lane = lax.broadcasted_iota(jnp.int32, gates.shape, 1)
    gates_ref[...] = jnp.where(lane % two_h >= M_HEADS, _log_sigmoid(gates), gates)
    gates_t = _dot_nt(wgt_ref[...], xc) + bgt_ref[...]
    sub = lax.broadcasted_iota(jnp.int32, gates_t.shape, 0)
    gatest_ref[...] = jnp.where(sub % two_h >= M_HEADS, _log_sigmoid(gates_t), gates_t)


def _mlstm_pre(rs, x, mod, w_up, conv_w, conv_b, w_qk, w_v, w_gates, b_gates):
    d, inner = w_up.shape
    dqk_all = w_qk.shape[1] // 2
    n_g = 2 * 2 * M_HEADS
    wg = jnp.concatenate([w_gates[0], w_gates[1]], axis=1)
    wg_pad = jnp.pad(wg, ((0, 0), (0, LANES - n_g))).astype(MXU)
    bg = jnp.concatenate([b_gates[0], b_gates[1]])
    bg_pad = jnp.pad(bg, (0, LANES - n_g)).reshape(1, LANES)
    body = functools.partial(_mlstm_pre_kernel, geom=rs.geom, k_scale=float((dqk_all // M_HEADS) ** -0.5))
    n, r = rs.n_tiles, rs.rows
    vmem = 2 * (w_up.size + w_qk.size + w_v.size) + 8 * (TM + 2 * HALO) * inner * 4 + 16 * 2**20
    return _row_call(
        body, rs, n,
        [rs.halo_prev(d), rs.tile(d), rs.halo_next(d, x.shape[0]), rs.mod(6 * d), _const_spec(w_up.shape), _const_spec(conv_w.shape),
         _const_spec((1, inner)), _const_spec(w_qk.shape), _const_spec(w_v.shape), _const_spec((inner, LANES)),
         _const_spec((1, LANES)), _const_spec((n_g, inner)), _const_spec((n_g, 1))],
        [rs.tile(dqk_all), rs.tile(dqk_all), rs.tile(inner), rs.tile(LANES), pl.BlockSpec((n_g, TM), lambda i: (0, i))],
        [jax.ShapeDtypeStruct((r, dqk_all), MXU), jax.ShapeDtypeStruct((r, dqk_all), MXU),
         jax.ShapeDtypeStruct((r, inner), MXU), jax.ShapeDtypeStruct((r, LANES), F32),
         jax.ShapeDtypeStruct((n_g, r), F32)],
        vmem=vmem, name="mlstm_pre",
    )(x, x, x, mod, w_up, conv_w, conv_b.reshape(1, inner), w_qk, w_v, wg_pad, bg_pad, wg.T.astype(MXU),
      bg.reshape(n_g, 1))


def _scan_rowblock(batch, nc_ctx, nc_lat):
    def rowblk(b, d, c):
        cc = jnp.where(d == 1, nc_ctx - 1 - c, c)
        lc = c - nc_ctx
        lc = jnp.where(d == 1, nc_lat - 1 - lc, lc)
        return jnp.where(c < nc_ctx, batch * nc_lat + b * nc_ctx + cc, b * nc_lat + lc)
    return rowblk


def _mlstm_scan_kernel(q_ref, k_ref, v_ref, gates_ref, gatest_ref, o_ref, c_scr, n_scr, m_scr):
    h, d, c = pl.program_id(1), pl.program_id(2), pl.program_id(3)
    L = q_ref.shape[0]

    @pl.when(c == 0)
    def _():
        c_scr[...] = jnp.zeros_like(c_scr)
        n_scr[...] = jnp.zeros_like(n_scr)
        m_scr[...] = jnp.zeros_like(m_scr)

    row = lax.broadcasted_iota(jnp.int32, (L, L), 0)
    col = lax.broadcasted_iota(jnp.int32, (L, L), 1)
    sign = 1 - 2 * d
    causal = (row - col) * sign >= 0
    causal_t = (col - row) * sign >= 0
    ones_c = jnp.where(causal, 1.0, 0.0).astype(MXU)
    ones_ct = jnp.where(causal_t, 1.0, 0.0).astype(MXU)

    gates = gates_ref[...]
    gates_t = gatest_ref[...]
    cs = sum(_dot(ones_c, p) for p in _split3(gates))
    cs_t = sum(_dot(p, ones_ct) for p in _split3(gates_t))
    two_h = 2 * M_HEADS
    lane = lax.broadcasted_iota(jnp.int32, gates.shape, 1)
    sub = lax.broadcasted_iota(jnp.int32, gates_t.shape, 0)
    i_slot, f_slot = d * two_h + h, d * two_h + M_HEADS + h

    def pick_col(a, slot):
        return jnp.sum(jnp.where(lane == slot, a, 0.0), axis=1, keepdims=True)

    def pick_row(a, slot):
        return jnp.sum(jnp.where(sub == slot, a, 0.0), axis=0, keepdims=True)

    li_col, li_row = pick_col(gates, i_slot), pick_row(gates_t, i_slot)
    b_col, b_row = pick_col(cs, f_slot), pick_row(cs_t, f_slot)
    bl = jnp.sum(pick_row(gates_t, f_slot), axis=1, keepdims=True)

    m_prev = m_scr[...]
    q, k, v = q_ref[...], k_ref[...], v_ref[...]
    dmat = jnp.where(causal, b_col - b_row + li_row, -jnp.inf)
    inter = b_col + m_prev
    mj = jnp.maximum(inter, jnp.max(dmat, axis=1, keepdims=True))
    wmat = jnp.exp(dmat - mj) * _dot_nt(q, k)
    g = jnp.exp(inter - mj)
    num = g * _dot(q, c_scr[...].astype(MXU)) + _dot(wmat.astype(MXU), v)
    den = g * jnp.sum(q.astype(F32) * n_scr[...], axis=1, keepdims=True) + jnp.sum(wmat, axis=1, keepdims=True)
    o_ref[...] = num / jnp.maximum(jnp.abs(den), jnp.exp(-mj))

    ds = bl - b_col + li_col
    m_new = jnp.maximum(bl + m_prev, jnp.max(ds, axis=0, keepdims=True))
    kw = k.astype(F32) * jnp.exp(ds - m_new)
    decay = jnp.exp(bl + m_prev - m_new)
    c_scr[...] = decay * c_scr[...] + _dot_tn(kw.astype(MXU), v)
    n_scr[...] = decay * n_scr[...] + jnp.sum(kw, axis=0, keepdims=True)
    m_scr[...] = m_new


def _mlstm_scan(rs, q, k, v, gates, gates_t):
    b, L = rs.batch, M_CHUNK
    dqk, dv = q.shape[1] // M_HEADS, v.shape[1] // M_HEADS
    nc_ctx, nc_lat = TM // L, rs.seq // L
    rowblk = _scan_rowblock(b, nc_ctx, nc_lat)
    n_g = gates_t.shape[0]
    return pl.pallas_call(
        _mlstm_scan_kernel, grid=(b, M_HEADS, 2, nc_ctx + nc_lat),
        in_specs=[pl.BlockSpec((L, dqk), lambda b_, h, d, c: (rowblk(b_, d, c), h)),
                  pl.BlockSpec((L, dqk), lambda b_, h, d, c: (rowblk(b_, d, c), h)),
                  pl.BlockSpec((L, dv), lambda b_, h, d, c: (rowblk(b_, d, c), h)),
                  pl.BlockSpec((L, LANES), lambda b_, h, d, c: (rowblk(b_, d, c), 0)),
                  pl.BlockSpec((n_g, L), lambda b_, h, d, c: (0, rowblk(b_, d, c)))],
        out_specs=pl.BlockSpec((None, L, dv), lambda b_, h, d, c: (d, rowblk(b_, d, c), h)),
        out_shape=jax.ShapeDtypeStruct((2, rs.rows, v.shape[1]), F32),
        scratch_shapes=[pltpu.VMEM((dqk, dv), F32), pltpu.VMEM((1, dqk), F32), pltpu.VMEM((1, 1), F32)],
        compiler_params=pltpu.CompilerParams(dimension_semantics=("arbitrary",) * 4,
                                             vmem_limit_bytes=_vmem_limit(32 * 2**20)),
        name="mlstm_scan",
    )(q, k, v, gates, gates_t)


def _mlstm_post_kernel(x_ref, s0_ref, s1_ref, mod_ref, wog_ref, ng_ref, wout_ref, g_ref, b_ref, o_ref, *, alpha):
    d = x_ref.shape[1]
    h = _mod(x_ref[...], mod_ref, 0, d).astype(MXU)
    og = _sigmoid(_dot(h, wog_ref[...]))
    a = _head_rms(og * (s0_ref[...] + s1_ref[...]), ng_ref[...], M_HEADS).astype(MXU)
    _residual_ln(x_ref, mod_ref, _dot(a, wout_ref[...]), g_ref, b_ref, o_ref, alpha)


def _mlstm_post(rs, n_tiles, x, s, mod, w_og, norm_g, w_out, ln_g, ln_b, alpha):
    d, inner = w_og.shape
    body = functools.partial(_mlstm_post_kernel, alpha=alpha)
    vmem = 2 * (w_og.size + w_out.size) + 10 * TM * inner * 4 + 16 * 2**20
    return _row_call(
        body, rs, n_tiles,
        [rs.tile(d), pl.BlockSpec((None, TM, inner), lambda i: (0, i, 0)), pl.BlockSpec((None, TM, inner), lambda i: (1, i, 0)),
         rs.mod(6 * d), _const_spec(w_og.shape), _const_spec((1, inner)), _const_spec(w_out.shape),
         _const_spec((1, d)), _const_spec((1, d))],
        rs.tile(d), jax.ShapeDtypeStruct((n_tiles * TM, d), F32), vmem=vmem, name="mlstm_post",
    )(x, s, s, mod, w_og, norm_g.reshape(1, inner), w_out, ln_g.reshape(1, d), ln_b.reshape(1, d))


def _mla_pre_kernel(x_ref, mod_ref, cos_ref, sin_ref, wdq_ref, qn_ref, wuq_ref, wdkv_ref, kvn_ref, wk_ref, wv_ref,
                    q_ref, k_ref, v_ref):
    d = x_ref.shape[1]
    hd = A_DNOPE + LANES
    h = _mod(x_ref[...], mod_ref, 0, d).astype(MXU)
    cos, sin = cos_ref[...], sin_ref[...]
    cq = _rms(_dot(h, wdq_ref[...]), qn_ref[...]).astype(MXU)
    qa = _dot(cq, wuq_ref[...])
    part0 = A_HEADS * hd
    for hh in range(A_HEADS):
        q_ref[:, hh * hd:hh * hd + A_DNOPE] = qa[:, hh * hd:hh * hd + A_DNOPE].astype(q_ref.dtype)
        rot = qa[:, hh * hd + A_DNOPE:(hh + 1) * hd] * cos + qa[:, part0 + hh * LANES:part0 + (hh + 1) * LANES] * sin
        q_ref[:, hh * hd + A_DNOPE:(hh + 1) * hd] = rot.astype(q_ref.dtype)
    dk = _dot(h, wdkv_ref[...])
    k_rope = (dk[:, A_KVLORA:A_KVLORA + LANES] * cos + dk[:, A_KVLORA + LANES:] * sin).astype(k_ref.dtype)
    ckv = _rms(dk[:, :A_KVLORA], kvn_ref[...]).astype(MXU)
    k_nope = _dot(ckv, wk_ref[...])
    for hh in range(A_HEADS):
        k_ref[:, hh * hd:hh * hd + A_DNOPE] = k_nope[:, hh * A_DNOPE:(hh + 1) * A_DNOPE].astype(k_ref.dtype)
        k_ref[:, hh * hd + A_DNOPE:(hh + 1) * hd] = k_rope
    v_ref[...] = _dot(ckv, wv_ref[...]).astype(v_ref.dtype)


def _rope_tables(seq):
    n_freq = A_DROPE // 4
    inv_freq = ROPE_BASE ** (-jnp.arange(n_freq, dtype=F32) / n_freq)
    pos = jnp.arange(seq)
    ang_row = (pos // GRID_W).astype(F32)[:, None] * inv_freq
    ang_col = (pos % GRID_W).astype(F32)[:, None] * inv_freq
    cos = jnp.concatenate([jnp.cos(ang_row)] * 2 + [jnp.cos(ang_col)] * 2, axis=1)
    sin = jnp.concatenate([-jnp.sin(ang_row), jnp.sin(ang_row), -jnp.sin(ang_col), jnp.sin(ang_col)], axis=1)
    pad = LANES - A_DROPE
    cos = jnp.pad(cos, ((0, 0), (0, pad)), constant_values=1.0)
    sin = jnp.pad(sin, ((0, 0), (0, pad)))
    ident = (jnp.ones((TM, LANES), F32), jnp.zeros((TM, LANES), F32))
    return jnp.concatenate([ident[0], cos]), jnp.concatenate([ident[1], sin])


def _rope_partner_cols(w):
    idx = np.arange(A_DROPE)
    half = A_DROPE // 4
    partner = np.where((idx % (2 * half)) < half, idx + half, idx - half)
    return w[..., partner]


def _mla_pre(rs, x, mod, w_dq, q_norm, w_uq, w_dkv, kv_norm, w_ukv):
    d, qlora = w_dq.shape
    hd = A_DNOPE + LANES
    pad = LANES - A_DROPE
    wq = w_uq.reshape(qlora, A_HEADS, A_DNOPE + A_DROPE)
    wq_main = jnp.pad(wq, ((0, 0), (0, 0), (0, pad))).reshape(qlora, A_HEADS * hd)
    wq_part = jnp.pad(_rope_partner_cols(wq[..., A_DNOPE:]), ((0, 0), (0, 0), (0, pad))).reshape(qlora, A_HEADS * LANES)
    wuq_all = jnp.concatenate([wq_main, wq_part], axis=1).astype(MXU)
    w_kr = w_dkv[:, A_KVLORA:]
    wdkv_all = jnp.concatenate([w_dkv[:, :A_KVLORA], jnp.pad(w_kr, ((0, 0), (0, pad))),
                                jnp.pad(_rope_partner_cols(w_kr), ((0, 0), (0, pad)))], axis=1).astype(MXU)
    wkv = w_ukv.reshape(A_KVLORA, A_HEADS, A_DNOPE + A_DV)
    w_k = wkv[..., :A_DNOPE].reshape(A_KVLORA, A_HEADS * A_DNOPE).astype(MXU)
    w_v = wkv[..., A_DNOPE:].reshape(A_KVLORA, A_HEADS * A_DV).astype(MXU)
    cos, sin = _rope_tables(rs.seq)
    nl, tps = rs.geom
    tab = pl.BlockSpec((TM, LANES), lambda i: (jnp.where(i >= nl, 0, 1 + i % tps), 0))
    n, r = rs.n_tiles, rs.rows
    vmem = 2 * 2 * (w_dq.size + wuq_all.size + wdkv_all.size + w_k.size + w_v.size) + 12 * TM * A_HEADS * hd * 4
    return _row_call(
        _mla_pre_kernel, rs, n,
        [rs.tile(d), rs.mod(6 * d), tab, tab, _const_spec(w_dq.shape), _const_spec((1, qlora)), _const_spec(wuq_all.shape),
         _const_spec(wdkv_all.shape), _const_spec((1, A_KVLORA)), _const_spec(w_k.shape), _const_spec(w_v.shape)],
        [rs.tile(A_HEADS * hd), rs.tile(A_HEADS * hd), rs.tile(A_HEADS * A_DV)],
        [jax.ShapeDtypeStruct((r, A_HEADS * hd), MXU), jax.ShapeDtypeStruct((r, A_HEADS * hd), MXU),
         jax.ShapeDtypeStruct((r, A_HEADS * A_DV), MXU)],
        vmem=vmem, name="mla_pre",
    )(x, mod, cos, sin, w_dq.astype(MXU), q_norm.reshape(1, qlora), wuq_all, wdkv_all, kv_norm.reshape(1, A_KVLORA), w_k, w_v)


def _attn_kernel(*refs, scale, with_latent):
    if with_latent:
        q_ref, kc_ref, vc_ref, kl_ref, vl_ref, _, o_ref = refs
    else:
        q_ref, kc_ref, vc_ref, o_ref = refs
    q = q_ref[...]
    s_c = _dot_nt(q, kc_ref[...]) * scale
    m = jnp.max(s_c, axis=1, keepdims=True)
    if with_latent:
        s_l = _dot_nt(q, kl_ref[...]) * scale
        m = jnp.maximum(m, jnp.max(s_l, axis=1, keepdims=True))
    p_c = jnp.exp(s_c - m)
    l = jnp.sum(p_c, axis=1, keepdims=True)
    acc = _dot(p_c.astype(MXU), vc_ref[...])
    if with_latent:
        p_l = jnp.exp(s_l - m)
        l = l + jnp.sum(p_l, axis=1, keepdims=True)
        acc = acc + _dot(p_l.astype(MXU), vl_ref[...])
    o_ref[...] = (acc / l).astype(o_ref.dtype)


def _mla_attention(rs, q, k, v):
    b, t = rs.batch, rs.seq
    hd = A_DNOPE + LANES
    scale = float((A_DNOPE + A_DROPE) ** -0.5)
    ctx_blk0 = rs.n_lat_tiles
    params = pltpu.CompilerParams(dimension_semantics=("arbitrary",) * 3, vmem_limit_bytes=_vmem_limit(48 * 2**20))
    out_shape = jax.ShapeDtypeStruct((rs.rows, A_HEADS * A_DV), MXU)
    o = pl.pallas_call(
        functools.partial(_attn_kernel, scale=scale, with_latent=False), grid=(b, A_HEADS, 1),
        in_specs=[pl.BlockSpec((TM, hd), lambda b_, h, i: (ctx_blk0 + b_, h)),
                  pl.BlockSpec((TM, hd), lambda b_, h, i: (ctx_blk0 + b_, h)),
                  pl.BlockSpec((TM, A_DV), lambda b_, h, i: (ctx_blk0 + b_, h))],
        out_specs=pl.BlockSpec((TM, A_DV), lambda b_, h, i: (ctx_blk0 + b_, h)),
        out_shape=out_shape, compiler_params=params, name="mla_attn_ctx",
    )(q, k, v)
    tq = TM
    nq = t // tq
    return pl.pallas_call(
        functools.partial(_attn_kernel, scale=scale, with_latent=True), grid=(b, A_HEADS, nq),
        in_specs=[pl.BlockSpec((tq, hd), lambda b_, h, i: (b_ * nq + i, h)),
                  pl.BlockSpec((TM, hd), lambda b_, h, i: (ctx_blk0 + b_, h)),
                  pl.BlockSpec((TM, A_DV), lambda b_, h, i: (ctx_blk0 + b_, h)),
                  pl.BlockSpec((t, hd), lambda b_, h, i: (b_, h)),
                  pl.BlockSpec((t, A_DV), lambda b_, h, i: (b_, h)),
                  pl.BlockSpec(memory_space=pl.ANY)],
        out_specs=pl.BlockSpec((tq, A_DV), lambda b_, h, i: (b_ * nq + i, h)),
        out_shape=out_shape, input_output_aliases={5: 0}, compiler_params=params, name="mla_attn_latent",
    )(q, k, v, k, v, o)


def _proj_post_kernel(x_ref, a_ref, mod_ref, wout_ref, g_ref, b_ref, o_ref, *, alpha):
    _residual_ln(x_ref, mod_ref, _dot(a_ref[...], wout_ref[...]), g_ref, b_ref, o_ref, alpha)


def _proj_post(rs, n_tiles, x, a, mod, w_out, ln_g, ln_b, alpha):
    kdim, d = w_out.shape
    body = functools.partial(_proj_post_kernel, alpha=alpha)
    return _row_call(
        body, rs, n_tiles,
        [rs.tile(d), rs.tile(kdim), rs.mod(6 * d), _const_spec(w_out.shape), _const_spec((1, d)), _const_spec((1, d))],
        rs.tile(d), jax.ShapeDtypeStruct((n_tiles * TM, d), F32), vmem=32 * 2**20, name="proj_post",
    )(x, a, mod, w_out, ln_g.reshape(1, d), ln_b.reshape(1, d))


def _gla_pre_kernel(x_ref, mod_ref, wqk_ref, wv_ref, wa1_ref, wa2_ref, ba_ref, q_ref, k_ref, v_ref, la_ref,
                    *, q_scale):
    d = x_ref.shape[1]
    h = _mod(x_ref[...], mod_ref, 0, d).astype(MXU)
    qk = _dot(h, wqk_ref[...])
    half = qk.shape[1] // 2
    q_ref[...] = qk[:, :half] * q_scale
    k_ref[...] = qk[:, half:]
    v_ref[...] = _dot(h, wv_ref[...]).astype(v_ref.dtype)
    a1 = _dot(h, wa1_ref[...]).astype(MXU)
    z = _dot(a1, wa2_ref[...]) + ba_ref[...]
    la_ref[...] = _log_sigmoid(z) * (1.0 / G_TAU)


def _gla_pre(rs, x, mod, w_qk, w_v, w_a1, w_a2, b_a):
    d = w_qk.shape[0]
    dk_all = w_qk.shape[1] // 2
    wa1 = jnp.pad(jnp.concatenate([w_a1[0], w_a1[1]], axis=1), ((0, 0), (0, LANES - 2 * G_RANK))).astype(MXU)
    wa2 = jnp.zeros((LANES, 2 * dk_all), F32)
    wa2 = wa2.at[:G_RANK, :dk_all].set(w_a2[0]).at[G_RANK:2 * G_RANK, dk_all:].set(w_a2[1]).astype(MXU)
    ba = jnp.concatenate([b_a[0], b_a[1]]).reshape(1, 2 * dk_all)
    body = functools.partial(_gla_pre_kernel, q_scale=float((dk_all // G_HEADS) ** -0.5))
    n, r = rs.n_tiles, rs.rows
    return _row_call(
        body, rs, n,
        [rs.tile(d), rs.mod(6 * d), _const_spec(w_qk.shape), _const_spec(w_v.shape), _const_spec(wa1.shape),
         _const_spec(wa2.shape), _const_spec(ba.shape)],
        [rs.tile(dk_all), rs.tile(dk_all), rs.tile(w_v.shape[1]), rs.tile(2 * dk_all)],
        [jax.ShapeDtypeStruct((r, dk_all), F32), jax.ShapeDtypeStruct((r, dk_all), F32),
         jax.ShapeDtypeStruct((r, w_v.shape[1]), MXU), jax.ShapeDtypeStruct((r, 2 * dk_all), F32)],
        vmem=40 * 2**20, name="gla_pre",
    )(x, mod, w_qk, w_v, wa1, wa2, ba)


def _gla_tables(L):
    t = np.arange(L)
    tau, taup = t[:, None], t[None, :]
    groups = [taup <= tau, taup > tau]
    masks = []
    c = L // 2
    while c >= 1:
        mid = (t // (2 * c)) * 2 * c + c
        second = (t % (2 * c)) >= c
        groups.append(second[:, None] & (taup >= mid[:, None]) & (taup <= tau))
        groups.append((~second)[:, None] & (taup > tau) & (taup <= mid[:, None] - 1))
        masks.append(second[:, None] & (~second)[None, :] & ((t // (2 * c))[:, None] == (t // (2 * c))[None, :]))
        c //= 2
    fwd = np.concatenate(groups, axis=0).astype(np.float32)
    fwd_masks = np.stack(masks).astype(np.float32)
    flip = lambda g: g[::-1, ::-1]
    bwd = np.concatenate([flip(g) for g in groups], axis=0).astype(np.float32)
    bwd_masks = np.stack([flip(m) for m in masks]).astype(np.float32)
    return np.stack([fwd, bwd]), np.stack([fwd_masks, bwd_masks])


def _gla_scan_kernel(q_ref, k_ref, v_ref, la_ref, sums_ref, masks_ref, o_ref, st_scr):
    c = pl.program_id(3)
    L, dk = q_ref.shape
    levels = masks_ref.shape[0]

    @pl.when(c == 0)
    def _():
        st_scr[...] = jnp.zeros_like(st_scr)

    q, k, v, la = q_ref[...], k_ref[...], v_ref[...], la_ref[...]
    sums = sums_ref[...]
    dec = sum(_dot(sums, p) for p in _split3(la))
    b, rem = dec[0:L], dec[L:2 * L]
    bl = jnp.sum(la, axis=0, keepdims=True)
    st = st_scr[...]
    inter = _dot_nt((q * jnp.exp(b)).astype(MXU), st.astype(MXU))
    row = lax.broadcasted_iota(jnp.int32, (L, L), 0)
    col = lax.broadcasted_iota(jnp.int32, (L, L), 1)
    att = jnp.where(row == col, jnp.sum(q * k, axis=1, keepdims=True), 0.0)
    for lv in range(levels):
        aq = dec[(2 + 2 * lv) * L:(3 + 2 * lv) * L]
        ak = dec[(3 + 2 * lv) * L:(4 + 2 * lv) * L]
        att = att + masks_ref[lv] * _dot_nt((q * jnp.exp(aq)).astype(MXU), (k * jnp.exp(ak)).astype(MXU))
    o_ref[...] = inter + _dot(att.astype(MXU), v)
    kd = (k * jnp.exp(rem)).astype(MXU)
    st_scr[...] = st * jnp.exp(bl) + _dot_tn(v, kd)


def _gla_scan(rs, q, k, v, la):
    b, L = rs.batch, G_CHUNK
    dk, dv = q.shape[1] // G_HEADS, v.shape[1] // G_HEADS
    nc_ctx, nc_lat = TM // L, rs.seq // L
    rowblk = _scan_rowblock(b, nc_ctx, nc_lat)
    sums, masks = _gla_tables(L)
    sums, masks = jnp.asarray(sums, MXU), jnp.asarray(masks, F32)
    return pl.pallas_call(
        _gla_scan_kernel, grid=(b, G_HEADS, 2, nc_ctx + nc_lat),
        in_specs=[pl.BlockSpec((L, dk), lambda b_, h, d, c: (rowblk(b_, d, c), h)),
                  pl.BlockSpec((L, dk), lambda b_, h, d, c: (rowblk(b_, d, c), h)),
                  pl.BlockSpec((L, dv), lambda b_, h, d, c: (rowblk(b_, d, c), h)),
                  pl.BlockSpec((L, dk), lambda b_, h, d, c: (rowblk(b_, d, c), d * G_HEADS + h)),
                  pl.BlockSpec((None,) + sums.shape[1:], lambda b_, h, d, c: (d, 0, 0)),
                  pl.BlockSpec((None,) + masks.shape[1:], lambda b_, h, d, c: (d, 0, 0, 0))],
        out_specs=pl.BlockSpec((None, L, dv), lambda b_, h, d, c: (d, rowblk(b_, d, c), h)),
        out_shape=jax.ShapeDtypeStruct((2, rs.rows, v.shape[1]), F32),
        scratch_shapes=[pltpu.VMEM((dv, dk), F32)],
        compiler_params=pltpu.CompilerParams(dimension_semantics=("arbitrary",) * 4,
                                             vmem_limit_bytes=_vmem_limit(32 * 2**20)),
        name="gla_scan",
    )(q, k, v, la, sums, masks)


def _gla_post_kernel(x_ref, s0_ref, s1_ref, mod_ref, wr_ref, ng_ref, wout_ref, g_ref, b_ref, o_ref, *, alpha):
    d = x_ref.shape[1]
    h = _mod(x_ref[...], mod_ref, 0, d).astype(MXU)
    r = _silu(_dot(h, wr_ref[...]))
    a = (_head_rms(s0_ref[...] + s1_ref[...], ng_ref[...], G_HEADS) * r).astype(MXU)
    _residual_ln(x_ref, mod_ref, _dot(a, wout_ref[...]), g_ref, b_ref, o_ref, alpha)


def _gla_post(rs, n_tiles, x, s, mod, w_r, norm_g, w_out, ln_g, ln_b, alpha):
    d, dv_all = w_r.shape
    body = functools.partial(_gla_post_kernel, alpha=alpha)
    return _row_call(
        body, rs, n_tiles,
        [rs.tile(d), pl.BlockSpec((None, TM, dv_all), lambda i: (0, i, 0)), pl.BlockSpec((None, TM, dv_all), lambda i: (1, i, 0)),
         rs.mod(6 * d), _const_spec(w_r.shape), _const_spec((1, dv_all)), _const_spec(w_out.shape),
         _const_spec((1, d)), _const_spec((1, d))],
        rs.tile(d), jax.ShapeDtypeStruct((n_tiles * TM, d), F32), vmem=40 * 2**20, name="gla_post",
    )(x, s, s, mod, w_r, norm_g.reshape(1, dv_all), w_out, ln_g.reshape(1, d), ln_b.reshape(1, d))


def kernel(x, c, ctx, c_ctx, ada_w, ada_b, ln_g, ln_b, ffn_w_in, ffn_conv_w, ffn_conv_b, ffn_w_out, m_w_up, m_conv_w, m_conv_b, m_w_qk, m_w_v, m_w_gates, m_b_gates, m_w_og, m_norm_g, m_w_out, a_w_dq, a_q_norm, a_w_uq, a_w_dkv, a_kv_norm, a_w_ukv, a_w_out, g_w_qk, g_w_v, g_w_r, g_w_a1, g_w_a2, g_b_a, g_norm_g, g_w_out):
    batch, seq, d = x.shape
    depth = ada_w.shape[0]
    n_mixers = 3
    alpha = float((2 * depth) ** 0.25)
    rs = _Rows(batch, seq, ctx.shape[1])
    bf = lambda w: w.astype(MXU)

    cond_rows = -(-(batch + 1) // SUBLANES) * SUBLANES
    cond = jnp.zeros((cond_rows, d), F32).at[:batch].set(c).at[batch].set(c_ctx)
    mods = _modulation(cond, bf(ada_w), ada_b).reshape(depth, cond_rows, 1, 6 * d)

    xa = jnp.concatenate([x.reshape(batch * seq, d), ctx.reshape(-1, d)], axis=0)
    for i in range(depth):
        need_ctx = i < depth - 1
        n_tiles = rs.n_tiles if need_ctx else rs.n_lat_tiles
        kind, j = i % n_mixers, i // n_mixers
        mod = mods[i]
        if kind == 0:
            q, k, v, gates, gates_t = _mlstm_pre(rs, xa, mod, bf(m_w_up[j]), m_conv_w[j], m_conv_b[j], bf(m_w_qk[j]),
                                                 bf(m_w_v[j]), m_w_gates[j], m_b_gates[j])
            s = _mlstm_scan(rs, q, k, v, gates, gates_t)
            xa = _mlstm_post(rs, n_tiles, xa, s, mod, bf(m_w_og[j]), m_norm_g[j], bf(m_w_out[j]), ln_g[i, 0], ln_b[i, 0], alpha)
        elif kind == 1:
            q, k, v = _mla_pre(rs, xa, mod, a_w_dq[j], a_q_norm[j], a_w_uq[j], a_w_dkv[j], a_kv_norm[j], a_w_ukv[j])
            o = _mla_attention(rs, q, k, v)
            xa = _proj_post(rs, n_tiles, xa, o, mod, bf(a_w_out[j]), ln_g[i, 0], ln_b[i, 0], alpha)
        else:
            q, k, v, la = _gla_pre(rs, xa, mod, bf(g_w_qk[j]), bf(g_w_v[j]), g_w_a1[j], g_w_a2[j], g_b_a[j])
            s = _gla_scan(rs, q, k, v, la)
            xa = _gla_post(rs, n_tiles, xa, s, mod, bf(g_w_r[j]), g_norm_g[j], bf(g_w_out[j]), ln_g[i, 0], ln_b[i, 0], alpha)
        xa = _ffn(rs, n_tiles, xa, mod, bf(ffn_w_in[i]), ffn_conv_w[i], ffn_conv_b[i], bf(ffn_w_out[i]),
                  ln_g[i, 1], ln_b[i, 1], alpha)
    return xa.reshape(batch, seq, d)
```

```python
import functools

import numpy as np
import jax
import jax.numpy as jnp
from jax import lax
from jax.experimental import pallas as pl
from jax.experimental.pallas import tpu as pltpu

F32 = jnp.float32
MXU = jnp.bfloat16

V7X_VMEM_BYTES = 64 * 2**20
SUBLANES = 8
LANES = 128

TM = 256
HALO = SUBLANES
GRID_W = 64
EPS = 1e-6
ROPE_BASE = 10000.0
G_TAU = 16.0

M_HEADS, A_HEADS, G_HEADS = 4, 8, 4
A_DNOPE, A_DROPE, A_DV, A_KVLORA = 128, 64, 128, 256
G_RANK = 16
M_CHUNK = 256
G_CHUNK = 128


def _vmem_limit(nbytes):
    return int(min(max(nbytes, 16 * 2**20), V7X_VMEM_BYTES - 8 * 2**20))


def _const_spec(shape):
    nd = len(shape)
    return pl.BlockSpec(shape, lambda *_: (0,) * nd, pipeline_mode=pl.Buffered(1))


def _dot(a, b):
    return jnp.dot(a, b, preferred_element_type=F32)


def _dot_nt(a, b):
    return lax.dot_general(a, b, (((1,), (1,)), ((), ())), preferred_element_type=F32)


def _dot_tn(a, b):
    return lax.dot_general(a, b, (((0,), (0,)), ((), ())), preferred_element_type=F32)


def _split3(x):
    hi = x.astype(MXU)
    r1 = x - hi.astype(F32)
    mid = r1.astype(MXU)
    lo = (r1 - mid.astype(F32)).astype(MXU)
    return hi, mid, lo


def _split2(x):
    hi = x.astype(MXU)
    return hi, (x - hi.astype(F32)).astype(MXU)


def _sigmoid(x):
    return 1.0 / (1.0 + jnp.exp(-x))


def _silu(x):
    return x * _sigmoid(x)


def _log_sigmoid(x):
    return jnp.minimum(x, 0.0) - jnp.log1p(jnp.exp(-jnp.abs(x)))


def _layer_norm(z, g, b):
    mu = jnp.mean(z, -1, keepdims=True)
    zc = z - mu
    var = jnp.mean(zc * zc, -1, keepdims=True)
    return zc * lax.rsqrt(var + EPS) * g + b


def _rms(x, g):
    return x * lax.rsqrt(jnp.mean(x * x, -1, keepdims=True) + EPS) * g


def _head_rms(x, g, heads):
    d = x.shape[-1] // heads
    return jnp.concatenate([_rms(x[:, h * d:(h + 1) * d], g[:, h * d:(h + 1) * d]) for h in range(heads)], axis=-1)


def _mod(x, mod_ref, k, d):
    return x * (1.0 + mod_ref[:, (k + 1) * d:(k + 2) * d]) + mod_ref[:, k * d:(k + 1) * d]


def _seq_flags(i, n_lat_tiles, tiles_per_seq):
    is_ctx = i >= n_lat_tiles
    pos = i % tiles_per_seq
    return jnp.logical_or(is_ctx, pos == 0), jnp.logical_or(is_ctx, pos == tiles_per_seq - 1)


def _halo_rows(xp_ref, x_ref, xn_ref, mod_ref, k, d, geom):
    first, last = _seq_flags(pl.program_id(0), *geom)
    hp = jnp.where(first, 0.0, _mod(xp_ref[...], mod_ref, k, d))
    hn = jnp.where(last, 0.0, _mod(xn_ref[...], mod_ref, k, d))
    return jnp.concatenate([hp, _mod(x_ref[...], mod_ref, k, d), hn], axis=0)


def _dwconv3(g_ext, w_ref, b_ref, c0, c1):
    n = g_ext.shape[0] - 2 * HALO
    return (w_ref[0:1, c0:c1] * g_ext[HALO - 1:HALO - 1 + n] + w_ref[1:2, c0:c1] * g_ext[HALO:HALO + n]
            + w_ref[2:3, c0:c1] * g_ext[HALO + 1:HALO + 1 + n] + b_ref[:, c0:c1])


class _Rows:
    def __init__(self, batch, seq, ctx_len):
        assert ctx_len == TM and seq % TM == 0
        self.batch, self.seq = batch, seq
        self.tiles_per_seq = seq // TM
        self.n_lat_tiles = batch * self.tiles_per_seq
        self.n_tiles = self.n_lat_tiles + batch
        self.rows = self.n_tiles * TM
        self.geom = (self.n_lat_tiles, self.tiles_per_seq)

    def tile(self, width):
        return pl.BlockSpec((TM, width), lambda i: (i, 0))

    def halo_prev(self, width):
        per = TM // HALO
        return pl.BlockSpec((HALO, width), lambda i: (jnp.maximum(i * per - 1, 0), 0))

    def halo_next(self, width, n_rows):
        per, nblk = TM // HALO, n_rows // HALO
        return pl.BlockSpec((HALO, width), lambda i: (jnp.minimum((i + 1) * per, nblk - 1), 0))

    def mod(self, width):
        nl, tps, b = self.n_lat_tiles, self.tiles_per_seq, self.batch
        return pl.BlockSpec((None, 1, width), lambda i: (jnp.where(i >= nl, b, i // tps), 0, 0))


def _row_call(body, rows, n_tiles, in_specs, out_specs, out_shape, scratch=(), vmem=0, name=None):
    return pl.pallas_call(
        body, grid=(n_tiles,), in_specs=in_specs, out_specs=out_specs, out_shape=out_shape,
        scratch_shapes=list(scratch), name=name,
        compiler_params=pltpu.CompilerParams(dimension_semantics=("arbitrary",), vmem_limit_bytes=_vmem_limit(vmem)))


def _modulation_kernel(c_ref, w_ref, b_ref, o_ref):
    o_ref[...] = _dot(_silu(c_ref[...]).astype(MXU), w_ref[...]) + b_ref[...]


def _modulation(cond, ada_w, ada_b):
    depth, d, n = ada_w.shape
    tn = d
    return pl.pallas_call(
        _modulation_kernel, grid=(depth, n // tn),
        in_specs=[pl.BlockSpec(cond.shape, lambda l, j: (0, 0)),
                  pl.BlockSpec((None, d, tn), lambda l, j: (l, 0, j)),
                  pl.BlockSpec((None, 1, tn), lambda l, j: (l, 0, j))],
        out_specs=pl.BlockSpec((None, cond.shape[0], tn), lambda l, j: (l, 0, j)),
        out_shape=jax.ShapeDtypeStruct((depth, cond.shape[0], n), F32), name="modulation",
    )(cond, ada_w, ada_b.reshape(depth, 1, n))


def _ffn_kernel(xp_ref, x_ref, xn_ref, mod_ref, win_ref, cw_ref, cb_ref, wout_ref, g_ref, b_ref, o_ref, a_scr,
                *, geom, alpha, ffn, tf):
    d = x_ref.shape[1]
    hext = _halo_rows(xp_ref, x_ref, xn_ref, mod_ref, 3, d, geom).astype(MXU)
    for f0 in range(0, ffn, tf):
        g_ext = _dot(hext, win_ref[:, f0:f0 + tf])
        up = _dot(hext, win_ref[:, ffn + f0:ffn + f0 + tf])[HALO:HALO + TM]
        a_scr[:, f0:f0 + tf] = (_silu(_dwconv3(g_ext, cw_ref, cb_ref, f0, f0 + tf)) * up).astype(MXU)
    f = _dot(a_scr[...], wout_ref[...])
    z = alpha * x_ref[...] + mod_ref[:, 5 * d:6 * d] * f
    o_ref[...] = _layer_norm(z, g_ref[...], b_ref[...])


def _ffn(rs, n_tiles, x, mod, w_in, conv_w, conv_b, w_out, ln_g, ln_b, alpha):
    d, ffn = w_out.shape[1], w_out.shape[0]
    tf = 256
    assert ffn % tf == 0
    body = functools.partial(_ffn_kernel, geom=rs.geom, alpha=alpha, ffn=ffn, tf=tf)
    vmem = 2 * (w_in.size + w_out.size) + 6 * TM * d * 4 + TM * ffn * 2 + 24 * 2**20
    return _row_call(
        body, rs, n_tiles,
        [rs.halo_prev(d), rs.tile(d), rs.halo_next(d, x.shape[0]), rs.mod(6 * d), _const_spec(w_in.shape), _const_spec(conv_w.shape),
         _const_spec((1, ffn)), _const_spec(w_out.shape), _const_spec((1, d)), _const_spec((1, d))],
        rs.tile(d), jax.ShapeDtypeStruct((n_tiles * TM, d), F32),
        scratch=[pltpu.VMEM((TM, ffn), MXU)], vmem=vmem, name="conv_ffn",
    )(x, x, x, mod, w_in, conv_w, conv_b.reshape(1, ffn), w_out, ln_g.reshape(1, d), ln_b.reshape(1, d))


def _residual_ln(x_ref, mod_ref, y, g_ref, b_ref, o_ref, alpha):
    d = x_ref.shape[1]
    z = alpha * x_ref[...] + mod_ref[:, 2 * d:3 * d] * y
    o_ref[...] = _layer_norm(z, g_ref[...], b_ref[...])


def _mlstm_pre_kernel(xp_ref, x_ref, xn_ref, mod_ref, wup_ref, cw_ref, cb_ref, wqk_ref, wv_ref, wg_ref, bg_ref,
                      wgt_ref, bgt_ref, q_ref, k_ref, v_ref, gates_ref, gatest_ref, *, geom, k_scale):
    d = x_ref.shape[1]
    hext = _halo_rows(xp_ref, x_ref, xn_ref, mod_ref, 0, d, geom).astype(MXU)
    xm_ext = _dot(hext, wup_ref[...])
    inner = xm_ext.shape[1]
    xc = _silu(_dwconv3(xm_ext, cw_ref, cb_ref, 0, inner)).astype(MXU)
    qk = _dot(xc, wqk_ref[...])
    half = qk.shape[1] // 2
    q_ref[...] = qk[:, :half].astype(q_ref.dtype)
    k_ref[...] = (qk[:, half:] * k_scale).astype(k_ref.dtype)
    v_ref[...] = _dot(xm_ext[HALO:HALO + TM].astype(MXU), wv_ref[...]).astype(v_ref.dtype)
    two_h = 2 * M_HEADS
    gates = _dot(xc, wg_ref[...]) + bg_ref[...]
    lane = lax.broadcasted_iota(jnp.int32, gates.shape, 1)
    gates_ref[...] = jnp.where(lane % two_h >= M_HEADS, _log_sigmoid(gates), gates)
    gates_t = _dot_nt(wgt_ref[...], xc) + bgt_ref[...]
    sub = lax.broadcasted_iota(jnp.int32, gates_t.shape, 0)
    gatest_ref[...] = jnp.where(sub % two_h >= M_HEADS, _log_sigmoid(gates_t), gates_t)


def _mlstm_pre(rs, x, mod, w_up, conv_w, conv_b, w_qk, w_v, w_gates, b_gates):
    d, inner = w_up.shape
    dqk_all = w_qk.shape[1] // 2
    n_g = 2 * 2 * M_HEADS
    wg = jnp.concatenate([w_gates[0], w_gates[1]], axis=1)
    wg_pad = jnp.pad(wg, ((0, 0), (0, LANES - n_g))).astype(MXU)
    bg = jnp.concatenate([b_gates[0], b_gates[1]])
    bg_pad = jnp.pad(bg, (0, LANES - n_g)).reshape(1, LANES)
    body = functools.partial(_mlstm_pre_kernel, geom=rs.geom, k_scale=float((dqk_all // M_HEADS) ** -0.5))
    n, r = rs.n_tiles, rs.rows
    vmem = 2 * (w_up.size + w_qk.size + w_v.size) + 8 * (TM + 2 * HALO) * inner * 4 + 16 * 2**20
    return _row_call(
        body, rs, n,
        [rs.halo_prev(d), rs.tile(d), rs.halo_next(d, x.shape[0]), rs.mod(6 * d), _const_spec(w_up.shape), _const_spec(conv_w.shape),
         _const_spec((1, inner)), _const_spec(w_qk.shape), _const_spec(w_v.shape), _const_spec((inner, LANES)),
         _const_spec((1, LANES)), _const_spec((n_g, inner)), _const_spec((n_g, 1))],
        [rs.tile(dqk_all), rs.tile(dqk_all), rs.tile(inner), rs.tile(LANES), pl.BlockSpec((n_g, TM), lambda i: (0, i))],
        [jax.ShapeDtypeStruct((r, dqk_all), MXU), jax.ShapeDtypeStruct((r, dqk_all), MXU),
         jax.ShapeDtypeStruct((r, inner), MXU), jax.ShapeDtypeStruct((r, LANES), F32),
         jax.ShapeDtypeStruct((n_g, r), F32)],
        vmem=vmem, name="mlstm_pre",
    )(x, x, x, mod, w_up, conv_w, conv_b.reshape(1, inner), w_qk, w_v, wg_pad, bg_pad, wg.T.astype(MXU),
      bg.reshape(n_g, 1))


def _scan_rowblock(batch, nc_ctx, nc_lat):
    def rowblk(b, d, c):
        cc = jnp.where(d == 1, nc_ctx - 1 - c, c)
        lc = c - nc_ctx
        lc = jnp.where(d == 1, nc_lat - 1 - lc, lc)
        return jnp.where(c < nc_ctx, batch * nc_lat + b * nc_ctx + cc, b * nc_lat + lc)
    return rowblk


def _mlstm_scan_kernel(q0_ref, k0_ref, v0_ref, g0_ref, gt0_ref, q1_ref, k1_ref, v1_ref, g1_ref, gt1_ref,
                       o0_ref, o1_ref, c_scr, n_scr, m_scr):
    @pl.when(pl.program_id(1) == 0)
    def _():
        c_scr[...] = jnp.zeros_like(c_scr)
        n_scr[...] = jnp.zeros_like(n_scr)
        m_scr[...] = jnp.zeros_like(m_scr)

    L = q0_ref.shape[0]
    dqk, dv = q0_ref.shape[1] // M_HEADS, v0_ref.shape[1] // M_HEADS
    two_h = 2 * M_HEADS
    dirs = ((q0_ref, k0_ref, v0_ref, g0_ref, gt0_ref, o0_ref), (q1_ref, k1_ref, v1_ref, g1_ref, gt1_ref, o1_ref))
    chains = [(d, h) for d in range(2) for h in range(M_HEADS)]
    row = lax.broadcasted_iota(jnp.int32, (L, L), 0)
    col = lax.broadcasted_iota(jnp.int32, (L, L), 1)
    causal = [col <= row, col >= row]
    causal_t = [row <= col, row >= col]
    gates = [dirs[d][3][...] for d in range(2)]
    gates_t = [dirs[d][4][...] for d in range(2)]
    cs = [sum(_dot(jnp.where(causal[d], 1.0, 0.0).astype(MXU), p) for p in _split3(gates[d])) for d in range(2)]
    cs_t = [sum(_dot(p, jnp.where(causal_t[d], 1.0, 0.0).astype(MXU)) for p in _split3(gates_t[d])) for d in range(2)]

    def per_chain(fn):
        return {ch: fn(*ch) for ch in chains}

    i_slot = lambda d, h: d * two_h + h
    f_slot = lambda d, h: d * two_h + M_HEADS + h
    qs = lambda h: slice(h * dqk, (h + 1) * dqk)
    vs = lambda h: slice(h * dv, (h + 1) * dv)
    c_prev = per_chain(lambda d, h: c_scr[d, h])
    n_prev = per_chain(lambda d, h: n_scr[d, h])
    m_prev = per_chain(lambda d, h: m_scr[d, h])
    q = per_chain(lambda d, h: dirs[d][0][:, qs(h)])
    k = per_chain(lambda d, h: dirs[d][1][:, qs(h)])
    v = per_chain(lambda d, h: dirs[d][2][:, vs(h)])
    li_col = per_chain(lambda d, h: gates[d][:, i_slot(d, h):i_slot(d, h) + 1])
    li_row = per_chain(lambda d, h: gates_t[d][i_slot(d, h):i_slot(d, h) + 1, :])
    b_col = per_chain(lambda d, h: cs[d][:, f_slot(d, h):f_slot(d, h) + 1])
    b_row = per_chain(lambda d, h: cs_t[d][f_slot(d, h):f_slot(d, h) + 1, :])
    bl = per_chain(lambda d, h: jnp.sum(gates_t[d][f_slot(d, h):f_slot(d, h) + 1, :], axis=1, keepdims=True))

    dmat = per_chain(lambda d, h: jnp.where(causal[d], b_col[d, h] - b_row[d, h] + li_row[d, h], -jnp.inf))
    inter = per_chain(lambda d, h: b_col[d, h] + m_prev[d, h])
    mj = per_chain(lambda d, h: jnp.maximum(inter[d, h], jnp.max(dmat[d, h], axis=1, keepdims=True)))
    qk = per_chain(lambda d, h: _dot_nt(q[d, h], k[d, h]))
    wmat = per_chain(lambda d, h: jnp.exp(dmat[d, h] - mj[d, h]) * qk[d, h])
    g = per_chain(lambda d, h: jnp.exp(inter[d, h] - mj[d, h]))
    qc = per_chain(lambda d, h: _dot(q[d, h], c_prev[d, h].astype(MXU)))
    wv = per_chain(lambda d, h: _dot(wmat[d, h].astype(MXU), v[d, h]))
    qn = per_chain(lambda d, h: jnp.sum(q[d, h].astype(F32) * n_prev[d, h], axis=1, keepdims=True))
    den = per_chain(lambda d, h: g[d, h] * qn[d, h] + jnp.sum(wmat[d, h], axis=1, keepdims=True))
    for d, h in chains:
        num = g[d, h] * qc[d, h] + wv[d, h]
        dirs[d][5][:, vs(h)] = num / jnp.maximum(jnp.abs(den[d, h]), jnp.exp(-mj[d, h]))

    ds = per_chain(lambda d, h: bl[d, h] - b_col[d, h] + li_col[d, h])
    m_new = per_chain(lambda d, h: jnp.maximum(bl[d, h] + m_prev[d, h], jnp.max(ds[d, h], axis=0, keepdims=True)))
    kw = per_chain(lambda d, h: k[d, h].astype(F32) * jnp.exp(ds[d, h] - m_new[d, h]))
    decay = per_chain(lambda d, h: jnp.exp(bl[d, h] + m_prev[d, h] - m_new[d, h]))
    kv = per_chain(lambda d, h: _dot_tn(kw[d, h].astype(MXU), v[d, h]))
    for d, h in chains:
        c_scr[d, h] = decay[d, h] * c_prev[d, h] + kv[d, h]
        n_scr[d, h] = decay[d, h] * n_prev[d, h] + jnp.sum(kw[d, h], axis=0, keepdims=True)
        m_scr[d, h] = m_new[d, h]


def _mlstm_scan(rs, q, k, v, gates, gates_t):
    b, L = rs.batch, M_CHUNK
    dqk_all, dv_all = q.shape[1], v.shape[1]
    nc_ctx, nc_lat = TM // L, rs.seq // L
    rowblk = _scan_rowblock(b, nc_ctx, nc_lat)
    n_g = gates_t.shape[0]

    def specs(d):
        rb = lambda b_, c: (rowblk(b_, d, c), 0)
        return [pl.BlockSpec((L, dqk_all), rb), pl.BlockSpec((L, dqk_all), rb), pl.BlockSpec((L, dv_all), rb),
                pl.BlockSpec((L, LANES), rb), pl.BlockSpec((n_g, L), lambda b_, c: (0, rowblk(b_, d, c)))]

    out_shape = jax.ShapeDtypeStruct((rs.rows, dv_all), F32)
    dqk, dv = dqk_all // M_HEADS, dv_all // M_HEADS
    return pl.pallas_call(
        _mlstm_scan_kernel, grid=(b, nc_ctx + nc_lat),
        in_specs=specs(0) + specs(1),
        out_specs=[pl.BlockSpec((L, dv_all), lambda b_, c: (rowblk(b_, 0, c), 0)),
                   pl.BlockSpec((L, dv_all), lambda b_, c: (rowblk(b_, 1, c), 0))],
        out_shape=[out_shape, out_shape],
        scratch_shapes=[pltpu.VMEM((2, M_HEADS, dqk, dv), F32), pltpu.VMEM((2, M_HEADS, 1, dqk), F32),
                        pltpu.VMEM((2, M_HEADS, 1, 1), F32)],
        compiler_params=pltpu.CompilerParams(dimension_semantics=("arbitrary",) * 2,
                                             vmem_limit_bytes=_vmem_limit(40 * 2**20)),
        name="mlstm_scan",
    )(q, k, v, gates, gates_t, q, k, v, gates, gates_t)


def _mlstm_post_kernel(x_ref, s0_ref, s1_ref, mod_ref, wog_ref, ng_ref, wout_ref, g_ref, b_ref, o_ref, *, alpha):
    d = x_ref.shape[1]
    h = _mod(x_ref[...], mod_ref, 0, d).astype(MXU)
    og = _sigmoid(_dot(h, wog_ref[...]))
    a = _head_rms(og * (s0_ref[...] + s1_ref[...]), ng_ref[...], M_HEADS).astype(MXU)
    _residual_ln(x_ref, mod_ref, _dot(a, wout_ref[...]), g_ref, b_ref, o_ref, alpha)


def _mlstm_post(rs, n_tiles, x, s, mod, w_og, norm_g, w_out, ln_g, ln_b, alpha):
    d, inner = w_og.shape
    body = functools.partial(_mlstm_post_kernel, alpha=alpha)
    vmem = 2 * (w_og.size + w_out.size) + 10 * TM * inner * 4 + 16 * 2**20
    return _row_call(
        body, rs, n_tiles,
        [rs.tile(d), rs.tile(inner), rs.tile(inner), rs.mod(6 * d), _const_spec(w_og.shape), _const_spec((1, inner)),
         _const_spec(w_out.shape), _const_spec((1, d)), _const_spec((1, d))],
        rs.tile(d), jax.ShapeDtypeStruct((n_tiles * TM, d), F32), vmem=vmem, name="mlstm_post",
    )(x, s[0], s[1], mod, w_og, norm_g.reshape(1, inner), w_out, ln_g.reshape(1, d), ln_b.reshape(1, d))


def _mla_pre_kernel(x_ref, mod_ref, cos_ref, sin_ref, wdq_ref, qn_ref, wuq_ref, wdkv_ref, kvn_ref, wk_ref, wv_ref,
                    q_ref, k_ref, v_ref, *, q_scale):
    d = x_ref.shape[1]
    hd = A_DNOPE + LANES
    h = _mod(x_ref[...], mod_ref, 0, d).astype(MXU)
    cos, sin = cos_ref[...], sin_ref[...]
    cq = _rms(_dot(h, wdq_ref[...]), qn_ref[...]).astype(MXU)
    qa = _dot(cq, wuq_ref[...])
    part0 = A_HEADS * hd
    cos_q, sin_q = cos * q_scale, sin * q_scale
    for hh in range(A_HEADS):
        q_ref[:, hh * hd:hh * hd + A_DNOPE] = (qa[:, hh * hd:hh * hd + A_DNOPE] * q_scale).astype(q_ref.dtype)
        rot = qa[:, hh * hd + A_DNOPE:(hh + 1) * hd] * cos_q + qa[:, part0 + hh * LANES:part0 + (hh + 1) * LANES] * sin_q
        q_ref[:, hh * hd + A_DNOPE:(hh + 1) * hd] = rot.astype(q_ref.dtype)
    dk = _dot(h, wdkv_ref[...])
    k_rope = (dk[:, A_KVLORA:A_KVLORA + LANES] * cos + dk[:, A_KVLORA + LANES:] * sin).astype(k_ref.dtype)
    ckv = _rms(dk[:, :A_KVLORA], kvn_ref[...]).astype(MXU)
    k_nope = _dot(ckv, wk_ref[...])
    for hh in range(A_HEADS):
        k_ref[:, hh * hd:hh * hd + A_DNOPE] = k_nope[:, hh * A_DNOPE:(hh + 1) * A_DNOPE].astype(k_ref.dtype)
        k_ref[:, hh * hd + A_DNOPE:(hh + 1) * hd] = k_rope
    v_ref[...] = _dot(ckv, wv_ref[...]).astype(v_ref.dtype)


def _rope_tables(seq):
    n_freq = A_DROPE // 4
    inv_freq = ROPE_BASE ** (-jnp.arange(n_freq, dtype=F32) / n_freq)
    pos = jnp.arange(seq)
    ang_row = (pos // GRID_W).astype(F32)[:, None] * inv_freq
    ang_col = (pos % GRID_W).astype(F32)[:, None] * inv_freq
    cos = jnp.concatenate([jnp.cos(ang_row)] * 2 + [jnp.cos(ang_col)] * 2, axis=1)
    sin = jnp.concatenate([-jnp.sin(ang_row), jnp.sin(ang_row), -jnp.sin(ang_col), jnp.sin(ang_col)], axis=1)
    pad = LANES - A_DROPE
    cos = jnp.pad(cos, ((0, 0), (0, pad)), constant_values=1.0)
    sin = jnp.pad(sin, ((0, 0), (0, pad)))
    ident = (jnp.ones((TM, LANES), F32), jnp.zeros((TM, LANES), F32))
    return jnp.concatenate([ident[0], cos]), jnp.concatenate([ident[1], sin])


def _rope_partner_cols(w):
    idx = np.arange(A_DROPE)
    half = A_DROPE // 4
    partner = np.where((idx % (2 * half)) < half, idx + half, idx - half)
    return w[..., partner]


def _mla_pre(rs, x, mod, w_dq, q_norm, w_uq, w_dkv, kv_norm, w_ukv):
    d, qlora = w_dq.shape
    hd = A_DNOPE + LANES
    pad = LANES - A_DROPE
    wq = w_uq.reshape(qlora, A_HEADS, A_DNOPE + A_DROPE)
    wq_main = jnp.pad(wq, ((0, 0), (0, 0), (0, pad))).reshape(qlora, A_HEADS * hd)
    wq_part = jnp.pad(_rope_partner_cols(wq[..., A_DNOPE:]), ((0, 0), (0, 0), (0, pad))).reshape(qlora, A_HEADS * LANES)
    wuq_all = jnp.concatenate([wq_main, wq_part], axis=1).astype(MXU)
    w_kr = w_dkv[:, A_KVLORA:]
    wdkv_all = jnp.concatenate([w_dkv[:, :A_KVLORA], jnp.pad(w_kr, ((0, 0), (0, pad))),
                                jnp.pad(_rope_partner_cols(w_kr), ((0, 0), (0, pad)))], axis=1).astype(MXU)
    wkv = w_ukv.reshape(A_KVLORA, A_HEADS, A_DNOPE + A_DV)
    w_k = wkv[..., :A_DNOPE].reshape(A_KVLORA, A_HEADS * A_DNOPE).astype(MXU)
    w_v = wkv[..., A_DNOPE:].reshape(A_KVLORA, A_HEADS * A_DV).astype(MXU)
    cos, sin = _rope_tables(rs.seq)
    nl, tps = rs.geom
    tab = pl.BlockSpec((TM, LANES), lambda i: (jnp.where(i >= nl, 0, 1 + i % tps), 0))
    n, r = rs.n_tiles, rs.rows
    vmem = 2 * 2 * (w_dq.size + wuq_all.size + wdkv_all.size + w_k.size + w_v.size) + 12 * TM * A_HEADS * hd * 4
    return _row_call(
        functools.partial(_mla_pre_kernel, q_scale=float((A_DNOPE + A_DROPE) ** -0.5 * np.log2(np.e))), rs, n,
        [rs.tile(d), rs.mod(6 * d), tab, tab, _const_spec(w_dq.shape), _const_spec((1, qlora)), _const_spec(wuq_all.shape),
         _const_spec(wdkv_all.shape), _const_spec((1, A_KVLORA)), _const_spec(w_k.shape), _const_spec(w_v.shape)],
        [rs.tile(A_HEADS * hd), rs.tile(A_HEADS * hd), rs.tile(A_HEADS * A_DV)],
        [jax.ShapeDtypeStruct((r, A_HEADS * hd), MXU), jax.ShapeDtypeStruct((r, A_HEADS * hd), MXU),
         jax.ShapeDtypeStruct((r, A_HEADS * A_DV), MXU)],
        vmem=vmem, name="mla_pre",
    )(x, mod, cos, sin, w_dq.astype(MXU), q_norm.reshape(1, qlora), wuq_all, wdkv_all, kv_norm.reshape(1, A_KVLORA), w_k, w_v)


def _attn_kernel(*refs, with_latent, kv_chunk):
    if with_latent:
        q_ref, kc_ref, vc_ref, kl_ref, vl_ref, _, o_ref = refs
    else:
        q_ref, kc_ref, vc_ref, o_ref = refs
    hd, dv = A_DNOPE + LANES, A_DV
    heads = range(q_ref.shape[1] // hd)
    qs = lambda h: slice(h * hd, (h + 1) * hd)
    vs = lambda h: slice(h * dv, (h + 1) * dv)
    q = [q_ref[:, qs(h)] for h in heads]
    s = [_dot_nt(q[h], kc_ref[:, qs(h)]) for h in heads]
    m = [jnp.max(s[h], axis=1, keepdims=True) for h in heads]
    p = [jnp.exp2(s[h] - m[h]) for h in heads]
    l = [jnp.sum(p[h], axis=1, keepdims=True) for h in heads]
    acc = [_dot(p[h].astype(MXU), vc_ref[:, vs(h)]) for h in heads]
    if with_latent:
        for c0 in range(0, kl_ref.shape[0], kv_chunk):
            s = [_dot_nt(q[h], kl_ref[c0:c0 + kv_chunk, qs(h)]) for h in heads]
            m_new = [jnp.maximum(m[h], jnp.max(s[h], axis=1, keepdims=True)) for h in heads]
            corr = [jnp.exp2(m[h] - m_new[h]) for h in heads]
            p = [jnp.exp2(s[h] - m_new[h]) for h in heads]
            l = [l[h] * corr[h] + jnp.sum(p[h], axis=1, keepdims=True) for h in heads]
            acc = [acc[h] * corr[h] + _dot(p[h].astype(MXU), vl_ref[c0:c0 + kv_chunk, vs(h)]) for h in heads]
            m = m_new
    for h in heads:
        o_ref[:, vs(h)] = (acc[h] / l[h]).astype(o_ref.dtype)


def _mla_attention(rs, q, k, v):
    b, t = rs.batch, rs.seq
    hpb = 2
    hd, dv = hpb * (A_DNOPE + LANES), hpb * A_DV
    kv_chunk = min(1024, t)
    assert t % kv_chunk == 0 and A_HEADS % hpb == 0
    ctx_blk0 = rs.n_lat_tiles
    params = pltpu.CompilerParams(dimension_semantics=("arbitrary",) * 3, vmem_limit_bytes=_vmem_limit(48 * 2**20))
    out_shape = jax.ShapeDtypeStruct((rs.rows, A_HEADS * A_DV), MXU)
    o = pl.pallas_call(
        functools.partial(_attn_kernel, with_latent=False, kv_chunk=kv_chunk), grid=(b, A_HEADS // hpb, 1),
        in_specs=[pl.BlockSpec((TM, hd), lambda b_, h, i: (ctx_blk0 + b_, h)),
                  pl.BlockSpec((TM, hd), lambda b_, h, i: (ctx_blk0 + b_, h)),
                  pl.BlockSpec((TM, dv), lambda b_, h, i: (ctx_blk0 + b_, h))],
        out_specs=pl.BlockSpec((TM, dv), lambda b_, h, i: (ctx_blk0 + b_, h)),
        out_shape=out_shape, compiler_params=params, name="mla_attn_ctx",
    )(q, k, v)
    tq = TM
    nq = t // tq
    return pl.pallas_call(
        functools.partial(_attn_kernel, with_latent=True, kv_chunk=kv_chunk), grid=(b, A_HEADS // hpb, nq),
        in_specs=[pl.BlockSpec((tq, hd), lambda b_, h, i: (b_ * nq + i, h)),
                  pl.BlockSpec((TM, hd), lambda b_, h, i: (ctx_blk0 + b_, h)),
                  pl.BlockSpec((TM, dv), lambda b_, h, i: (ctx_blk0 + b_, h)),
                  pl.BlockSpec((t, hd), lambda b_, h, i: (b_, h)),
                  pl.BlockSpec((t, dv), lambda b_, h, i: (b_, h)),
                  pl.BlockSpec(memory_space=pl.ANY)],
        out_specs=pl.BlockSpec((tq, dv), lambda b_, h, i: (b_ * nq + i, h)),
        out_shape=out_shape, input_output_aliases={5: 0}, compiler_params=params, name="mla_attn_latent",
    )(q, k, v, k, v, o)


def _proj_post_kernel(x_ref, a_ref, mod_ref, wout_ref, g_ref, b_ref, o_ref, *, alpha):
    _residual_ln(x_ref, mod_ref, _dot(a_ref[...], wout_ref[...]), g_ref, b_ref, o_ref, alpha)


def _proj_post(rs, n_tiles, x, a, mod, w_out, ln_g, ln_b, alpha):
    kdim, d = w_out.shape
    body = functools.partial(_proj_post_kernel, alpha=alpha)
    return _row_call(
        body, rs, n_tiles,
        [rs.tile(d), rs.tile(kdim), rs.mod(6 * d), _const_spec(w_out.shape), _const_spec((1, d)), _const_spec((1, d))],
        rs.tile(d), jax.ShapeDtypeStruct((n_tiles * TM, d), F32), vmem=32 * 2**20, name="proj_post",
    )(x, a, mod, w_out, ln_g.reshape(1, d), ln_b.reshape(1, d))


def _gla_pre_kernel(x_ref, mod_ref, wqk_ref, wv_ref, wa1_ref, wa2_ref, ba_ref, q_ref, k_ref, v_ref, la_ref,
                    *, q_scale):
    d = x_ref.shape[1]
    h = _mod(x_ref[...], mod_ref, 0, d).astype(MXU)
    qk = _dot(h, wqk_ref[...])
    half = qk.shape[1] // 2
    q_ref[...] = qk[:, :half] * q_scale
    k_ref[...] = qk[:, half:]
    v_ref[...] = _dot(h, wv_ref[...]).astype(v_ref.dtype)
    a1 = _dot(h, wa1_ref[...]).astype(MXU)
    z = _dot(a1, wa2_ref[...]) + ba_ref[...]
    la_ref[...] = _log_sigmoid(z) * (1.0 / G_TAU)


def _gla_pre(rs, x, mod, w_qk, w_v, w_a1, w_a2, b_a):
    d = w_qk.shape[0]
    dk_all = w_qk.shape[1] // 2
    wa1 = jnp.pad(jnp.concatenate([w_a1[0], w_a1[1]], axis=1), ((0, 0), (0, LANES - 2 * G_RANK))).astype(MXU)
    wa2 = jnp.zeros((LANES, 2 * dk_all), F32)
    wa2 = wa2.at[:G_RANK, :dk_all].set(w_a2[0]).at[G_RANK:2 * G_RANK, dk_all:].set(w_a2[1]).astype(MXU)
    ba = jnp.concatenate([b_a[0], b_a[1]]).reshape(1, 2 * dk_all)
    body = functools.partial(_gla_pre_kernel, q_scale=float((dk_all // G_HEADS) ** -0.5))
    n, r = rs.n_tiles, rs.rows
    return _row_call(
        body, rs, n,
        [rs.tile(d), rs.mod(6 * d), _const_spec(w_qk.shape), _const_spec(w_v.shape), _const_spec(wa1.shape),
         _const_spec(wa2.shape), _const_spec(ba.shape)],
        [rs.tile(dk_all), rs.tile(dk_all), rs.tile(w_v.shape[1]), rs.tile(2 * dk_all)],
        [jax.ShapeDtypeStruct((r, dk_all), F32), jax.ShapeDtypeStruct((r, dk_all), F32),
         jax.ShapeDtypeStruct((r, w_v.shape[1]), MXU), jax.ShapeDtypeStruct((r, 2 * dk_all), F32)],
        vmem=40 * 2**20, name="gla_pre",
    )(x, mod, w_qk, w_v, wa1, wa2, ba)


def _gla_tables(L):
    t = np.arange(L)
    tau, taup = t[:, None], t[None, :]
    groups = [taup <= tau, taup > tau]
    masks = []
    c = L // 2
    while c >= 1:
        blk = t // (2 * c)
        mid = blk * 2 * c + c
        second = (t % (2 * c)) >= c
        q_side = second[:, None] & (taup >= mid[:, None]) & (taup <= tau)
        k_side = (~second)[:, None] & (taup > tau) & (taup <= mid[:, None] - 1)
        groups.append(q_side | k_side)
        masks.append(second[:, None] & (~second)[None, :] & (blk[:, None] == blk[None, :]))
        c //= 2
    flip = lambda g: g[::-1, ::-1]
    sums = np.stack([np.concatenate(groups, axis=0), np.concatenate([flip(g) for g in groups], axis=0)])
    lvl = np.stack([np.stack(masks), np.stack([flip(m) for m in masks])])
    return sums.astype(np.float32), lvl.astype(np.float32)


def _gla_scan_kernel(q0_ref, k0_ref, v0_ref, la0_ref, q1_ref, k1_ref, v1_ref, la1_ref, sums_ref, masks_ref,
                     o0_ref, o1_ref, st_scr):
    @pl.when(pl.program_id(1) == 0)
    def _():
        st_scr[...] = jnp.zeros_like(st_scr)

    L = q0_ref.shape[0]
    dk, dv = q0_ref.shape[1] // G_HEADS, v0_ref.shape[1] // G_HEADS
    levels = masks_ref.shape[1]
    dirs = ((q0_ref, k0_ref, v0_ref, la0_ref, o0_ref), (q1_ref, k1_ref, v1_ref, la1_ref, o1_ref))
    chains = [(d, h) for d in range(2) for h in range(G_HEADS)]
    row = lax.broadcasted_iota(jnp.int32, (L, L), 0)
    col = lax.broadcasted_iota(jnp.int32, (L, L), 1)
    eye = row == col

    st_prev = {(d, h): st_scr[d, h] for d, h in chains}
    e = [jnp.exp(sum(_dot(sums_ref[d], p) for p in _split2(dirs[d][3][...]))) for d in range(2)]
    q = [dirs[d][0][...] for d in range(2)]
    k = [dirs[d][1][...] for d in range(2)]
    qe = [(q[d] * e[d][0:L]).astype(MXU) for d in range(2)]
    kd = [(k[d] * e[d][L:2 * L]).astype(MXU) for d in range(2)]
    qk_diag = [q[d] * k[d] for d in range(2)]
    qt = [[(q[d] * e[d][(2 + lv) * L:(3 + lv) * L]).astype(MXU) for lv in range(levels)] for d in range(2)]
    kt = [[(k[d] * e[d][(2 + lv) * L:(3 + lv) * L]).astype(MXU) for lv in range(levels)] for d in range(2)]
    decay = [jnp.exp(jnp.sum(dirs[d][3][...], axis=0, keepdims=True)) for d in range(2)]

    att = {}
    for d, h in chains:
        ks = slice(h * dk, (h + 1) * dk)
        a = jnp.where(eye, jnp.sum(qk_diag[d][:, ks], axis=1, keepdims=True), 0.0)
        for lv in range(levels):
            a = a + masks_ref[d, lv] * _dot_nt(qt[d][lv][:, ks], kt[d][lv][:, ks])
        att[d, h] = a.astype(MXU)
    for d, h in chains:
        ks, vs = slice(h * dk, (h + 1) * dk), slice(h * dv, (h + 1) * dv)
        v = dirs[d][2][:, vs]
        dirs[d][4][:, vs] = _dot_nt(qe[d][:, ks], st_prev[d, h].astype(MXU)) + _dot(att[d, h], v)
        st_scr[d, h] = st_prev[d, h] * decay[d][:, ks] + _dot_tn(v, kd[d][:, ks])


def _gla_scan(rs, q, k, v, la):
    b, L = rs.batch, G_CHUNK
    dk_all, dv_all = q.shape[1], v.shape[1]
    nc_ctx, nc_lat = TM // L, rs.seq // L
    rowblk = _scan_rowblock(b, nc_ctx, nc_lat)
    sums, masks = _gla_tables(L)
    sums, masks = jnp.asarray(sums, MXU), jnp.asarray(masks, F32)

    def specs(d):
        rb = lambda b_, c: (rowblk(b_, d, c), 0)
        return [pl.BlockSpec((L, dk_all), rb), pl.BlockSpec((L, dk_all), rb), pl.BlockSpec((L, dv_all), rb),
                pl.BlockSpec((L, dk_all), lambda b_, c: (rowblk(b_, d, c), d))]

    out_shape = jax.ShapeDtypeStruct((rs.rows, dv_all), F32)
    return pl.pallas_call(
        _gla_scan_kernel, grid=(b, nc_ctx + nc_lat),
        in_specs=specs(0) + specs(1) + [_const_spec(sums.shape), _const_spec(masks.shape)],
        out_specs=[pl.BlockSpec((L, dv_all), lambda b_, c: (rowblk(b_, 0, c), 0)),
                   pl.BlockSpec((L, dv_all), lambda b_, c: (rowblk(b_, 1, c), 0))],
        out_shape=[out_shape, out_shape],
        scratch_shapes=[pltpu.VMEM((2, G_HEADS, dv_all // G_HEADS, dk_all // G_HEADS), F32)],
        compiler_params=pltpu.CompilerParams(dimension_semantics=("arbitrary",) * 2,
                                             vmem_limit_bytes=_vmem_limit(32 * 2**20)),
        name="gla_scan",
    )(q, k, v, la, q, k, v, la, sums, masks)


def _gla_post_kernel(x_ref, s0_ref, s1_ref, mod_ref, wr_ref, ng_ref, wout_ref, g_ref, b_ref, o_ref, *, alpha):
    d = x_ref.shape[1]
    h = _mod(x_ref[...], mod_ref, 0, d).astype(MXU)
    r = _silu(_dot(h, wr_ref[...]))
    a = (_head_rms(s0_ref[...] + s1_ref[...], ng_ref[...], G_HEADS) * r).astype(MXU)
    _residual_ln(x_ref, mod_ref, _dot(a, wout_ref[...]), g_ref, b_ref, o_ref, alpha)


def _gla_post(rs, n_tiles, x, s, mod, w_r, norm_g, w_out, ln_g, ln_b, alpha):
    d, dv_all = w_r.shape
    body = functools.partial(_gla_post_kernel, alpha=alpha)
    return _row_call(
        body, rs, n_tiles,
        [rs.tile(d), rs.tile(dv_all), rs.tile(dv_all), rs.mod(6 * d), _const_spec(w_r.shape), _const_spec((1, dv_all)), _const_spec(w_out.shape),
         _const_spec((1, d)), _const_spec((1, d))],
        rs.tile(d), jax.ShapeDtypeStruct((n_tiles * TM, d), F32), vmem=40 * 2**20, name="gla_post",
    )(x, s[0], s[1], mod, w_r, norm_g.reshape(1, dv_all), w_out, ln_g.reshape(1, d), ln_b.reshape(1, d))


def kernel(x, c, ctx, c_ctx, ada_w, ada_b, ln_g, ln_b, ffn_w_in, ffn_conv_w, ffn_conv_b, ffn_w_out, m_w_up, m_conv_w, m_conv_b, m_w_qk, m_w_v, m_w_gates, m_b_gates, m_w_og, m_norm_g, m_w_out, a_w_dq, a_q_norm, a_w_uq, a_w_dkv, a_kv_norm, a_w_ukv, a_w_out, g_w_qk, g_w_v, g_w_r, g_w_a1, g_w_a2, g_b_a, g_norm_g, g_w_out):
    batch, seq, d = x.shape
    depth = ada_w.shape[0]
    n_mixers = 3
    alpha = float((2 * depth) ** 0.25)
    rs = _Rows(batch, seq, ctx.shape[1])
    bf = lambda w: w.astype(MXU)

    cond_rows = -(-(batch + 1) // SUBLANES) * SUBLANES
    cond = jnp.zeros((cond_rows, d), F32).at[:batch].set(c).at[batch].set(c_ctx)
    mods = _modulation(cond, bf(ada_w), ada_b).reshape(depth, cond_rows, 1, 6 * d)

    xa = jnp.concatenate([x.reshape(batch * seq, d), ctx.reshape(-1, d)], axis=0)
    for i in range(depth):
        need_ctx = i < depth - 1
        n_tiles = rs.n_tiles if need_ctx else rs.n_lat_tiles
        kind, j = i % n_mixers, i // n_mixers
        mod = mods[i]
        if kind == 0:
            q, k, v, gates, gates_t = _mlstm_pre(rs, xa, mod, bf(m_w_up[j]), m_conv_w[j], m_conv_b[j], bf(m_w_qk[j]),
                                                 bf(m_w_v[j]), m_w_gates[j], m_b_gates[j])
            s = _mlstm_scan(rs, q, k, v, gates, gates_t)
            xa = _mlstm_post(rs, n_tiles, xa, s, mod, bf(m_w_og[j]), m_norm_g[j], bf(m_w_out[j]), ln_g[i, 0], ln_b[i, 0], alpha)
        elif kind == 1:
            q, k, v = _mla_pre(rs, xa, mod, a_w_dq[j], a_q_norm[j], a_w_uq[j], a_w_dkv[j], a_kv_norm[j], a_w_ukv[j])
            o = _mla_attention(rs, q, k, v)
            xa = _proj_post(rs, n_tiles, xa, o, mod, bf(a_w_out[j]), ln_g[i, 0], ln_b[i, 0], alpha)
        else:
            q, k, v, la = _gla_pre(rs, xa, mod, bf(g_w_qk[j]), bf(g_w_v[j]), g_w_a1[j], g_w_a2[j], g_b_a[j])
            s = _gla_scan(rs, q, k, v, la)
            xa = _gla_post(rs, n_tiles, xa, s, mod, bf(g_w_r[j]), g_norm_g[j], bf(g_w_out[j]), ln_g[i, 0], ln_b[i, 0], alpha)
        xa = _ffn(rs, n_tiles, xa, mod, bf(ffn_w_in[i]), ffn_conv_w[i], ffn_conv_b[i], bf(ffn_w_out[i]),
                  ln_g[i, 1], ln_b[i, 1], alpha)
    return xa.reshape(batch, seq, d)
```

```python
import functools

import numpy as np
import jax
import jax.numpy as jnp
from jax import lax
from jax.experimental import pallas as pl
from jax.experimental.pallas import tpu as pltpu

F32 = jnp.float32
MXU = jnp.bfloat16

V7X_VMEM_BYTES = 64 * 2**20
SUBLANES = 8
LANES = 128

TM = 512
ATTN_TQ = 512
HALO = SUBLANES
GRID_W = 64
EPS = 1e-6
ROPE_BASE = 10000.0
G_TAU = 16.0

M_HEADS, A_HEADS, G_HEADS = 4, 8, 4
A_DNOPE, A_DROPE, A_DV, A_KVLORA = 128, 64, 128, 256
G_RANK = 16
M_CHUNK = 256
G_CHUNK = 128


def _vmem_limit(nbytes):
    return int(min(max(nbytes, 16 * 2**20), V7X_VMEM_BYTES - 8 * 2**20))


def _const_spec(shape):
    nd = len(shape)
    return pl.BlockSpec(shape, lambda *_: (0,) * nd, pipeline_mode=pl.Buffered(1))


def _dot(a, b):
    return jnp.dot(a, b, preferred_element_type=F32)


def _dot_nt(a, b):
    return lax.dot_general(a, b, (((1,), (1,)), ((), ())), preferred_element_type=F32)


def _dot_tn(a, b):
    return lax.dot_general(a, b, (((0,), (0,)), ((), ())), preferred_element_type=F32)


def _split3(x):
    hi = x.astype(MXU)
    r1 = x - hi.astype(F32)
    mid = r1.astype(MXU)
    lo = (r1 - mid.astype(F32)).astype(MXU)
    return hi, mid, lo


def _split2(x):
    hi = x.astype(MXU)
    return hi, (x - hi.astype(F32)).astype(MXU)


def _sigmoid(x):
    return 1.0 / (1.0 + jnp.exp(-x))


def _silu(x):
    return x * _sigmoid(x)


def _log_sigmoid(x):
    return jnp.minimum(x, 0.0) - jnp.log1p(jnp.exp(-jnp.abs(x)))


def _layer_norm(z, g, b):
    mu = jnp.mean(z, -1, keepdims=True)
    zc = z - mu
    var = jnp.mean(zc * zc, -1, keepdims=True)
    return zc * lax.rsqrt(var + EPS) * g + b


def _rms(x, g):
    return x * lax.rsqrt(jnp.mean(x * x, -1, keepdims=True) + EPS) * g


def _head_rms(x, g, heads):
    d = x.shape[-1] // heads
    return jnp.concatenate([_rms(x[:, h * d:(h + 1) * d], g[:, h * d:(h + 1) * d]) for h in range(heads)], axis=-1)


def _mod(x, mod_ref, k, d):
    return x * (1.0 + mod_ref[:, (k + 1) * d:(k + 2) * d]) + mod_ref[:, k * d:(k + 1) * d]


def _halo_rows(xp_ref, x_ref, xn_ref, mod_ref, k, d, geom):
    n_lat_tiles, tiles_per_seq, _ = geom
    i = pl.program_id(0)
    is_ctx = i >= n_lat_tiles
    pos = i % tiles_per_seq
    first = jnp.logical_or(is_ctx, pos == 0)
    last = jnp.logical_or(is_ctx, pos == tiles_per_seq - 1)
    hp = jnp.where(first, 0.0, _mod(xp_ref[...], mod_ref, k, d))
    hn = jnp.where(last, 0.0, _mod(xn_ref[...], mod_ref, k, d))
    return jnp.concatenate([hp, _mod(x_ref[...], mod_ref, k, d), hn], axis=0)


def _seq_edges(geom):
    n_lat_tiles, _, ctx_len = geom
    is_ctx = pl.program_id(0) >= n_lat_tiles
    r = lax.broadcasted_iota(jnp.int32, (TM, 1), 0) % ctx_len
    return jnp.logical_and(is_ctx, r == 0), jnp.logical_and(is_ctx, r == ctx_len - 1)


def _dwconv3(g_ext, w_ref, b_ref, c0, c1, edges):
    n = g_ext.shape[0] - 2 * HALO
    starts, ends = edges
    prev = jnp.where(starts, 0.0, g_ext[HALO - 1:HALO - 1 + n])
    nxt = jnp.where(ends, 0.0, g_ext[HALO + 1:HALO + 1 + n])
    return (w_ref[0:1, c0:c1] * prev + w_ref[1:2, c0:c1] * g_ext[HALO:HALO + n] + w_ref[2:3, c0:c1] * nxt
            + b_ref[:, c0:c1])


class _Rows:
    def __init__(self, batch, seq, ctx_len):
        assert TM % ctx_len == 0 and (batch * ctx_len) % TM == 0 and seq % TM == 0
        self.batch, self.seq, self.ctx_len = batch, seq, ctx_len
        self.tiles_per_seq = seq // TM
        self.n_lat_tiles = batch * self.tiles_per_seq
        self.n_tiles = self.n_lat_tiles + batch * ctx_len // TM
        self.rows = self.n_tiles * TM
        self.geom = (self.n_lat_tiles, self.tiles_per_seq, ctx_len)

    def tile(self, width):
        return pl.BlockSpec((TM, width), lambda i: (i, 0))

    def halo_prev(self, width):
        per = TM // HALO
        return pl.BlockSpec((HALO, width), lambda i: (jnp.maximum(i * per - 1, 0), 0))

    def halo_next(self, width, n_rows):
        per, nblk = TM // HALO, n_rows // HALO
        return pl.BlockSpec((HALO, width), lambda i: (jnp.minimum((i + 1) * per, nblk - 1), 0))

    def mod(self, width):
        nl, tps, b = self.n_lat_tiles, self.tiles_per_seq, self.batch
        return pl.BlockSpec((None, 1, width), lambda i: (jnp.where(i >= nl, b, i // tps), 0, 0))


def _row_call(body, rows, n_tiles, in_specs, out_specs, out_shape, scratch=(), vmem=0, name=None):
    return pl.pallas_call(
        body, grid=(n_tiles,), in_specs=in_specs, out_specs=out_specs, out_shape=out_shape,
        scratch_shapes=list(scratch), name=name,
        compiler_params=pltpu.CompilerParams(dimension_semantics=("arbitrary",), vmem_limit_bytes=_vmem_limit(vmem)))


def _modulation_kernel(c_ref, w_ref, b_ref, o_ref):
    o_ref[...] = _dot(_silu(c_ref[...]).astype(MXU), w_ref[...]) + b_ref[...]


def _modulation(cond, ada_w, ada_b):
    depth, d, n = ada_w.shape
    tn = d
    return pl.pallas_call(
        _modulation_kernel, grid=(depth, n // tn),
        in_specs=[pl.BlockSpec(cond.shape, lambda l, j: (0, 0)),
                  pl.BlockSpec((None, d, tn), lambda l, j: (l, 0, j)),
                  pl.BlockSpec((None, 1, tn), lambda l, j: (l, 0, j))],
        out_specs=pl.BlockSpec((None, cond.shape[0], tn), lambda l, j: (l, 0, j)),
        out_shape=jax.ShapeDtypeStruct((depth, cond.shape[0], n), F32), name="modulation",
    )(cond, ada_w, ada_b.reshape(depth, 1, n))


def _ffn_kernel(xp_ref, x_ref, xn_ref, mod_ref, win_ref, cw_ref, cb_ref, wout_ref, g_ref, b_ref, o_ref, a_scr,
                *, geom, alpha, ffn, tf):
    d = x_ref.shape[1]
    hext = _halo_rows(xp_ref, x_ref, xn_ref, mod_ref, 3, d, geom).astype(MXU)
    edges = _seq_edges(geom)
    for f0 in range(0, ffn, tf):
        g_ext = _dot(hext, win_ref[:, f0:f0 + tf])
        up = _dot(hext, win_ref[:, ffn + f0:ffn + f0 + tf])[HALO:HALO + TM]
        a_scr[:, f0:f0 + tf] = (_silu(_dwconv3(g_ext, cw_ref, cb_ref, f0, f0 + tf, edges)) * up).astype(MXU)
    f = _dot(a_scr[...], wout_ref[...])
    z = alpha * x_ref[...] + mod_ref[:, 5 * d:6 * d] * f
    o_ref[...] = _layer_norm(z, g_ref[...], b_ref[...])


def _ffn(rs, n_tiles, x, mod, w_in, conv_w, conv_b, w_out, ln_g, ln_b, alpha):
    d, ffn = w_out.shape[1], w_out.shape[0]
    tf = 256
    assert ffn % tf == 0
    body = functools.partial(_ffn_kernel, geom=rs.geom, alpha=alpha, ffn=ffn, tf=tf)
    vmem = 2 * (w_in.size + w_out.size) + 6 * TM * d * 4 + TM * ffn * 2 + 24 * 2**20
    return _row_call(
        body, rs, n_tiles,
        [rs.halo_prev(d), rs.tile(d), rs.halo_next(d, x.shape[0]), rs.mod(6 * d), _const_spec(w_in.shape), _const_spec(conv_w.shape),
         _const_spec((1, ffn)), _const_spec(w_out.shape), _const_spec((1, d)), _const_spec((1, d))],
        rs.tile(d), jax.ShapeDtypeStruct((n_tiles * TM, d), F32),
        scratch=[pltpu.VMEM((TM, ffn), MXU)], vmem=vmem, name="conv_ffn",
    )(x, x, x, mod, w_in, conv_w, conv_b.reshape(1, ffn), w_out, ln_g.reshape(1, d), ln_b.reshape(1, d))


def _residual_ln(x_ref, mod_ref, y, g_ref, b_ref, o_ref, alpha):
    d = x_ref.shape[1]
    z = alpha * x_ref[...] + mod_ref[:, 2 * d:3 * d] * y
    o_ref[...] = _layer_norm(z, g_ref[...], b_ref[...])


def _mlstm_pre_kernel(xp_ref, x_ref, xn_ref, mod_ref, wup_ref, cw_ref, cb_ref, wqk_ref, wv_ref, wg_ref, bg_ref,
                      q_ref, k_ref, v_ref, gates_ref, *, geom, k_scale):
    d = x_ref.shape[1]
    hext = _halo_rows(xp_ref, x_ref, xn_ref, mod_ref, 0, d, geom).astype(MXU)
    xm_ext = _dot(hext, wup_ref[...])
    inner = xm_ext.shape[1]
    xc = _silu(_dwconv3(xm_ext, cw_ref, cb_ref, 0, inner, _seq_edges(geom))).astype(MXU)
    qk = _dot(xc, wqk_ref[...])
    half = qk.shape[1] // 2
    q_ref[...] = qk[:, :half].astype(q_ref.dtype)
    k_ref[...] = (qk[:, half:] * k_scale).astype(k_ref.dtype)
    v_ref[...] = _dot(xm_ext[HALO:HALO + TM].astype(MXU), wv_ref[...]).astype(v_ref.dtype)
    two_h = 2 * M_HEADS
    gates = _dot(xc, wg_ref[...]) + bg_ref[...]
    lane = lax.broadcasted_iota(jnp.int32, gates.shape, 1)
    gates_ref[...] = jnp.where(lane % two_h >= M_HEADS, _log_sigmoid(gates), gates)


def _mlstm_pre(rs, x, mod, w_up, conv_w, conv_b, w_qk, w_v, w_gates, b_gates):
    d, inner = w_up.shape
    dqk_all = w_qk.shape[1] // 2
    n_g = 2 * 2 * M_HEADS
    wg = jnp.concatenate([w_gates[0], w_gates[1]], axis=1)
    wg_pad = jnp.pad(wg, ((0, 0), (0, LANES - n_g))).astype(MXU)
    bg = jnp.concatenate([b_gates[0], b_gates[1]])
    bg_pad = jnp.pad(bg, (0, LANES - n_g)).reshape(1, LANES)
    body = functools.partial(_mlstm_pre_kernel, geom=rs.geom, k_scale=float((dqk_all // M_HEADS) ** -0.5))
    n, r = rs.n_tiles, rs.rows
    vmem = 2 * (w_up.size + w_qk.size + w_v.size) + 8 * (TM + 2 * HALO) * inner * 4 + 16 * 2**20
    return _row_call(
        body, rs, n,
        [rs.halo_prev(d), rs.tile(d), rs.halo_next(d, x.shape[0]), rs.mod(6 * d), _const_spec(w_up.shape), _const_spec(conv_w.shape),
         _const_spec((1, inner)), _const_spec(w_qk.shape), _const_spec(w_v.shape), _const_spec((inner, LANES)),
         _const_spec((1, LANES))],
        [rs.tile(dqk_all), rs.tile(dqk_all), rs.tile(inner), rs.tile(LANES)],
        [jax.ShapeDtypeStruct((r, dqk_all), MXU), jax.ShapeDtypeStruct((r, dqk_all), MXU),
         jax.ShapeDtypeStruct((r, inner), MXU), jax.ShapeDtypeStruct((r, LANES), F32)],
        vmem=vmem, name="mlstm_pre",
    )(x, x, x, mod, w_up, conv_w, conv_b.reshape(1, inner), w_qk, w_v, wg_pad, bg_pad)


def _scan_rowblock(batch, nc_ctx, nc_lat):
    def rowblk(b, d, c):
        cc = jnp.where(d == 1, nc_ctx - 1 - c, c)
        lc = c - nc_ctx
        lc = jnp.where(d == 1, nc_lat - 1 - lc, lc)
        return jnp.where(c < nc_ctx, batch * nc_lat + b * nc_ctx + cc, b * nc_lat + lc)
    return rowblk


def _mlstm_scan_kernel(q0_ref, k0_ref, v0_ref, g0_ref, q1_ref, k1_ref, v1_ref, g1_ref,
                       o0_ref, o1_ref, c_scr, n_scr, m_scr):
    @pl.when(pl.program_id(1) == 0)
    def _():
        c_scr[...] = jnp.zeros_like(c_scr)
        n_scr[...] = jnp.zeros_like(n_scr)
        m_scr[...] = jnp.zeros_like(m_scr)

    L = q0_ref.shape[0]
    dqk, dv = q0_ref.shape[1] // M_HEADS, v0_ref.shape[1] // M_HEADS
    two_h = 2 * M_HEADS
    dirs = ((q0_ref, k0_ref, v0_ref, g0_ref, o0_ref), (q1_ref, k1_ref, v1_ref, g1_ref, o1_ref))
    chains = [(d, h) for d in range(2) for h in range(M_HEADS)]
    row = lax.broadcasted_iota(jnp.int32, (L, L), 0)
    col = lax.broadcasted_iota(jnp.int32, (L, L), 1)
    causal = [col <= row, col >= row]
    causal_t = [row <= col, row >= col]
    eye = jnp.where(row == col, 1.0, 0.0).astype(MXU)
    gates = [dirs[d][3][...] for d in range(2)]
    pieces = [_split3(gates[d]) for d in range(2)]
    cs = [sum(_dot(jnp.where(causal[d], 1.0, 0.0).astype(MXU), p) for p in pieces[d]) for d in range(2)]
    cs_t = [sum(_dot_tn(p, jnp.where(causal_t[d], 1.0, 0.0).astype(MXU)) for p in pieces[d]) for d in range(2)]
    gates_t = [sum(_dot_tn(p, eye) for p in pieces[d]) for d in range(2)]

    def per_chain(fn):
        return {ch: fn(*ch) for ch in chains}

    i_slot = lambda d, h: d * two_h + h
    f_slot = lambda d, h: d * two_h + M_HEADS + h
    qs = lambda h: slice(h * dqk, (h + 1) * dqk)
    vs = lambda h: slice(h * dv, (h + 1) * dv)
    c_prev = per_chain(lambda d, h: c_scr[d, h])
    n_prev = per_chain(lambda d, h: n_scr[d, h])
    m_prev = per_chain(lambda d, h: m_scr[d, h])
    q = per_chain(lambda d, h: dirs[d][0][:, qs(h)])
    k = per_chain(lambda d, h: dirs[d][1][:, qs(h)])
    v = per_chain(lambda d, h: dirs[d][2][:, vs(h)])
    li_col = per_chain(lambda d, h: gates[d][:, i_slot(d, h):i_slot(d, h) + 1])
    li_row = per_chain(lambda d, h: gates_t[d][i_slot(d, h):i_slot(d, h) + 1, :])
    b_col = per_chain(lambda d, h: cs[d][:, f_slot(d, h):f_slot(d, h) + 1])
    b_row = per_chain(lambda d, h: cs_t[d][f_slot(d, h):f_slot(d, h) + 1, :])
    bl = per_chain(lambda d, h: jnp.sum(gates_t[d][f_slot(d, h):f_slot(d, h) + 1, :], axis=1, keepdims=True))

    dmat = per_chain(lambda d, h: jnp.where(causal[d], b_col[d, h] - b_row[d, h] + li_row[d, h], -jnp.inf))
    inter = per_chain(lambda d, h: b_col[d, h] + m_prev[d, h])
    mj = per_chain(lambda d, h: jnp.maximum(inter[d, h], jnp.max(dmat[d, h], axis=1, keepdims=True)))
    qk = per_chain(lambda d, h: _dot_nt(q[d, h], k[d, h]))
    wmat = per_chain(lambda d, h: jnp.exp(dmat[d, h] - mj[d, h]) * qk[d, h])
    g = per_chain(lambda d, h: jnp.exp(inter[d, h] - mj[d, h]))
    qc = per_chain(lambda d, h: _dot(q[d, h], c_prev[d, h].astype(MXU)))
    wv = per_chain(lambda d, h: _dot(wmat[d, h].astype(MXU), v[d, h]))
    qn = per_chain(lambda d, h: jnp.sum(q[d, h].astype(F32) * n_prev[d, h], axis=1, keepdims=True))
    den = per_chain(lambda d, h: g[d, h] * qn[d, h] + jnp.sum(wmat[d, h], axis=1, keepdims=True))
    for d, h in chains:
        num = g[d, h] * qc[d, h] + wv[d, h]
        dirs[d][4][:, vs(h)] = num / jnp.maximum(jnp.abs(den[d, h]), jnp.exp(-mj[d, h]))

    ds = per_chain(lambda d, h: bl[d, h] - b_col[d, h] + li_col[d, h])
    m_new = per_chain(lambda d, h: jnp.maximum(bl[d, h] + m_prev[d, h], jnp.max(ds[d, h], axis=0, keepdims=True)))
    kw = per_chain(lambda d, h: k[d, h].astype(F32) * jnp.exp(ds[d, h] - m_new[d, h]))
    decay = per_chain(lambda d, h: jnp.exp(bl[d, h] + m_prev[d, h] - m_new[d, h]))
    kv = per_chain(lambda d, h: _dot_tn(kw[d, h].astype(MXU), v[d, h]))
    for d, h in chains:
        c_scr[d, h] = decay[d, h] * c_prev[d, h] + kv[d, h]
        n_scr[d, h] = decay[d, h] * n_prev[d, h] + jnp.sum(kw[d, h], axis=0, keepdims=True)
        m_scr[d, h] = m_new[d, h]


def _mlstm_scan(rs, q, k, v, gates):
    b, L = rs.batch, M_CHUNK
    dqk_all, dv_all = q.shape[1], v.shape[1]
    nc_ctx, nc_lat = rs.ctx_len // L, rs.seq // L
    rowblk = _scan_rowblock(b, nc_ctx, nc_lat)

    def specs(d):
        rb = lambda b_, c: (rowblk(b_, d, c), 0)
        return [pl.BlockSpec((L, dqk_all), rb), pl.BlockSpec((L, dqk_all), rb), pl.BlockSpec((L, dv_all), rb),
                pl.BlockSpec((L, LANES), rb)]

    out_shape = jax.ShapeDtypeStruct((rs.rows, dv_all), F32)
    dqk, dv = dqk_all // M_HEADS, dv_all // M_HEADS
    return pl.pallas_call(
        _mlstm_scan_kernel, grid=(b, nc_ctx + nc_lat),
        in_specs=specs(0) + specs(1),
        out_specs=[pl.BlockSpec((L, dv_all), lambda b_, c: (rowblk(b_, 0, c), 0)),
                   pl.BlockSpec((L, dv_all), lambda b_, c: (rowblk(b_, 1, c), 0))],
        out_shape=[out_shape, out_shape],
        scratch_shapes=[pltpu.VMEM((2, M_HEADS, dqk, dv), F32), pltpu.VMEM((2, M_HEADS, 1, dqk), F32),
                        pltpu.VMEM((2, M_HEADS, 1, 1), F32)],
        compiler_params=pltpu.CompilerParams(dimension_semantics=("arbitrary",) * 2,
                                             vmem_limit_bytes=_vmem_limit(40 * 2**20)),
        name="mlstm_scan",
    )(q, k, v, gates, q, k, v, gates)


def _mlstm_post_kernel(x_ref, s0_ref, s1_ref, mod_ref, wog_ref, ng_ref, wout_ref, g_ref, b_ref, o_ref, *, alpha):
    d = x_ref.shape[1]
    h = _mod(x_ref[...], mod_ref, 0, d).astype(MXU)
    og = _sigmoid(_dot(h, wog_ref[...]))
    a = _head_rms(og * (s0_ref[...] + s1_ref[...]), ng_ref[...], M_HEADS).astype(MXU)
    _residual_ln(x_ref, mod_ref, _dot(a, wout_ref[...]), g_ref, b_ref, o_ref, alpha)


def _mlstm_post(rs, n_tiles, x, s, mod, w_og, norm_g, w_out, ln_g, ln_b, alpha):
    d, inner = w_og.shape
    body = functools.partial(_mlstm_post_kernel, alpha=alpha)
    vmem = 2 * (w_og.size + w_out.size) + 10 * TM * inner * 4 + 16 * 2**20
    return _row_call(
        body, rs, n_tiles,
        [rs.tile(d), rs.tile(inner), rs.tile(inner), rs.mod(6 * d), _const_spec(w_og.shape), _const_spec((1, inner)),
         _const_spec(w_out.shape), _const_spec((1, d)), _const_spec((1, d))],
        rs.tile(d), jax.ShapeDtypeStruct((n_tiles * TM, d), F32), vmem=vmem, name="mlstm_post",
    )(x, s[0], s[1], mod, w_og, norm_g.reshape(1, inner), w_out, ln_g.reshape(1, d), ln_b.reshape(1, d))


def _mla_pre_kernel(x_ref, mod_ref, cos_ref, sin_ref, wdq_ref, qn_ref, wuq_ref, wdkv_ref, kvn_ref, wk_ref, wv_ref,
                    q_ref, k_ref, v_ref, *, q_scale):
    d = x_ref.shape[1]
    hd = A_DNOPE + LANES
    h = _mod(x_ref[...], mod_ref, 0, d).astype(MXU)
    cos, sin = cos_ref[...], sin_ref[...]
    cq = _rms(_dot(h, wdq_ref[...]), qn_ref[...]).astype(MXU)
    qa = _dot(cq, wuq_ref[...])
    part0 = A_HEADS * hd
    cos_q, sin_q = cos * q_scale, sin * q_scale
    for hh in range(A_HEADS):
        q_ref[:, hh * hd:hh * hd + A_DNOPE] = (qa[:, hh * hd:hh * hd + A_DNOPE] * q_scale).astype(q_ref.dtype)
        rot = qa[:, hh * hd + A_DNOPE:(hh + 1) * hd] * cos_q + qa[:, part0 + hh * LANES:part0 + (hh + 1) * LANES] * sin_q
        q_ref[:, hh * hd + A_DNOPE:(hh + 1) * hd] = rot.astype(q_ref.dtype)
    dk = _dot(h, wdkv_ref[...])
    k_rope = (dk[:, A_KVLORA:A_KVLORA + LANES] * cos + dk[:, A_KVLORA + LANES:] * sin).astype(k_ref.dtype)
    ckv = _rms(dk[:, :A_KVLORA], kvn_ref[...]).astype(MXU)
    k_nope = _dot(ckv, wk_ref[...])
    for hh in range(A_HEADS):
        k_ref[:, hh * hd:hh * hd + A_DNOPE] = k_nope[:, hh * A_DNOPE:(hh + 1) * A_DNOPE].astype(k_ref.dtype)
        k_ref[:, hh * hd + A_DNOPE:(hh + 1) * hd] = k_rope
    v_ref[...] = _dot(ckv, wv_ref[...]).astype(v_ref.dtype)


def _rope_tables(seq):
    n_freq = A_DROPE // 4
    inv_freq = ROPE_BASE ** (-jnp.arange(n_freq, dtype=F32) / n_freq)
    pos = jnp.arange(seq)
    ang_row = (pos // GRID_W).astype(F32)[:, None] * inv_freq
    ang_col = (pos % GRID_W).astype(F32)[:, None] * inv_freq
    cos = jnp.concatenate([jnp.cos(ang_row)] * 2 + [jnp.cos(ang_col)] * 2, axis=1)
    sin = jnp.concatenate([-jnp.sin(ang_row), jnp.sin(ang_row), -jnp.sin(ang_col), jnp.sin(ang_col)], axis=1)
    pad = LANES - A_DROPE
    cos = jnp.pad(cos, ((0, 0), (0, pad)), constant_values=1.0)
    sin = jnp.pad(sin, ((0, 0), (0, pad)))
    ident = (jnp.ones((TM, LANES), F32), jnp.zeros((TM, LANES), F32))
    return jnp.concatenate([ident[0], cos]), jnp.concatenate([ident[1], sin])


def _rope_partner_cols(w):
    idx = np.arange(A_DROPE)
    half = A_DROPE // 4
    partner = np.where((idx % (2 * half)) < half, idx + half, idx - half)
    return w[..., partner]


def _mla_pre(rs, x, mod, w_dq, q_norm, w_uq, w_dkv, kv_norm, w_ukv):
    d, qlora = w_dq.shape
    hd = A_DNOPE + LANES
    pad = LANES - A_DROPE
    wq = w_uq.reshape(qlora, A_HEADS, A_DNOPE + A_DROPE)
    wq_main = jnp.pad(wq, ((0, 0), (0, 0), (0, pad))).reshape(qlora, A_HEADS * hd)
    wq_part = jnp.pad(_rope_partner_cols(wq[..., A_DNOPE:]), ((0, 0), (0, 0), (0, pad))).reshape(qlora, A_HEADS * LANES)
    wuq_all = jnp.concatenate([wq_main, wq_part], axis=1).astype(MXU)
    w_kr = w_dkv[:, A_KVLORA:]
    wdkv_all = jnp.concatenate([w_dkv[:, :A_KVLORA], jnp.pad(w_kr, ((0, 0), (0, pad))),
                                jnp.pad(_rope_partner_cols(w_kr), ((0, 0), (0, pad)))], axis=1).astype(MXU)
    wkv = w_ukv.reshape(A_KVLORA, A_HEADS, A_DNOPE + A_DV)
    w_k = wkv[..., :A_DNOPE].reshape(A_KVLORA, A_HEADS * A_DNOPE).astype(MXU)
    w_v = wkv[..., A_DNOPE:].reshape(A_KVLORA, A_HEADS * A_DV).astype(MXU)
    cos, sin = _rope_tables(rs.seq)
    nl, tps = rs.n_lat_tiles, rs.tiles_per_seq
    tab = pl.BlockSpec((TM, LANES), lambda i: (jnp.where(i >= nl, 0, 1 + i % tps), 0))
    n, r = rs.n_tiles, rs.rows
    vmem = 2 * 2 * (w_dq.size + wuq_all.size + wdkv_all.size + w_k.size + w_v.size) + 12 * TM * A_HEADS * hd * 4
    return _row_call(
        functools.partial(_mla_pre_kernel, q_scale=float((A_DNOPE + A_DROPE) ** -0.5 * np.log2(np.e))), rs, n,
        [rs.tile(d), rs.mod(6 * d), tab, tab, _const_spec(w_dq.shape), _const_spec((1, qlora)), _const_spec(wuq_all.shape),
         _const_spec(wdkv_all.shape), _const_spec((1, A_KVLORA)), _const_spec(w_k.shape), _const_spec(w_v.shape)],
        [rs.tile(A_HEADS * hd), rs.tile(A_HEADS * hd), rs.tile(A_HEADS * A_DV)],
        [jax.ShapeDtypeStruct((r, A_HEADS * hd), MXU), jax.ShapeDtypeStruct((r, A_HEADS * hd), MXU),
         jax.ShapeDtypeStruct((r, A_HEADS * A_DV), MXU)],
        vmem=vmem, name="mla_pre",
    )(x, mod, cos, sin, w_dq.astype(MXU), q_norm.reshape(1, qlora), wuq_all, wdkv_all, kv_norm.reshape(1, A_KVLORA), w_k, w_v)


def _attn_kernel(*refs, with_latent, kv_chunk):
    if with_latent:
        q_ref, kc_ref, vc_ref, kl_ref, vl_ref, _, o_ref = refs
    else:
        q_ref, kc_ref, vc_ref, o_ref = refs
    hd, dv = A_DNOPE + LANES, A_DV
    heads = range(q_ref.shape[1] // hd)
    qs = lambda h: slice(h * hd, (h + 1) * hd)
    vs = lambda h: slice(h * dv, (h + 1) * dv)
    q = [q_ref[:, qs(h)] for h in heads]
    s = [_dot_nt(q[h], kc_ref[:, qs(h)]) for h in heads]
    m = [jnp.max(s[h], axis=1, keepdims=True) for h in heads]
    p = [jnp.exp2(s[h] - m[h]) for h in heads]
    l = [jnp.sum(p[h], axis=1, keepdims=True) for h in heads]
    acc = [_dot(p[h].astype(MXU), vc_ref[:, vs(h)]) for h in heads]
    if with_latent:
        for c0 in range(0, kl_ref.shape[0], kv_chunk):
            s = [_dot_nt(q[h], kl_ref[c0:c0 + kv_chunk, qs(h)]) for h in heads]
            m_new = [jnp.maximum(m[h], jnp.max(s[h], axis=1, keepdims=True)) for h in heads]
            corr = [jnp.exp2(m[h] - m_new[h]) for h in heads]
            p = [jnp.exp2(s[h] - m_new[h]) for h in heads]
            l = [l[h] * corr[h] + jnp.sum(p[h], axis=1, keepdims=True) for h in heads]
            acc = [acc[h] * corr[h] + _dot(p[h].astype(MXU), vl_ref[c0:c0 + kv_chunk, vs(h)]) for h in heads]
            m = m_new
    for h in heads:
        o_ref[:, vs(h)] = (acc[h] / l[h]).astype(o_ref.dtype)


def _mla_attention(rs, q, k, v):
    b, t = rs.batch, rs.seq
    hpb = 2
    hd, dv = hpb * (A_DNOPE + LANES), hpb * A_DV
    kv_chunk = min(1024, t)
    assert t % kv_chunk == 0 and A_HEADS % hpb == 0
    tc = rs.ctx_len
    ctx_blk0 = b * t // tc
    params = pltpu.CompilerParams(dimension_semantics=("arbitrary",) * 3, vmem_limit_bytes=_vmem_limit(48 * 2**20))
    out_shape = jax.ShapeDtypeStruct((rs.rows, A_HEADS * A_DV), MXU)
    o = pl.pallas_call(
        functools.partial(_attn_kernel, with_latent=False, kv_chunk=kv_chunk), grid=(b, A_HEADS // hpb, 1),
        in_specs=[pl.BlockSpec((tc, hd), lambda b_, h, i: (ctx_blk0 + b_, h)),
                  pl.BlockSpec((tc, hd), lambda b_, h, i: (ctx_blk0 + b_, h)),
                  pl.BlockSpec((tc, dv), lambda b_, h, i: (ctx_blk0 + b_, h))],
        out_specs=pl.BlockSpec((tc, dv), lambda b_, h, i: (ctx_blk0 + b_, h)),
        out_shape=out_shape, compiler_params=params, name="mla_attn_ctx",
    )(q, k, v)
    tq = ATTN_TQ
    nq = t // tq
    return pl.pallas_call(
        functools.partial(_attn_kernel, with_latent=True, kv_chunk=kv_chunk), grid=(b, A_HEADS // hpb, nq),
        in_specs=[pl.BlockSpec((tq, hd), lambda b_, h, i: (b_ * nq + i, h)),
                  pl.BlockSpec((tc, hd), lambda b_, h, i: (ctx_blk0 + b_, h)),
                  pl.BlockSpec((tc, dv), lambda b_, h, i: (ctx_blk0 + b_, h)),
                  pl.BlockSpec((t, hd), lambda b_, h, i: (b_, h)),
                  pl.BlockSpec((t, dv), lambda b_, h, i: (b_, h)),
                  pl.BlockSpec(memory_space=pl.ANY)],
        out_specs=pl.BlockSpec((tq, dv), lambda b_, h, i: (b_ * nq + i, h)),
        out_shape=out_shape, input_output_aliases={5: 0}, compiler_params=params, name="mla_attn_latent",
    )(q, k, v, k, v, o)


def _proj_post_kernel(x_ref, a_ref, mod_ref, wout_ref, g_ref, b_ref, o_ref, *, alpha):
    _residual_ln(x_ref, mod_ref, _dot(a_ref[...], wout_ref[...]), g_ref, b_ref, o_ref, alpha)


def _proj_post(rs, n_tiles, x, a, mod, w_out, ln_g, ln_b, alpha):
    kdim, d = w_out.shape
    body = functools.partial(_proj_post_kernel, alpha=alpha)
    return _row_call(
        body, rs, n_tiles,
        [rs.tile(d), rs.tile(kdim), rs.mod(6 * d), _const_spec(w_out.shape), _const_spec((1, d)), _const_spec((1, d))],
        rs.tile(d), jax.ShapeDtypeStruct((n_tiles * TM, d), F32), vmem=32 * 2**20, name="proj_post",
    )(x, a, mod, w_out, ln_g.reshape(1, d), ln_b.reshape(1, d))


def _gla_pre_kernel(x_ref, mod_ref, wqk_ref, wv_ref, wa1_ref, wa2_ref, ba_ref, q_ref, k_ref, v_ref, la_ref,
                    *, q_scale):
    d = x_ref.shape[1]
    h = _mod(x_ref[...], mod_ref, 0, d).astype(MXU)
    qk = _dot(h, wqk_ref[...])
    half = qk.shape[1] // 2
    q_ref[...] = qk[:, :half] * q_scale
    k_ref[...] = qk[:, half:]
    v_ref[...] = _dot(h, wv_ref[...]).astype(v_ref.dtype)
    a1 = _dot(h, wa1_ref[...]).astype(MXU)
    z = _dot(a1, wa2_ref[...]) + ba_ref[...]
    la_ref[...] = _log_sigmoid(z) * (1.0 / G_TAU)


def _gla_pre(rs, x, mod, w_qk, w_v, w_a1, w_a2, b_a):
    d = w_qk.shape[0]
    dk_all = w_qk.shape[1] // 2
    wa1 = jnp.pad(jnp.concatenate([w_a1[0], w_a1[1]], axis=1), ((0, 0), (0, LANES - 2 * G_RANK))).astype(MXU)
    wa2 = jnp.zeros((LANES, 2 * dk_all), F32)
    wa2 = wa2.at[:G_RANK, :dk_all].set(w_a2[0]).at[G_RANK:2 * G_RANK, dk_all:].set(w_a2[1]).astype(MXU)
    ba = jnp.concatenate([b_a[0], b_a[1]]).reshape(1, 2 * dk_all)
    body = functools.partial(_gla_pre_kernel, q_scale=float((dk_all // G_HEADS) ** -0.5))
    n, r = rs.n_tiles, rs.rows
    return _row_call(
        body, rs, n,
        [rs.tile(d), rs.mod(6 * d), _const_spec(w_qk.shape), _const_spec(w_v.shape), _const_spec(wa1.shape),
         _const_spec(wa2.shape), _const_spec(ba.shape)],
        [rs.tile(dk_all), rs.tile(dk_all), rs.tile(w_v.shape[1]), rs.tile(2 * dk_all)],
        [jax.ShapeDtypeStruct((r, dk_all), F32), jax.ShapeDtypeStruct((r, dk_all), F32),
         jax.ShapeDtypeStruct((r, w_v.shape[1]), MXU), jax.ShapeDtypeStruct((r, 2 * dk_all), F32)],
        vmem=40 * 2**20, name="gla_pre",
    )(x, mod, w_qk, w_v, wa1, wa2, ba)


def _gla_tables(L):
    t = np.arange(L)
    tau, taup = t[:, None], t[None, :]
    groups = [taup <= tau, taup > tau]
    masks = []
    c = L // 2
    while c >= 1:
        blk = t // (2 * c)
        mid = blk * 2 * c + c
        second = (t % (2 * c)) >= c
        q_side = second[:, None] & (taup >= mid[:, None]) & (taup <= tau)
        k_side = (~second)[:, None] & (taup > tau) & (taup <= mid[:, None] - 1)
        groups.append(q_side | k_side)
        masks.append(second[:, None] & (~second)[None, :] & (blk[:, None] == blk[None, :]))
        c //= 2
    flip = lambda g: g[::-1, ::-1]
    sums = np.stack([np.concatenate(groups, axis=0), np.concatenate([flip(g) for g in groups], axis=0)])
    lvl = np.stack([np.stack(masks), np.stack([flip(m) for m in masks])])
    return sums.astype(np.float32), lvl.astype(np.float32)


def _gla_scan_kernel(q0_ref, k0_ref, v0_ref, la0_ref, q1_ref, k1_ref, v1_ref, la1_ref, sums_ref, masks_ref,
                     o0_ref, o1_ref, st_scr):
    @pl.when(pl.program_id(1) == 0)
    def _():
        st_scr[...] = jnp.zeros_like(st_scr)

    L = q0_ref.shape[0]
    dk, dv = q0_ref.shape[1] // G_HEADS, v0_ref.shape[1] // G_HEADS
    levels = masks_ref.shape[1]
    dirs = ((q0_ref, k0_ref, v0_ref, la0_ref, o0_ref), (q1_ref, k1_ref, v1_ref, la1_ref, o1_ref))
    chains = [(d, h) for d in range(2) for h in range(G_HEADS)]
    row = lax.broadcasted_iota(jnp.int32, (L, L), 0)
    col = lax.broadcasted_iota(jnp.int32, (L, L), 1)
    eye = row == col

    st_prev = {(d, h): st_scr[d, h] for d, h in chains}
    e = [jnp.exp(sum(_dot(sums_ref[d], p) for p in _split2(dirs[d][3][...]))) for d in range(2)]
    q = [dirs[d][0][...] for d in range(2)]
    k = [dirs[d][1][...] for d in range(2)]
    qe = [(q[d] * e[d][0:L]).astype(MXU) for d in range(2)]
    kd = [(k[d] * e[d][L:2 * L]).astype(MXU) for d in range(2)]
    qk_diag = [q[d] * k[d] for d in range(2)]
    qt = [[(q[d] * e[d][(2 + lv) * L:(3 + lv) * L]).astype(MXU) for lv in range(levels)] for d in range(2)]
    kt = [[(k[d] * e[d][(2 + lv) * L:(3 + lv) * L]).astype(MXU) for lv in range(levels)] for d in range(2)]
    decay = [jnp.exp(jnp.sum(dirs[d][3][...], axis=0, keepdims=True)) for d in range(2)]

    att = {}
    for d, h in chains:
        ks = slice(h * dk, (h + 1) * dk)
        a = jnp.where(eye, jnp.sum(qk_diag[d][:, ks], axis=1, keepdims=True), 0.0)
        for lv in range(levels):
            a = a + masks_ref[d, lv] * _dot_nt(qt[d][lv][:, ks], kt[d][lv][:, ks])
        att[d, h] = a.astype(MXU)
    for d, h in chains:
        ks, vs = slice(h * dk, (h + 1) * dk), slice(h * dv, (h + 1) * dv)
        v = dirs[d][2][:, vs]
        dirs[d][4][:, vs] = _dot_nt(qe[d][:, ks], st_prev[d, h].astype(MXU)) + _dot(att[d, h], v)
        st_scr[d, h] = st_prev[d, h] * decay[d][:, ks] + _dot_tn(v, kd[d][:, ks])


def _gla_scan(rs, q, k, v, la):
    b, L = rs.batch, G_CHUNK
    dk_all, dv_all = q.shape[1], v.shape[1]
    nc_ctx, nc_lat = rs.ctx_len // L, rs.seq // L
    rowblk = _scan_rowblock(b, nc_ctx, nc_lat)
    sums, masks = _gla_tables(L)
    sums, masks = jnp.asarray(sums, MXU), jnp.asarray(masks, F32)

    def specs(d):
        rb = lambda b_, c: (rowblk(b_, d, c), 0)
        return [pl.BlockSpec((L, dk_all), rb), pl.BlockSpec((L, dk_all), rb), pl.BlockSpec((L, dv_all), rb),
                pl.BlockSpec((L, dk_all), lambda b_, c: (rowblk(b_, d, c), d))]

    out_shape = jax.ShapeDtypeStruct((rs.rows, dv_all), F32)
    return pl.pallas_call(
        _gla_scan_kernel, grid=(b, nc_ctx + nc_lat),
        in_specs=specs(0) + specs(1) + [_const_spec(sums.shape), _const_spec(masks.shape)],
        out_specs=[pl.BlockSpec((L, dv_all), lambda b_, c: (rowblk(b_, 0, c), 0)),
                   pl.BlockSpec((L, dv_all), lambda b_, c: (rowblk(b_, 1, c), 0))],
        out_shape=[out_shape, out_shape],
        scratch_shapes=[pltpu.VMEM((2, G_HEADS, dv_all // G_HEADS, dk_all // G_HEADS), F32)],
        compiler_params=pltpu.CompilerParams(dimension_semantics=("arbitrary",) * 2,
                                             vmem_limit_bytes=_vmem_limit(32 * 2**20)),
        name="gla_scan",
    )(q, k, v, la, q, k, v, la, sums, masks)


def _gla_post_kernel(x_ref, s0_ref, s1_ref, mod_ref, wr_ref, ng_ref, wout_ref, g_ref, b_ref, o_ref, *, alpha):
    d = x_ref.shape[1]
    h = _mod(x_ref[...], mod_ref, 0, d).astype(MXU)
    r = _silu(_dot(h, wr_ref[...]))
    a = (_head_rms(s0_ref[...] + s1_ref[...], ng_ref[...], G_HEADS) * r).astype(MXU)
    _residual_ln(x_ref, mod_ref, _dot(a, wout_ref[...]), g_ref, b_ref, o_ref, alpha)


def _gla_post(rs, n_tiles, x, s, mod, w_r, norm_g, w_out, ln_g, ln_b, alpha):
    d, dv_all = w_r.shape
    body = functools.partial(_gla_post_kernel, alpha=alpha)
    return _row_call(
        body, rs, n_tiles,
        [rs.tile(d), rs.tile(dv_all), rs.tile(dv_all), rs.mod(6 * d), _const_spec(w_r.shape), _const_spec((1, dv_all)), _const_spec(w_out.shape),
         _const_spec((1, d)), _const_spec((1, d))],
        rs.tile(d), jax.ShapeDtypeStruct((n_tiles * TM, d), F32), vmem=40 * 2**20, name="gla_post",
    )(x, s[0], s[1], mod, w_r, norm_g.reshape(1, dv_all), w_out, ln_g.reshape(1, d), ln_b.reshape(1, d))


def kernel(x, c, ctx, c_ctx, ada_w, ada_b, ln_g, ln_b, ffn_w_in, ffn_conv_w, ffn_conv_b, ffn_w_out, m_w_up, m_conv_w, m_conv_b, m_w_qk, m_w_v, m_w_gates, m_b_gates, m_w_og, m_norm_g, m_w_out, a_w_dq, a_q_norm, a_w_uq, a_w_dkv, a_kv_norm, a_w_ukv, a_w_out, g_w_qk, g_w_v, g_w_r, g_w_a1, g_w_a2, g_b_a, g_norm_g, g_w_out):
    batch, seq, d = x.shape
    depth = ada_w.shape[0]
    n_mixers = 3
    alpha = float((2 * depth) ** 0.25)
    rs = _Rows(batch, seq, ctx.shape[1])
    bf = lambda w: w.astype(MXU)

    cond_rows = -(-(batch + 1) // SUBLANES) * SUBLANES
    cond = jnp.zeros((cond_rows, d), F32).at[:batch].set(c).at[batch].set(c_ctx)
    mods = _modulation(cond, bf(ada_w), ada_b).reshape(depth, cond_rows, 1, 6 * d)

    xa = jnp.concatenate([x.reshape(batch * seq, d), ctx.reshape(-1, d)], axis=0)
    for i in range(depth):
        need_ctx = i < depth - 1
        n_tiles = rs.n_tiles if need_ctx else rs.n_lat_tiles
        kind, j = i % n_mixers, i // n_mixers
        mod = mods[i]
        if kind == 0:
            q, k, v, gates = _mlstm_pre(rs, xa, mod, bf(m_w_up[j]), m_conv_w[j], m_conv_b[j], bf(m_w_qk[j]),
                                                 bf(m_w_v[j]), m_w_gates[j], m_b_gates[j])
            s = _mlstm_scan(rs, q, k, v, gates)
            xa = _mlstm_post(rs, n_tiles, xa, s, mod, bf(m_w_og[j]), m_norm_g[j], bf(m_w_out[j]), ln_g[i, 0], ln_b[i, 0], alpha)
        elif kind == 1:
            q, k, v = _mla_pre(rs, xa, mod, a_w_dq[j], a_q_norm[j], a_w_uq[j], a_w_dkv[j], a_kv_norm[j], a_w_ukv[j])
            o = _mla_attention(rs, q, k, v)
            xa = _proj_post(rs, n_tiles, xa, o, mod, bf(a_w_out[j]), ln_g[i, 0], ln_b[i, 0], alpha)
        else:
            q, k, v, la = _gla_pre(rs, xa, mod, bf(g_w_qk[j]), bf(g_w_v[j]), g_w_a1[j], g_w_a2[j], g_b_a[j])
            s = _gla_scan(rs, q, k, v, la)
            xa = _gla_post(rs, n_tiles, xa, s, mod, bf(g_w_r[j]), g_norm_g[j], bf(g_w_out[j]), ln_g[i, 0], ln_b[i, 0], alpha)
        xa = _ffn(rs, n_tiles, xa, mod, bf(ffn_w_in[i]), ffn_conv_w[i], ffn_conv_b[i], bf(ffn_w_out[i]),
                  ln_g[i, 1], ln_b[i, 1], alpha)
    return xa.reshape(batch, seq, d)
```

```python
import functools

import numpy as np
import jax
import jax.numpy as jnp
from jax import lax
from jax.experimental import pallas as pl
from jax.experimental.pallas import tpu as pltpu

F32 = jnp.float32
MXU = jnp.bfloat16

V7X_VMEM_BYTES = 64 * 2**20
SUBLANES = 8
LANES = 128

TM = 512
ATTN_TQ = 512
HALO = SUBLANES
GRID_W = 64
EPS = 1e-6
ROPE_BASE = 10000.0
G_TAU = 16.0

M_HEADS, A_HEADS, G_HEADS = 4, 8, 4
A_DNOPE, A_DROPE, A_DV, A_KVLORA = 128, 64, 128, 256
G_RANK = 16
M_CHUNK = 256
M_SCAN_GROUP = 8
G_CHUNK = 128


def _vmem_limit(nbytes):
    return int(min(max(nbytes, 16 * 2**20), V7X_VMEM_BYTES - 8 * 2**20))


def _const_spec(shape):
    nd = len(shape)
    return pl.BlockSpec(shape, lambda *_: (0,) * nd, pipeline_mode=pl.Buffered(1))


def _dot(a, b):
    return jnp.dot(a, b, preferred_element_type=F32)


def _dot_nt(a, b):
    return lax.dot_general(a, b, (((1,), (1,)), ((), ())), preferred_element_type=F32)


def _dot_tn(a, b):
    return lax.dot_general(a, b, (((0,), (0,)), ((), ())), preferred_element_type=F32)


def _split3(x):
    hi = x.astype(MXU)
    r1 = x - hi.astype(F32)
    mid = r1.astype(MXU)
    lo = (r1 - mid.astype(F32)).astype(MXU)
    return hi, mid, lo


def _split2(x):
    hi = x.astype(MXU)
    return hi, (x - hi.astype(F32)).astype(MXU)


def _sigmoid(x):
    return 1.0 / (1.0 + jnp.exp(-x))


def _silu(x):
    return x * _sigmoid(x)


def _log_sigmoid(x):
    return jnp.minimum(x, 0.0) - jnp.log1p(jnp.exp(-jnp.abs(x)))


def _layer_norm(z, g, b):
    mu = jnp.mean(z, -1, keepdims=True)
    zc = z - mu
    var = jnp.mean(zc * zc, -1, keepdims=True)
    return zc * lax.rsqrt(var + EPS) * g + b


def _rms(x, g):
    return x * lax.rsqrt(jnp.mean(x * x, -1, keepdims=True) + EPS) * g


def _head_rms(x, g, heads):
    d = x.shape[-1] // heads
    return jnp.concatenate([_rms(x[:, h * d:(h + 1) * d], g[:, h * d:(h + 1) * d]) for h in range(heads)], axis=-1)


def _mod(x, mod_ref, k, d):
    return x * (1.0 + mod_ref[:, (k + 1) * d:(k + 2) * d]) + mod_ref[:, k * d:(k + 1) * d]


def _halo_rows(xp_ref, x_ref, xn_ref, mod_ref, k, d, geom):
    n_lat_tiles, tiles_per_seq, _ = geom
    i = pl.program_id(0)
    is_ctx = i >= n_lat_tiles
    pos = i % tiles_per_seq
    first = jnp.logical_or(is_ctx, pos == 0)
    last = jnp.logical_or(is_ctx, pos == tiles_per_seq - 1)
    hp = jnp.where(first, 0.0, _mod(xp_ref[...], mod_ref, k, d))
    hn = jnp.where(last, 0.0, _mod(xn_ref[...], mod_ref, k, d))
    return jnp.concatenate([hp, _mod(x_ref[...], mod_ref, k, d), hn], axis=0)


def _seq_edges(geom):
    n_lat_tiles, _, ctx_len = geom
    is_ctx = pl.program_id(0) >= n_lat_tiles
    r = lax.broadcasted_iota(jnp.int32, (TM, 1), 0) % ctx_len
    return jnp.logical_and(is_ctx, r == 0), jnp.logical_and(is_ctx, r == ctx_len - 1)


def _dwconv3(g_ext, w_ref, b_ref, c0, c1, edges):
    n = g_ext.shape[0] - 2 * HALO
    starts, ends = edges
    prev = jnp.where(starts, 0.0, g_ext[HALO - 1:HALO - 1 + n])
    nxt = jnp.where(ends, 0.0, g_ext[HALO + 1:HALO + 1 + n])
    return (w_ref[0:1, c0:c1] * prev + w_ref[1:2, c0:c1] * g_ext[HALO:HALO + n] + w_ref[2:3, c0:c1] * nxt
            + b_ref[:, c0:c1])


class _Rows:
    def __init__(self, batch, seq, ctx_len):
        assert TM % ctx_len == 0 and (batch * ctx_len) % TM == 0 and seq % TM == 0
        self.batch, self.seq, self.ctx_len = batch, seq, ctx_len
        self.tiles_per_seq = seq // TM
        self.n_lat_tiles = batch * self.tiles_per_seq
        self.n_tiles = self.n_lat_tiles + batch * ctx_len // TM
        self.rows = self.n_tiles * TM
        self.geom = (self.n_lat_tiles, self.tiles_per_seq, ctx_len)

    def tile(self, width):
        return pl.BlockSpec((TM, width), lambda i: (i, 0))

    def halo_prev(self, width):
        per = TM // HALO
        return pl.BlockSpec((HALO, width), lambda i: (jnp.maximum(i * per - 1, 0), 0))

    def halo_next(self, width, n_rows):
        per, nblk = TM // HALO, n_rows // HALO
        return pl.BlockSpec((HALO, width), lambda i: (jnp.minimum((i + 1) * per, nblk - 1), 0))

    def mod(self, width):
        nl, tps, b = self.n_lat_tiles, self.tiles_per_seq, self.batch
        return pl.BlockSpec((None, 1, width), lambda i: (jnp.where(i >= nl, b, i // tps), 0, 0))


def _row_call(body, rows, n_tiles, in_specs, out_specs, out_shape, scratch=(), vmem=0, name=None):
    return pl.pallas_call(
        body, grid=(n_tiles,), in_specs=in_specs, out_specs=out_specs, out_shape=out_shape,
        scratch_shapes=list(scratch), name=name,
        compiler_params=pltpu.CompilerParams(dimension_semantics=("arbitrary",), vmem_limit_bytes=_vmem_limit(vmem)))


def _modulation_kernel(c_ref, w_ref, b_ref, o_ref):
    o_ref[...] = _dot(_silu(c_ref[...]).astype(MXU), w_ref[...]) + b_ref[...]


def _modulation(cond, ada_w, ada_b):
    depth, d, n = ada_w.shape
    tn = d
    return pl.pallas_call(
        _modulation_kernel, grid=(depth, n // tn),
        in_specs=[pl.BlockSpec(cond.shape, lambda l, j: (0, 0)),
                  pl.BlockSpec((None, d, tn), lambda l, j: (l, 0, j)),
                  pl.BlockSpec((None, 1, tn), lambda l, j: (l, 0, j))],
        out_specs=pl.BlockSpec((None, cond.shape[0], tn), lambda l, j: (l, 0, j)),
        out_shape=jax.ShapeDtypeStruct((depth, cond.shape[0], n), F32), name="modulation",
    )(cond, ada_w, ada_b.reshape(depth, 1, n))


def _ffn_kernel(xp_ref, x_ref, xn_ref, mod_ref, win_ref, cw_ref, cb_ref, wout_ref, g_ref, b_ref, o_ref, a_scr,
                *, geom, alpha, ffn, tf):
    d = x_ref.shape[1]
    hext = _halo_rows(xp_ref, x_ref, xn_ref, mod_ref, 3, d, geom).astype(MXU)
    edges = _seq_edges(geom)
    for f0 in range(0, ffn, tf):
        g_ext = _dot(hext, win_ref[:, f0:f0 + tf])
        up = _dot(hext, win_ref[:, ffn + f0:ffn + f0 + tf])[HALO:HALO + TM]
        a_scr[:, f0:f0 + tf] = (_silu(_dwconv3(g_ext, cw_ref, cb_ref, f0, f0 + tf, edges)) * up).astype(MXU)
    f = _dot(a_scr[...], wout_ref[...])
    z = alpha * x_ref[...] + mod_ref[:, 5 * d:6 * d] * f
    o_ref[...] = _layer_norm(z, g_ref[...], b_ref[...])


def _ffn(rs, n_tiles, x, mod, w_in, conv_w, conv_b, w_out, ln_g, ln_b, alpha):
    d, ffn = w_out.shape[1], w_out.shape[0]
    tf = 256
    assert ffn % tf == 0
    body = functools.partial(_ffn_kernel, geom=rs.geom, alpha=alpha, ffn=ffn, tf=tf)
    vmem = 2 * (w_in.size + w_out.size) + 6 * TM * d * 4 + TM * ffn * 2 + 24 * 2**20
    return _row_call(
        body, rs, n_tiles,
        [rs.halo_prev(d), rs.tile(d), rs.halo_next(d, x.shape[0]), rs.mod(6 * d), _const_spec(w_in.shape), _const_spec(conv_w.shape),
         _const_spec((1, ffn)), _const_spec(w_out.shape), _const_spec((1, d)), _const_spec((1, d))],
        rs.tile(d), jax.ShapeDtypeStruct((n_tiles * TM, d), F32),
        scratch=[pltpu.VMEM((TM, ffn), MXU)], vmem=vmem, name="conv_ffn",
    )(x, x, x, mod, w_in, conv_w, conv_b.reshape(1, ffn), w_out, ln_g.reshape(1, d), ln_b.reshape(1, d))


def _residual_ln(x_ref, mod_ref, y, g_ref, b_ref, o_ref, alpha):
    d = x_ref.shape[1]
    z = alpha * x_ref[...] + mod_ref[:, 2 * d:3 * d] * y
    o_ref[...] = _layer_norm(z, g_ref[...], b_ref[...])


def _weight_product_kernel(a_ref, b_ref, o_ref):
    o_ref[...] = _dot(a_ref[...].astype(MXU), b_ref[...].astype(MXU)).astype(o_ref.dtype)


def _weight_product(a, b):
    m, kdim = a.shape
    n = b.shape[1]
    tn = 512
    assert n % tn == 0
    return pl.pallas_call(
        _weight_product_kernel, grid=(n // tn,),
        in_specs=[pl.BlockSpec((m, kdim), lambda j: (0, 0)), pl.BlockSpec((kdim, tn), lambda j: (0, j))],
        out_specs=pl.BlockSpec((m, tn), lambda j: (0, j)),
        out_shape=jax.ShapeDtypeStruct((m, n), MXU),
        compiler_params=pltpu.CompilerParams(dimension_semantics=("arbitrary",), vmem_limit_bytes=_vmem_limit(40 * 2**20)),
        name="weight_product",
    )(a, b)


def _mlstm_pre_kernel(xp_ref, x_ref, xn_ref, mod_ref, wup_ref, cw_ref, cb_ref, wqk_ref, wv_ref, wg_ref, bg_ref,
                      q_ref, k_ref, v_ref, gates_ref, *, geom, k_scale):
    d = x_ref.shape[1]
    hext = _halo_rows(xp_ref, x_ref, xn_ref, mod_ref, 0, d, geom).astype(MXU)
    xm_ext = _dot(hext, wup_ref[...])
    inner = xm_ext.shape[1]
    xc = _silu(_dwconv3(xm_ext, cw_ref, cb_ref, 0, inner, _seq_edges(geom))).astype(MXU)
    qk = _dot(xc, wqk_ref[...])
    half = qk.shape[1] // 2
    q_ref[...] = qk[:, :half].astype(q_ref.dtype)
    k_ref[...] = (qk[:, half:] * k_scale).astype(k_ref.dtype)
    v_ref[...] = _dot(_mod(x_ref[...], mod_ref, 0, d).astype(MXU), wv_ref[...]).astype(v_ref.dtype)
    two_h = 2 * M_HEADS
    gates = _dot(xc, wg_ref[...]) + bg_ref[...]
    lane = lax.broadcasted_iota(jnp.int32, gates.shape, 1)
    gates_ref[...] = jnp.where(lane % two_h >= M_HEADS, _log_sigmoid(gates), gates)


def _mlstm_pre(rs, x, mod, w_up, conv_w, conv_b, w_qk, w_v, w_gates, b_gates):
    d, inner = w_up.shape
    dqk_all = w_qk.shape[1] // 2
    n_g = 2 * 2 * M_HEADS
    wg = jnp.concatenate([w_gates[0], w_gates[1]], axis=1)
    wg_pad = jnp.pad(wg, ((0, 0), (0, LANES - n_g))).astype(MXU)
    bg = jnp.concatenate([b_gates[0], b_gates[1]])
    bg_pad = jnp.pad(bg, (0, LANES - n_g)).reshape(1, LANES)
    body = functools.partial(_mlstm_pre_kernel, geom=rs.geom, k_scale=float((dqk_all // M_HEADS) ** -0.5))
    n, r = rs.n_tiles, rs.rows
    vmem = 2 * (w_up.size + w_qk.size + w_v.size) + 8 * (TM + 2 * HALO) * inner * 4 + 16 * 2**20
    return _row_call(
        body, rs, n,
        [rs.halo_prev(d), rs.tile(d), rs.halo_next(d, x.shape[0]), rs.mod(6 * d), _const_spec(w_up.shape), _const_spec(conv_w.shape),
         _const_spec((1, inner)), _const_spec(w_qk.shape), _const_spec(w_v.shape), _const_spec((inner, LANES)),
         _const_spec((1, LANES))],
        [rs.tile(dqk_all), rs.tile(dqk_all), rs.tile(inner), rs.tile(LANES)],
        [jax.ShapeDtypeStruct((r, dqk_all), MXU), jax.ShapeDtypeStruct((r, dqk_all), MXU),
         jax.ShapeDtypeStruct((r, inner), MXU), jax.ShapeDtypeStruct((r, LANES), F32)],
        vmem=vmem, name="mlstm_pre",
    )(x, x, x, mod, w_up, conv_w, conv_b.reshape(1, inner), w_qk, w_v, wg_pad, bg_pad)


def _scan_rowblock(batch, nc_ctx, nc_lat):
    def rowblk(b, d, c):
        cc = jnp.where(d == 1, nc_ctx - 1 - c, c)
        lc = c - nc_ctx
        lc = jnp.where(d == 1, nc_lat - 1 - lc, lc)
        return jnp.where(c < nc_ctx, batch * nc_lat + b * nc_ctx + cc, b * nc_lat + lc)
    return rowblk


def _mlstm_scan_kernel(q0_ref, k0_ref, v0_ref, g0_ref, q1_ref, k1_ref, v1_ref, g1_ref,
                       o0_ref, o1_ref, c_scr, n_scr, m_scr):
    @pl.when(pl.program_id(1) == 0)
    def _():
        c_scr[...] = jnp.zeros_like(c_scr)
        n_scr[...] = jnp.zeros_like(n_scr)
        m_scr[...] = jnp.zeros_like(m_scr)

    L = q0_ref.shape[0]
    dqk, dv = q0_ref.shape[1] // M_HEADS, v0_ref.shape[1] // M_HEADS
    two_h = 2 * M_HEADS
    dirs = ((q0_ref, k0_ref, v0_ref, g0_ref, o0_ref), (q1_ref, k1_ref, v1_ref, g1_ref, o1_ref))
    chains = [(d, h) for d in range(2) for h in range(M_HEADS)]
    row = lax.broadcasted_iota(jnp.int32, (L, L), 0)
    col = lax.broadcasted_iota(jnp.int32, (L, L), 1)
    causal = [col <= row, col >= row]
    causal_t = [row <= col, row >= col]
    eye = jnp.where(row == col, 1.0, 0.0).astype(MXU)
    gates = [dirs[d][3][...] for d in range(2)]
    pieces = [_split3(gates[d]) for d in range(2)]
    cs = [sum(_dot(jnp.where(causal[d], 1.0, 0.0).astype(MXU), p) for p in pieces[d]) for d in range(2)]
    cs_t = [sum(_dot_tn(p, jnp.where(causal_t[d], 1.0, 0.0).astype(MXU)) for p in pieces[d]) for d in range(2)]
    gates_t = [sum(_dot_tn(p, eye) for p in pieces[d]) for d in range(2)]

    i_slot = lambda d, h: d * two_h + h
    f_slot = lambda d, h: d * two_h + M_HEADS + h
    qs = lambda h: slice(h * dqk, (h + 1) * dqk)
    vs = lambda h: slice(h * dv, (h + 1) * dv)
    for g0 in range(0, len(chains), M_SCAN_GROUP):
        _mlstm_advance(chains[g0:g0 + M_SCAN_GROUP], dirs, causal, gates, gates_t, cs, cs_t, i_slot, f_slot, qs, vs,
                       c_scr, n_scr, m_scr)


def _mlstm_advance(chains, dirs, causal, gates, gates_t, cs, cs_t, i_slot, f_slot, qs, vs, c_scr, n_scr, m_scr):
    def per_chain(fn):
        return {ch: fn(*ch) for ch in chains}

    c_prev = per_chain(lambda d, h: c_scr[d, h])
    n_prev = per_chain(lambda d, h: n_scr[d, h])
    m_prev = per_chain(lambda d, h: m_scr[d, h])
    q = per_chain(lambda d, h: dirs[d][0][:, qs(h)])
    k = per_chain(lambda d, h: dirs[d][1][:, qs(h)])
    v = per_chain(lambda d, h: dirs[d][2][:, vs(h)])
    li_col = per_chain(lambda d, h: gates[d][:, i_slot(d, h):i_slot(d, h) + 1])
    li_row = per_chain(lambda d, h: gates_t[d][i_slot(d, h):i_slot(d, h) + 1, :])
    b_col = per_chain(lambda d, h: cs[d][:, f_slot(d, h):f_slot(d, h) + 1])
    b_row = per_chain(lambda d, h: cs_t[d][f_slot(d, h):f_slot(d, h) + 1, :])
    bl = per_chain(lambda d, h: jnp.sum(gates_t[d][f_slot(d, h):f_slot(d, h) + 1, :], axis=1, keepdims=True))

    dmat = per_chain(lambda d, h: jnp.where(causal[d], b_col[d, h] - b_row[d, h] + li_row[d, h], -jnp.inf))
    inter = per_chain(lambda d, h: b_col[d, h] + m_prev[d, h])
    mj = per_chain(lambda d, h: jnp.maximum(inter[d, h], jnp.max(dmat[d, h], axis=1, keepdims=True)))
    qk = per_chain(lambda d, h: _dot_nt(q[d, h], k[d, h]))
    wmat = per_chain(lambda d, h: jnp.exp(dmat[d, h] - mj[d, h]) * qk[d, h])
    g = per_chain(lambda d, h: jnp.exp(inter[d, h] - mj[d, h]))
    qc = per_chain(lambda d, h: _dot(q[d, h], c_prev[d, h].astype(MXU)))
    wv = per_chain(lambda d, h: _dot(wmat[d, h].astype(MXU), v[d, h]))
    qn = per_chain(lambda d, h: jnp.sum(q[d, h].astype(F32) * n_prev[d, h], axis=1, keepdims=True))
    den = per_chain(lambda d, h: g[d, h] * qn[d, h] + jnp.sum(wmat[d, h], axis=1, keepdims=True))
    for d, h in chains:
        num = g[d, h] * qc[d, h] + wv[d, h]
        dirs[d][4][:, vs(h)] = num / jnp.maximum(jnp.abs(den[d, h]), jnp.exp(-mj[d, h]))

    ds = per_chain(lambda d, h: bl[d, h] - b_col[d, h] + li_col[d, h])
    m_new = per_chain(lambda d, h: jnp.maximum(bl[d, h] + m_prev[d, h], jnp.max(ds[d, h], axis=0, keepdims=True)))
    kw = per_chain(lambda d, h: k[d, h].astype(F32) * jnp.exp(ds[d, h] - m_new[d, h]))
    decay = per_chain(lambda d, h: jnp.exp(bl[d, h] + m_prev[d, h] - m_new[d, h]))
    kv = per_chain(lambda d, h: _dot_tn(kw[d, h].astype(MXU), v[d, h]))
    for d, h in chains:
        c_scr[d, h] = decay[d, h] * c_prev[d, h] + kv[d, h]
        n_scr[d, h] = decay[d, h] * n_prev[d, h] + jnp.sum(kw[d, h], axis=0, keepdims=True)
        m_scr[d, h] = m_new[d, h]


def _mlstm_scan(rs, q, k, v, gates):
    b, L = rs.batch, M_CHUNK
    dqk_all, dv_all = q.shape[1], v.shape[1]
    nc_ctx, nc_lat = rs.ctx_len // L, rs.seq // L
    rowblk = _scan_rowblock(b, nc_ctx, nc_lat)

    def specs(d):
        rb = lambda b_, c: (rowblk(b_, d, c), 0)
        return [pl.BlockSpec((L, dqk_all), rb), pl.BlockSpec((L, dqk_all), rb), pl.BlockSpec((L, dv_all), rb),
                pl.BlockSpec((L, LANES), rb)]

    out_shape = jax.ShapeDtypeStruct((rs.rows, dv_all), F32)
    dqk, dv = dqk_all // M_HEADS, dv_all // M_HEADS
    return pl.pallas_call(
        _mlstm_scan_kernel, grid=(b, nc_ctx + nc_lat),
        in_specs=specs(0) + specs(1),
        out_specs=[pl.BlockSpec((L, dv_all), lambda b_, c: (rowblk(b_, 0, c), 0)),
                   pl.BlockSpec((L, dv_all), lambda b_, c: (rowblk(b_, 1, c), 0))],
        out_shape=[out_shape, out_shape],
        scratch_shapes=[pltpu.VMEM((2, M_HEADS, dqk, dv), F32), pltpu.VMEM((2, M_HEADS, 1, dqk), F32),
                        pltpu.VMEM((2, M_HEADS, 1, 1), F32)],
        compiler_params=pltpu.CompilerParams(dimension_semantics=("arbitrary",) * 2,
                                             vmem_limit_bytes=_vmem_limit(40 * 2**20)),
        name="mlstm_scan",
    )(q, k, v, gates, q, k, v, gates)


def _mlstm_post_kernel(x_ref, s0_ref, s1_ref, mod_ref, wog_ref, ng_ref, wout_ref, g_ref, b_ref, o_ref, *, alpha):
    d = x_ref.shape[1]
    h = _mod(x_ref[...], mod_ref, 0, d).astype(MXU)
    og = _sigmoid(_dot(h, wog_ref[...]))
    a = _head_rms(og * (s0_ref[...] + s1_ref[...]), ng_ref[...], M_HEADS).astype(MXU)
    _residual_ln(x_ref, mod_ref, _dot(a, wout_ref[...]), g_ref, b_ref, o_ref, alpha)


def _mlstm_post(rs, n_tiles, x, s, mod, w_og, norm_g, w_out, ln_g, ln_b, alpha):
    d, inner = w_og.shape
    body = functools.partial(_mlstm_post_kernel, alpha=alpha)
    vmem = 2 * (w_og.size + w_out.size) + 10 * TM * inner * 4 + 16 * 2**20
    return _row_call(
        body, rs, n_tiles,
        [rs.tile(d), rs.tile(inner), rs.tile(inner), rs.mod(6 * d), _const_spec(w_og.shape), _const_spec((1, inner)),
         _const_spec(w_out.shape), _const_spec((1, d)), _const_spec((1, d))],
        rs.tile(d), jax.ShapeDtypeStruct((n_tiles * TM, d), F32), vmem=vmem, name="mlstm_post",
    )(x, s[0], s[1], mod, w_og, norm_g.reshape(1, inner), w_out, ln_g.reshape(1, d), ln_b.reshape(1, d))


def _mla_pre_kernel(x_ref, mod_ref, cos_ref, sin_ref, wdq_ref, qn_ref, wuq_ref, wdkv_ref, kvn_ref, wk_ref, wv_ref,
                    q_ref, k_ref, v_ref, *, q_scale):
    d = x_ref.shape[1]
    hd = A_DNOPE + LANES
    h = _mod(x_ref[...], mod_ref, 0, d).astype(MXU)
    cos, sin = cos_ref[...], sin_ref[...]
    cq = _rms(_dot(h, wdq_ref[...]), qn_ref[...]).astype(MXU)
    qa = _dot(cq, wuq_ref[...])
    part0 = A_HEADS * hd
    cos_q, sin_q = cos * q_scale, sin * q_scale
    for hh in range(A_HEADS):
        q_ref[:, hh * hd:hh * hd + A_DNOPE] = (qa[:, hh * hd:hh * hd + A_DNOPE] * q_scale).astype(q_ref.dtype)
        rot = qa[:, hh * hd + A_DNOPE:(hh + 1) * hd] * cos_q + qa[:, part0 + hh * LANES:part0 + (hh + 1) * LANES] * sin_q
        q_ref[:, hh * hd + A_DNOPE:(hh + 1) * hd] = rot.astype(q_ref.dtype)
    dk = _dot(h, wdkv_ref[...])
    k_rope = (dk[:, A_KVLORA:A_KVLORA + LANES] * cos + dk[:, A_KVLORA + LANES:] * sin).astype(k_ref.dtype)
    ckv = _rms(dk[:, :A_KVLORA], kvn_ref[...]).astype(MXU)
    k_nope = _dot(ckv, wk_ref[...])
    for hh in range(A_HEADS):
        k_ref[:, hh * hd:hh * hd + A_DNOPE] = k_nope[:, hh * A_DNOPE:(hh + 1) * A_DNOPE].astype(k_ref.dtype)
        k_ref[:, hh * hd + A_DNOPE:(hh + 1) * hd] = k_rope
    v_ref[...] = _dot_nt(wv_ref[...], ckv).astype(v_ref.dtype)


def _rope_tables(seq):
    n_freq = A_DROPE // 4
    inv_freq = ROPE_BASE ** (-jnp.arange(n_freq, dtype=F32) / n_freq)
    pos = jnp.arange(seq)
    ang_row = (pos // GRID_W).astype(F32)[:, None] * inv_freq
    ang_col = (pos % GRID_W).astype(F32)[:, None] * inv_freq
    cos = jnp.concatenate([jnp.cos(ang_row)] * 2 + [jnp.cos(ang_col)] * 2, axis=1)
    sin = jnp.concatenate([-jnp.sin(ang_row), jnp.sin(ang_row), -jnp.sin(ang_col), jnp.sin(ang_col)], axis=1)
    pad = LANES - A_DROPE
    cos = jnp.pad(cos, ((0, 0), (0, pad)), constant_values=1.0)
    sin = jnp.pad(sin, ((0, 0), (0, pad)))
    ident = (jnp.ones((TM, LANES), F32), jnp.zeros((TM, LANES), F32))
    return jnp.concatenate([ident[0], cos]), jnp.concatenate([ident[1], sin])


def _rope_partner_cols(w):
    idx = np.arange(A_DROPE)
    half = A_DROPE // 4
    partner = np.where((idx % (2 * half)) < half, idx + half, idx - half)
    return w[..., partner]


def _mla_pre(rs, x, mod, w_dq, q_norm, w_uq, w_dkv, kv_norm, w_ukv):
    d, qlora = w_dq.shape
    hd = A_DNOPE + LANES
    pad = LANES - A_DROPE
    wq = w_uq.reshape(qlora, A_HEADS, A_DNOPE + A_DROPE)
    wq_main = jnp.pad(wq, ((0, 0), (0, 0), (0, pad))).reshape(qlora, A_HEADS * hd)
    wq_part = jnp.pad(_rope_partner_cols(wq[..., A_DNOPE:]), ((0, 0), (0, 0), (0, pad))).reshape(qlora, A_HEADS * LANES)
    wuq_all = jnp.concatenate([wq_main, wq_part], axis=1).astype(MXU)
    w_kr = w_dkv[:, A_KVLORA:]
    wdkv_all = jnp.concatenate([w_dkv[:, :A_KVLORA], jnp.pad(w_kr, ((0, 0), (0, pad))),
                                jnp.pad(_rope_partner_cols(w_kr), ((0, 0), (0, pad)))], axis=1).astype(MXU)
    wkv = w_ukv.reshape(A_KVLORA, A_HEADS, A_DNOPE + A_DV)
    w_k = wkv[..., :A_DNOPE].reshape(A_KVLORA, A_HEADS * A_DNOPE).astype(MXU)
    w_v = wkv[..., A_DNOPE:].reshape(A_KVLORA, A_HEADS * A_DV).T.astype(MXU)
    cos, sin = _rope_tables(rs.seq)
    nl, tps = rs.n_lat_tiles, rs.tiles_per_seq
    tab = pl.BlockSpec((TM, LANES), lambda i: (jnp.where(i >= nl, 0, 1 + i % tps), 0))
    n, r = rs.n_tiles, rs.rows
    vmem = 2 * 2 * (w_dq.size + wuq_all.size + wdkv_all.size + w_k.size + w_v.size) + 12 * TM * A_HEADS * hd * 4
    return _row_call(
        functools.partial(_mla_pre_kernel, q_scale=float((A_DNOPE + A_DROPE) ** -0.5 * np.log2(np.e))), rs, n,
        [rs.tile(d), rs.mod(6 * d), tab, tab, _const_spec(w_dq.shape), _const_spec((1, qlora)), _const_spec(wuq_all.shape),
         _const_spec(wdkv_all.shape), _const_spec((1, A_KVLORA)), _const_spec(w_k.shape), _const_spec(w_v.shape)],
        [rs.tile(A_HEADS * hd), rs.tile(A_HEADS * hd), pl.BlockSpec((A_HEADS * A_DV, TM), lambda i: (0, i))],
        [jax.ShapeDtypeStruct((r, A_HEADS * hd), MXU), jax.ShapeDtypeStruct((r, A_HEADS * hd), MXU),
         jax.ShapeDtypeStruct((A_HEADS * A_DV, r), MXU)],
        vmem=vmem, name="mla_pre",
    )(x, mod, cos, sin, w_dq.astype(MXU), q_norm.reshape(1, qlora), wuq_all, wdkv_all, kv_norm.reshape(1, A_KVLORA), w_k, w_v)


def _attn_kernel(*refs, with_latent, kv_chunk):
    if with_latent:
        q_ref, kc_ref, vc_ref, kl_ref, vl_ref, o_ref = refs
    else:
        q_ref, kc_ref, vc_ref, o_ref = refs
    hd, dv = A_DNOPE + LANES, A_DV
    heads = range(q_ref.shape[1] // hd)
    qs = lambda h: slice(h * hd, (h + 1) * hd)
    vs = lambda h: slice(h * dv, (h + 1) * dv)
    q = [q_ref[:, qs(h)] for h in heads]
    s = [_dot_nt(kc_ref[:, qs(h)], q[h]) for h in heads]
    m = [jnp.max(s[h], axis=0, keepdims=True) for h in heads]
    p = [jnp.exp2(s[h] - m[h]) for h in heads]
    l = [jnp.sum(p[h], axis=0, keepdims=True) for h in heads]
    acc = [_dot(vc_ref[vs(h), :], p[h].astype(MXU)) for h in heads]
    if with_latent:
        for c0 in range(0, kl_ref.shape[0], kv_chunk):
            s = [_dot_nt(kl_ref[c0:c0 + kv_chunk, qs(h)], q[h]) for h in heads]
            m_new = [jnp.maximum(m[h], jnp.max(s[h], axis=0, keepdims=True)) for h in heads]
            corr = [jnp.exp2(m[h] - m_new[h]) for h in heads]
            p = [jnp.exp2(s[h] - m_new[h]) for h in heads]
            l = [l[h] * corr[h] + jnp.sum(p[h], axis=0, keepdims=True) for h in heads]
            acc = [acc[h] * corr[h] + _dot(vl_ref[vs(h), c0:c0 + kv_chunk], p[h].astype(MXU)) for h in heads]
            m = m_new
    for h in heads:
        o_ref[:, vs(h)] = jnp.transpose(acc[h] / l[h]).astype(o_ref.dtype)


def _mla_attention(rs, q, k, v_t):
    b, t = rs.batch, rs.seq
    hpb = 2
    hd, dv = hpb * (A_DNOPE + LANES), hpb * A_DV
    kv_chunk = min(1024, t)
    assert t % kv_chunk == 0 and A_HEADS % hpb == 0
    tc = rs.ctx_len
    ctx_blk0 = b * t // tc
    params = pltpu.CompilerParams(dimension_semantics=("arbitrary",) * 3, vmem_limit_bytes=_vmem_limit(48 * 2**20))
    o_ctx = pl.pallas_call(
        functools.partial(_attn_kernel, with_latent=False, kv_chunk=kv_chunk), grid=(b, A_HEADS // hpb, 1),
        in_specs=[pl.BlockSpec((tc, hd), lambda b_, h, i: (ctx_blk0 + b_, h)),
                  pl.BlockSpec((tc, hd), lambda b_, h, i: (ctx_blk0 + b_, h)),
                  pl.BlockSpec((dv, tc), lambda b_, h, i: (h, ctx_blk0 + b_))],
        out_specs=pl.BlockSpec((tc, dv), lambda b_, h, i: (b_, h)),
        out_shape=jax.ShapeDtypeStruct((b * tc, A_HEADS * A_DV), MXU), compiler_params=params, name="mla_attn_ctx",
    )(q, k, v_t)
    tq = ATTN_TQ
    nq = t // tq
    o_lat = pl.pallas_call(
        functools.partial(_attn_kernel, with_latent=True, kv_chunk=kv_chunk), grid=(b, A_HEADS // hpb, nq),
        in_specs=[pl.BlockSpec((tq, hd), lambda b_, h, i: (b_ * nq + i, h)),
                  pl.BlockSpec((tc, hd), lambda b_, h, i: (ctx_blk0 + b_, h)),
                  pl.BlockSpec((dv, tc), lambda b_, h, i: (h, ctx_blk0 + b_)),
                  pl.BlockSpec((t, hd), lambda b_, h, i: (b_, h)),
                  pl.BlockSpec((dv, t), lambda b_, h, i: (h, b_))],
        out_specs=pl.BlockSpec((tq, dv), lambda b_, h, i: (b_ * nq + i, h)),
        out_shape=jax.ShapeDtypeStruct((b * t, A_HEADS * A_DV), MXU), compiler_params=params, name="mla_attn_latent",
    )(q, k, v_t, k, v_t)
    return o_lat, o_ctx


def _proj_post_kernel(x_ref, al_ref, ac_ref, mod_ref, wout_ref, g_ref, b_ref, o_ref, *, alpha, n_lat_tiles):
    a = jnp.where(pl.program_id(0) >= n_lat_tiles, ac_ref[...], al_ref[...])
    _residual_ln(x_ref, mod_ref, _dot(a, wout_ref[...]), g_ref, b_ref, o_ref, alpha)


def _proj_post(rs, n_tiles, x, a_lat, a_ctx, mod, w_out, ln_g, ln_b, alpha):
    kdim, d = w_out.shape
    nl = rs.n_lat_tiles
    body = functools.partial(_proj_post_kernel, alpha=alpha, n_lat_tiles=nl)
    return _row_call(
        body, rs, n_tiles,
        [rs.tile(d), pl.BlockSpec((TM, kdim), lambda i: (jnp.minimum(i, nl - 1), 0)),
         pl.BlockSpec((TM, kdim), lambda i: (jnp.maximum(i - nl, 0), 0)), rs.mod(6 * d), _const_spec(w_out.shape),
         _const_spec((1, d)), _const_spec((1, d))],
        rs.tile(d), jax.ShapeDtypeStruct((n_tiles * TM, d), F32), vmem=32 * 2**20, name="proj_post",
    )(x, a_lat, a_ctx, mod, w_out, ln_g.reshape(1, d), ln_b.reshape(1, d))


def _gla_pre_kernel(x_ref, mod_ref, wqk_ref, wv_ref, wa1_ref, wa2_ref, ba_ref, q_ref, k_ref, v_ref, la_ref,
                    *, q_scale):
    d = x_ref.shape[1]
    h = _mod(x_ref[...], mod_ref, 0, d).astype(MXU)
    qk = _dot(h, wqk_ref[...])
    half = qk.shape[1] // 2
    q_ref[...] = qk[:, :half] * q_scale
    k_ref[...] = qk[:, half:]
    v_ref[...] = _dot(h, wv_ref[...]).astype(v_ref.dtype)
    a1 = _dot(h, wa1_ref[...]).astype(MXU)
    z = _dot(a1, wa2_ref[...]) + ba_ref[...]
    la_ref[...] = _log_sigmoid(z) * (1.0 / G_TAU)


def _gla_pre(rs, x, mod, w_qk, w_v, w_a1, w_a2, b_a):
    d = w_qk.shape[0]
    dk_all = w_qk.shape[1] // 2
    wa1 = jnp.pad(jnp.concatenate([w_a1[0], w_a1[1]], axis=1), ((0, 0), (0, LANES - 2 * G_RANK))).astype(MXU)
    wa2 = jnp.zeros((LANES, 2 * dk_all), F32)
    wa2 = wa2.at[:G_RANK, :dk_all].set(w_a2[0]).at[G_RANK:2 * G_RANK, dk_all:].set(w_a2[1]).astype(MXU)
    ba = jnp.concatenate([b_a[0], b_a[1]]).reshape(1, 2 * dk_all)
    body = functools.partial(_gla_pre_kernel, q_scale=float((dk_all // G_HEADS) ** -0.5))
    n, r = rs.n_tiles, rs.rows
    return _row_call(
        body, rs, n,
        [rs.tile(d), rs.mod(6 * d), _const_spec(w_qk.shape), _const_spec(w_v.shape), _const_spec(wa1.shape),
         _const_spec(wa2.shape), _const_spec(ba.shape)],
        [rs.tile(dk_all), rs.tile(dk_all), rs.tile(w_v.shape[1]), rs.tile(2 * dk_all)],
        [jax.ShapeDtypeStruct((r, dk_all), F32), jax.ShapeDtypeStruct((r, dk_all), F32),
         jax.ShapeDtypeStruct((r, w_v.shape[1]), MXU), jax.ShapeDtypeStruct((r, 2 * dk_all), F32)],
        vmem=40 * 2**20, name="gla_pre",
    )(x, mod, w_qk, w_v, wa1, wa2, ba)


def _gla_tables(L):
    t = np.arange(L)
    tau, taup = t[:, None], t[None, :]
    groups = [taup <= tau, taup > tau]
    masks = []
    c = L // 2
    while c >= 1:
        blk = t // (2 * c)
        mid = blk * 2 * c + c
        second = (t % (2 * c)) >= c
        q_side = second[:, None] & (taup >= mid[:, None]) & (taup <= tau)
        k_side = (~second)[:, None] & (taup > tau) & (taup <= mid[:, None] - 1)
        groups.append(q_side | k_side)
        masks.append(second[:, None] & (~second)[None, :] & (blk[:, None] == blk[None, :]))
        c //= 2
    flip = lambda g: g[::-1, ::-1]
    sums = np.stack([np.concatenate(groups, axis=0), np.concatenate([flip(g) for g in groups], axis=0)])
    lvl = np.stack([np.stack(masks), np.stack([flip(m) for m in masks])])
    return sums.astype(np.float32), lvl.astype(np.float32)


def _gla_scan_kernel(q0_ref, k0_ref, v0_ref, la0_ref, q1_ref, k1_ref, v1_ref, la1_ref, sums_ref, masks_ref,
                     o0_ref, o1_ref, st_scr):
    @pl.when(pl.program_id(1) == 0)
    def _():
        st_scr[...] = jnp.zeros_like(st_scr)

    L = q0_ref.shape[0]
    dk, dv = q0_ref.shape[1] // G_HEADS, v0_ref.shape[1] // G_HEADS
    levels = masks_ref.shape[1]
    dirs = ((q0_ref, k0_ref, v0_ref, la0_ref, o0_ref), (q1_ref, k1_ref, v1_ref, la1_ref, o1_ref))
    chains = [(d, h) for d in range(2) for h in range(G_HEADS)]
    row = lax.broadcasted_iota(jnp.int32, (L, L), 0)
    col = lax.broadcasted_iota(jnp.int32, (L, L), 1)
    eye = row == col

    st_prev = {(d, h): st_scr[d, h] for d, h in chains}
    e = [jnp.exp(_dot(sums_ref[d], jnp.concatenate(_split2(dirs[d][3][...]), axis=0))) for d in range(2)]
    q = [dirs[d][0][...] for d in range(2)]
    k = [dirs[d][1][...] for d in range(2)]
    qe = [(q[d] * e[d][0:L]).astype(MXU) for d in range(2)]
    kd = [(k[d] * e[d][L:2 * L]).astype(MXU) for d in range(2)]
    qk_diag = [q[d] * k[d] for d in range(2)]
    qt = [[(q[d] * e[d][(2 + lv) * L:(3 + lv) * L]).astype(MXU) for lv in range(levels)] for d in range(2)]
    kt = [[(k[d] * e[d][(2 + lv) * L:(3 + lv) * L]).astype(MXU) for lv in range(levels)] for d in range(2)]
    decay = [jnp.exp(jnp.sum(dirs[d][3][...], axis=0, keepdims=True)) for d in range(2)]

    att = {}
    for d, h in chains:
        ks = slice(h * dk, (h + 1) * dk)
        a = jnp.where(eye, jnp.sum(qk_diag[d][:, ks], axis=1, keepdims=True), 0.0)
        for lv in range(levels):
            a = a + masks_ref[d, lv] * _dot_nt(qt[d][lv][:, ks], kt[d][lv][:, ks])
        att[d, h] = a.astype(MXU)
    for d, h in chains:
        ks, vs = slice(h * dk, (h + 1) * dk), slice(h * dv, (h + 1) * dv)
        v = dirs[d][2][:, vs]
        dirs[d][4][:, vs] = _dot_nt(qe[d][:, ks], st_prev[d, h].astype(MXU)) + _dot(att[d, h], v)
        st_scr[d, h] = st_prev[d, h] * decay[d][:, ks] + _dot_tn(v, kd[d][:, ks])


def _gla_scan(rs, q, k, v, la):
    b, L = rs.batch, G_CHUNK
    dk_all, dv_all = q.shape[1], v.shape[1]
    nc_ctx, nc_lat = rs.ctx_len // L, rs.seq // L
    rowblk = _scan_rowblock(b, nc_ctx, nc_lat)
    sums, masks = _gla_tables(L)
    sums, masks = jnp.asarray(np.concatenate([sums, sums], axis=2), MXU), jnp.asarray(masks, F32)

    def specs(d):
        rb = lambda b_, c: (rowblk(b_, d, c), 0)
        return [pl.BlockSpec((L, dk_all), rb), pl.BlockSpec((L, dk_all), rb), pl.BlockSpec((L, dv_all), rb),
                pl.BlockSpec((L, dk_all), lambda b_, c: (rowblk(b_, d, c), d))]

    out_shape = jax.ShapeDtypeStruct((rs.rows, dv_all), F32)
    return pl.pallas_call(
        _gla_scan_kernel, grid=(b, nc_ctx + nc_lat),
        in_specs=specs(0) + specs(1) + [_const_spec(sums.shape), _const_spec(masks.shape)],
        out_specs=[pl.BlockSpec((L, dv_all), lambda b_, c: (rowblk(b_, 0, c), 0)),
                   pl.BlockSpec((L, dv_all), lambda b_, c: (rowblk(b_, 1, c), 0))],
        out_shape=[out_shape, out_shape],
        scratch_shapes=[pltpu.VMEM((2, G_HEADS, dv_all // G_HEADS, dk_all // G_HEADS), F32)],
        compiler_params=pltpu.CompilerParams(dimension_semantics=("arbitrary",) * 2,
                                             vmem_limit_bytes=_vmem_limit(32 * 2**20)),
        name="gla_scan",
    )(q, k, v, la, q, k, v, la, sums, masks)


def _gla_post_kernel(x_ref, s0_ref, s1_ref, mod_ref, wr_ref, ng_ref, wout_ref, g_ref, b_ref, o_ref, *, alpha):
    d = x_ref.shape[1]
    h = _mod(x_ref[...], mod_ref, 0, d).astype(MXU)
    r = _silu(_dot(h, wr_ref[...]))
    a = (_head_rms(s0_ref[...] + s1_ref[...], ng_ref[...], G_HEADS) * r).astype(MXU)
    _residual_ln(x_ref, mod_ref, _dot(a, wout_ref[...]), g_ref, b_ref, o_ref, alpha)


def _gla_post(rs, n_tiles, x, s, mod, w_r, norm_g, w_out, ln_g, ln_b, alpha):
    d, dv_all = w_r.shape
    body = functools.partial(_gla_post_kernel, alpha=alpha)
    return _row_call(
        body, rs, n_tiles,
        [rs.tile(d), rs.tile(dv_all), rs.tile(dv_all), rs.mod(6 * d), _const_spec(w_r.shape), _const_spec((1, dv_all)), _const_spec(w_out.shape),
         _const_spec((1, d)), _const_spec((1, d))],
        rs.tile(d), jax.ShapeDtypeStruct((n_tiles * TM, d), F32), vmem=40 * 2**20, name="gla_post",
    )(x, s[0], s[1], mod, w_r, norm_g.reshape(1, dv_all), w_out, ln_g.reshape(1, d), ln_b.reshape(1, d))


def kernel(x, c, ctx, c_ctx, ada_w, ada_b, ln_g, ln_b, ffn_w_in, ffn_conv_w, ffn_conv_b, ffn_w_out, m_w_up, m_conv_w, m_conv_b, m_w_qk, m_w_v, m_w_gates, m_b_gates, m_w_og, m_norm_g, m_w_out, a_w_dq, a_q_norm, a_w_uq, a_w_dkv, a_kv_norm, a_w_ukv, a_w_out, g_w_qk, g_w_v, g_w_r, g_w_a1, g_w_a2, g_b_a, g_norm_g, g_w_out):
    batch, seq, d = x.shape
    depth = ada_w.shape[0]
    n_mixers = 3
    alpha = float((2 * depth) ** 0.25)
    rs = _Rows(batch, seq, ctx.shape[1])
    bf = lambda w: w.astype(MXU)

    cond_rows = -(-(batch + 1) // SUBLANES) * SUBLANES
    cond = jnp.zeros((cond_rows, d), F32).at[:batch].set(c).at[batch].set(c_ctx)
    mods = _modulation(cond, bf(ada_w), ada_b).reshape(depth, cond_rows, 1, 6 * d)

    xa = jnp.concatenate([x.reshape(batch * seq, d), ctx.reshape(-1, d)], axis=0)
    for i in range(depth):
        need_ctx = i < depth - 1
        n_tiles = rs.n_tiles if need_ctx else rs.n_lat_tiles
        kind, j = i % n_mixers, i // n_mixers
        mod = mods[i]
        if kind == 0:
            q, k, v, gates = _mlstm_pre(rs, xa, mod, bf(m_w_up[j]), m_conv_w[j], m_conv_b[j], bf(m_w_qk[j]),
                                        _weight_product(m_w_up[j], m_w_v[j]), m_w_gates[j], m_b_gates[j])
            s = _mlstm_scan(rs, q, k, v, gates)
            xa = _mlstm_post(rs, n_tiles, xa, s, mod, bf(m_w_og[j]), m_norm_g[j], bf(m_w_out[j]), ln_g[i, 0], ln_b[i, 0], alpha)
        elif kind == 1:
            q, k, v = _mla_pre(rs, xa, mod, a_w_dq[j], a_q_norm[j], a_w_uq[j], a_w_dkv[j], a_kv_norm[j], a_w_ukv[j])
            o_lat, o_ctx = _mla_attention(rs, q, k, v)
            xa = _proj_post(rs, n_tiles, xa, o_lat, o_ctx, mod, bf(a_w_out[j]), ln_g[i, 0], ln_b[i, 0], alpha)
        else:
            q, k, v, la = _gla_pre(rs, xa, mod, bf(g_w_qk[j]), bf(g_w_v[j]), g_w_a1[j], g_w_a2[j], g_b_a[j])
            s = _gla_scan(rs, q, k, v, la)
            xa = _gla_post(rs, n_tiles, xa, s, mod, bf(g_w_r[j]), g_norm_g[j], bf(g_w_out[j]), ln_g[i, 0], ln_b[i, 0], alpha)
        xa = _ffn(rs, n_tiles, xa, mod, bf(ffn_w_in[i]), ffn_conv_w[i], ffn_conv_b[i], bf(ffn_w_out[i]),
                  ln_g[i, 1], ln_b[i, 1], alpha)
    return xa.reshape(batch, seq, d)
```

```python
import functools

import numpy as np
import jax
import jax.numpy as jnp
from jax import lax
from jax.experimental import pallas as pl
from jax.experimental.pallas import tpu as pltpu

F32 = jnp.float32
MXU = jnp.bfloat16

V7X_VMEM_BYTES = 64 * 2**20
SUBLANES = 8
LANES = 128

TM = 512
ATTN_TQ = 512
HALO = SUBLANES
GRID_W = 64
EPS = 1e-6
ROPE_BASE = 10000.0
G_TAU = 16.0

M_HEADS, A_HEADS, G_HEADS = 4, 8, 4
A_DNOPE, A_DROPE, A_DV, A_KVLORA = 128, 64, 128, 256
G_RANK = 16
M_CHUNK = 256
M_SCAN_GROUP = 8
G_CHUNK = 128


def _vmem_limit(nbytes):
    return int(min(max(nbytes, 16 * 2**20), V7X_VMEM_BYTES - 8 * 2**20))


def _const_spec(shape):
    nd = len(shape)
    return pl.BlockSpec(shape, lambda *_: (0,) * nd, pipeline_mode=pl.Buffered(1))


def _dot(a, b):
    return jnp.dot(a, b, preferred_element_type=F32)


def _dot_nt(a, b):
    return lax.dot_general(a, b, (((1,), (1,)), ((), ())), preferred_element_type=F32)


def _dot_tn(a, b):
    return lax.dot_general(a, b, (((0,), (0,)), ((), ())), preferred_element_type=F32)


def _split3(x):
    hi = x.astype(MXU)
    r1 = x - hi.astype(F32)
    mid = r1.astype(MXU)
    lo = (r1 - mid.astype(F32)).astype(MXU)
    return hi, mid, lo


def _split2(x):
    hi = x.astype(MXU)
    return hi, (x - hi.astype(F32)).astype(MXU)


def _sigmoid(x):
    return 1.0 / (1.0 + jnp.exp(-x))


def _silu(x):
    return x * _sigmoid(x)


def _log_sigmoid(x):
    return jnp.minimum(x, 0.0) - jnp.log1p(jnp.exp(-jnp.abs(x)))


def _layer_norm(z, g, b):
    mu = jnp.mean(z, -1, keepdims=True)
    zc = z - mu
    var = jnp.mean(zc * zc, -1, keepdims=True)
    return zc * lax.rsqrt(var + EPS) * g + b


def _rms(x, g):
    return x * lax.rsqrt(jnp.mean(x * x, -1, keepdims=True) + EPS) * g


def _head_rms(x, g, heads):
    d = x.shape[-1] // heads
    return jnp.concatenate([_rms(x[:, h * d:(h + 1) * d], g[:, h * d:(h + 1) * d]) for h in range(heads)], axis=-1)


def _mod(x, mod_ref, k, d):
    return x * (1.0 + mod_ref[:, (k + 1) * d:(k + 2) * d]) + mod_ref[:, k * d:(k + 1) * d]


def _halo_rows(xp_ref, x_ref, xn_ref, mod_ref, k, d, geom):
    n_lat_tiles, tiles_per_seq, _ = geom
    i = pl.program_id(0)
    is_ctx = i >= n_lat_tiles
    pos = i % tiles_per_seq
    first = jnp.logical_or(is_ctx, pos == 0)
    last = jnp.logical_or(is_ctx, pos == tiles_per_seq - 1)
    hp = jnp.where(first, 0.0, _mod(xp_ref[...], mod_ref, k, d))
    hn = jnp.where(last, 0.0, _mod(xn_ref[...], mod_ref, k, d))
    return jnp.concatenate([hp, _mod(x_ref[...], mod_ref, k, d), hn], axis=0)


def _seq_edges(geom):
    n_lat_tiles, _, ctx_len = geom
    is_ctx = pl.program_id(0) >= n_lat_tiles
    r = lax.broadcasted_iota(jnp.int32, (TM, 1), 0) % ctx_len
    return jnp.logical_and(is_ctx, r == 0), jnp.logical_and(is_ctx, r == ctx_len - 1)


def _dwconv3(g_ext, w_ref, b_ref, c0, c1, edges):
    n = g_ext.shape[0] - 2 * HALO
    starts, ends = edges
    prev = jnp.where(starts, 0.0, g_ext[HALO - 1:HALO - 1 + n])
    nxt = jnp.where(ends, 0.0, g_ext[HALO + 1:HALO + 1 + n])
    return (w_ref[0:1, c0:c1] * prev + w_ref[1:2, c0:c1] * g_ext[HALO:HALO + n] + w_ref[2:3, c0:c1] * nxt
            + b_ref[:, c0:c1])


class _Rows:
    def __init__(self, batch, seq, ctx_len):
        assert TM % ctx_len == 0 and (batch * ctx_len) % TM == 0 and seq % TM == 0
        self.batch, self.seq, self.ctx_len = batch, seq, ctx_len
        self.tiles_per_seq = seq // TM
        self.n_lat_tiles = batch * self.tiles_per_seq
        self.n_tiles = self.n_lat_tiles + batch * ctx_len // TM
        self.rows = self.n_tiles * TM
        self.geom = (self.n_lat_tiles, self.tiles_per_seq, ctx_len)

    def tile(self, width):
        return pl.BlockSpec((TM, width), lambda i: (i, 0))

    def halo_prev(self, width):
        per = TM // HALO
        return pl.BlockSpec((HALO, width), lambda i: (jnp.maximum(i * per - 1, 0), 0))

    def halo_next(self, width, n_rows):
        per, nblk = TM // HALO, n_rows // HALO
        return pl.BlockSpec((HALO, width), lambda i: (jnp.minimum((i + 1) * per, nblk - 1), 0))

    def mod(self, width):
        nl, tps, b = self.n_lat_tiles, self.tiles_per_seq, self.batch
        return pl.BlockSpec((None, 1, width), lambda i: (jnp.where(i >= nl, b, i // tps), 0, 0))


def _row_call(body, rows, n_tiles, in_specs, out_specs, out_shape, scratch=(), vmem=0, name=None):
    return pl.pallas_call(
        body, grid=(n_tiles,), in_specs=in_specs, out_specs=out_specs, out_shape=out_shape,
        scratch_shapes=list(scratch), name=name,
        compiler_params=pltpu.CompilerParams(dimension_semantics=("arbitrary",), vmem_limit_bytes=_vmem_limit(vmem)))


def _modulation_kernel(c_ref, w_ref, b_ref, o_ref):
    o_ref[...] = _dot(_silu(c_ref[...]).astype(MXU), w_ref[...]) + b_ref[...]


def _modulation(cond, ada_w, ada_b):
    depth, d, n = ada_w.shape
    tn = d
    return pl.pallas_call(
        _modulation_kernel, grid=(depth, n // tn),
        in_specs=[pl.BlockSpec(cond.shape, lambda l, j: (0, 0)),
                  pl.BlockSpec((None, d, tn), lambda l, j: (l, 0, j)),
                  pl.BlockSpec((None, 1, tn), lambda l, j: (l, 0, j))],
        out_specs=pl.BlockSpec((None, cond.shape[0], tn), lambda l, j: (l, 0, j)),
        out_shape=jax.ShapeDtypeStruct((depth, cond.shape[0], n), F32), name="modulation",
    )(cond, ada_w, ada_b.reshape(depth, 1, n))


def _ffn_kernel(xp_ref, x_ref, xn_ref, mod_ref, win_ref, cw_ref, cb_ref, wout_ref, g_ref, b_ref, o_ref, a_scr,
                *, geom, alpha, ffn, tf):
    d = x_ref.shape[1]
    hext = _halo_rows(xp_ref, x_ref, xn_ref, mod_ref, 3, d, geom).astype(MXU)
    edges = _seq_edges(geom)
    for f0 in range(0, ffn, tf):
        g_ext = _dot(hext, win_ref[:, f0:f0 + tf])
        up = _dot(hext, win_ref[:, ffn + f0:ffn + f0 + tf])[HALO:HALO + TM]
        a_scr[:, f0:f0 + tf] = (_silu(_dwconv3(g_ext, cw_ref, cb_ref, f0, f0 + tf, edges)) * up).astype(MXU)
    f = _dot(a_scr[...], wout_ref[...])
    z = alpha * x_ref[...] + mod_ref[:, 5 * d:6 * d] * f
    o_ref[...] = _layer_norm(z, g_ref[...], b_ref[...])


def _ffn(rs, n_tiles, x, mod, w_in, conv_w, conv_b, w_out, ln_g, ln_b, alpha):
    d, ffn = w_out.shape[1], w_out.shape[0]
    tf = 256
    assert ffn % tf == 0
    body = functools.partial(_ffn_kernel, geom=rs.geom, alpha=alpha, ffn=ffn, tf=tf)
    vmem = 2 * (w_in.size + w_out.size) + 6 * TM * d * 4 + TM * ffn * 2 + 24 * 2**20
    return _row_call(
        body, rs, n_tiles,
        [rs.halo_prev(d), rs.tile(d), rs.halo_next(d, x.shape[0]), rs.mod(6 * d), _const_spec(w_in.shape), _const_spec(conv_w.shape),
         _const_spec((1, ffn)), _const_spec(w_out.shape), _const_spec((1, d)), _const_spec((1, d))],
        rs.tile(d), jax.ShapeDtypeStruct((n_tiles * TM, d), F32),
        scratch=[pltpu.VMEM((TM, ffn), MXU)], vmem=vmem, name="conv_ffn",
    )(x, x, x, mod, w_in, conv_w, conv_b.reshape(1, ffn), w_out, ln_g.reshape(1, d), ln_b.reshape(1, d))


def _residual_ln(x_ref, mod_ref, y, g_ref, b_ref, o_ref, alpha):
    d = x_ref.shape[1]
    z = alpha * x_ref[...] + mod_ref[:, 2 * d:3 * d] * y
    o_ref[...] = _layer_norm(z, g_ref[...], b_ref[...])


def _weight_product_kernel(a_ref, b_ref, o_ref):
    o_ref[...] = _dot(a_ref[...].astype(MXU), b_ref[...].astype(MXU)).astype(o_ref.dtype)


def _weight_product(a, b):
    m, kdim = a.shape
    n = b.shape[1]
    tn = 512
    assert n % tn == 0
    return pl.pallas_call(
        _weight_product_kernel, grid=(n // tn,),
        in_specs=[pl.BlockSpec((m, kdim), lambda j: (0, 0)), pl.BlockSpec((kdim, tn), lambda j: (0, j))],
        out_specs=pl.BlockSpec((m, tn), lambda j: (0, j)),
        out_shape=jax.ShapeDtypeStruct((m, n), MXU),
        compiler_params=pltpu.CompilerParams(dimension_semantics=("arbitrary",), vmem_limit_bytes=_vmem_limit(40 * 2**20)),
        name="weight_product",
    )(a, b)


def _mlstm_pre_kernel(xp_ref, x_ref, xn_ref, mod_ref, wup_ref, cw_ref, cb_ref, wqk_ref, wv_ref, wg_ref, bg_ref,
                      q_ref, k_ref, v_ref, gates_ref, *, geom, k_scale):
    d = x_ref.shape[1]
    hext = _halo_rows(xp_ref, x_ref, xn_ref, mod_ref, 0, d, geom).astype(MXU)
    xm_ext = _dot(hext, wup_ref[...])
    inner = xm_ext.shape[1]
    xc = _silu(_dwconv3(xm_ext, cw_ref, cb_ref, 0, inner, _seq_edges(geom))).astype(MXU)
    qk = _dot(xc, wqk_ref[...])
    half = qk.shape[1] // 2
    q_ref[...] = qk[:, :half].astype(q_ref.dtype)
    k_ref[...] = (qk[:, half:] * k_scale).astype(k_ref.dtype)
    v_ref[...] = _dot(_mod(x_ref[...], mod_ref, 0, d).astype(MXU), wv_ref[...]).astype(v_ref.dtype)
    two_h = 2 * M_HEADS
    gates = _dot(xc, wg_ref[...]) + bg_ref[...]
    lane = lax.broadcasted_iota(jnp.int32, gates.shape, 1)
    gates_ref[...] = jnp.where(lane % two_h >= M_HEADS, _log_sigmoid(gates), gates)


def _mlstm_pre(rs, x, mod, w_up, conv_w, conv_b, w_qk, w_v, w_gates, b_gates):
    d, inner = w_up.shape
    dqk_all = w_qk.shape[1] // 2
    n_g = 2 * 2 * M_HEADS
    wg = jnp.concatenate([w_gates[0], w_gates[1]], axis=1)
    wg_pad = jnp.pad(wg, ((0, 0), (0, LANES - n_g))).astype(MXU)
    bg = jnp.concatenate([b_gates[0], b_gates[1]])
    bg_pad = jnp.pad(bg, (0, LANES - n_g)).reshape(1, LANES)
    body = functools.partial(_mlstm_pre_kernel, geom=rs.geom, k_scale=float((dqk_all // M_HEADS) ** -0.5))
    n, r = rs.n_tiles, rs.rows
    vmem = 2 * (w_up.size + w_qk.size + w_v.size) + 8 * (TM + 2 * HALO) * inner * 4 + 16 * 2**20
    return _row_call(
        body, rs, n,
        [rs.halo_prev(d), rs.tile(d), rs.halo_next(d, x.shape[0]), rs.mod(6 * d), _const_spec(w_up.shape), _const_spec(conv_w.shape),
         _const_spec((1, inner)), _const_spec(w_qk.shape), _const_spec(w_v.shape), _const_spec((inner, LANES)),
         _const_spec((1, LANES))],
        [rs.tile(dqk_all), rs.tile(dqk_all), rs.tile(inner), rs.tile(LANES)],
        [jax.ShapeDtypeStruct((r, dqk_all), MXU), jax.ShapeDtypeStruct((r, dqk_all), MXU),
         jax.ShapeDtypeStruct((r, inner), MXU), jax.ShapeDtypeStruct((r, LANES), F32)],
        vmem=vmem, name="mlstm_pre",
    )(x, x, x, mod, w_up, conv_w, conv_b.reshape(1, inner), w_qk, w_v, wg_pad, bg_pad)


def _scan_rowblock(batch, nc_ctx, nc_lat):
    def rowblk(b, d, c):
        cc = jnp.where(d == 1, nc_ctx - 1 - c, c)
        lc = c - nc_ctx
        lc = jnp.where(d == 1, nc_lat - 1 - lc, lc)
        return jnp.where(c < nc_ctx, batch * nc_lat + b * nc_ctx + cc, b * nc_lat + lc)
    return rowblk


def _mlstm_scan_kernel(q0_ref, k0_ref, v0_ref, g0_ref, q1_ref, k1_ref, v1_ref, g1_ref,
                       o0_ref, o1_ref, c_scr, n_scr, m_scr):
    @pl.when(pl.program_id(1) == 0)
    def _():
        c_scr[...] = jnp.zeros_like(c_scr)
        n_scr[...] = jnp.zeros_like(n_scr)
        m_scr[...] = jnp.zeros_like(m_scr)

    L = q0_ref.shape[0]
    dqk, dv = q0_ref.shape[1] // M_HEADS, v0_ref.shape[1] // M_HEADS
    two_h = 2 * M_HEADS
    dirs = ((q0_ref, k0_ref, v0_ref, g0_ref, o0_ref), (q1_ref, k1_ref, v1_ref, g1_ref, o1_ref))
    chains = [(d, h) for d in range(2) for h in range(M_HEADS)]
    row = lax.broadcasted_iota(jnp.int32, (L, L), 0)
    col = lax.broadcasted_iota(jnp.int32, (L, L), 1)
    causal = [col <= row, col >= row]
    causal_t = [row <= col, row >= col]
    eye = jnp.where(row == col, 1.0, 0.0).astype(MXU)
    gates = [dirs[d][3][...] for d in range(2)]
    pieces = [_split3(gates[d]) for d in range(2)]
    cs = [sum(_dot(jnp.where(causal[d], 1.0, 0.0).astype(MXU), p) for p in pieces[d]) for d in range(2)]
    cs_t = [sum(_dot_tn(p, jnp.where(causal_t[d], 1.0, 0.0).astype(MXU)) for p in pieces[d]) for d in range(2)]
    gates_t = [sum(_dot_tn(p, eye) for p in pieces[d]) for d in range(2)]

    i_slot = lambda d, h: d * two_h + h
    f_slot = lambda d, h: d * two_h + M_HEADS + h
    qs = lambda h: slice(h * dqk, (h + 1) * dqk)
    vs = lambda h: slice(h * dv, (h + 1) * dv)
    for g0 in range(0, len(chains), M_SCAN_GROUP):
        _mlstm_advance(chains[g0:g0 + M_SCAN_GROUP], dirs, causal, gates, gates_t, cs, cs_t, i_slot, f_slot, qs, vs,
                       c_scr, n_scr, m_scr)


def _mlstm_advance(chains, dirs, causal, gates, gates_t, cs, cs_t, i_slot, f_slot, qs, vs, c_scr, n_scr, m_scr):
    def per_chain(fn):
        return {ch: fn(*ch) for ch in chains}

    c_prev = per_chain(lambda d, h: c_scr[d, h])
    n_prev = per_chain(lambda d, h: n_scr[d, h])
    m_prev = per_chain(lambda d, h: m_scr[d, h])
    q = per_chain(lambda d, h: dirs[d][0][:, qs(h)])
    k = per_chain(lambda d, h: dirs[d][1][:, qs(h)])
    v = per_chain(lambda d, h: dirs[d][2][:, vs(h)])
    li_col = per_chain(lambda d, h: gates[d][:, i_slot(d, h):i_slot(d, h) + 1])
    li_row = per_chain(lambda d, h: gates_t[d][i_slot(d, h):i_slot(d, h) + 1, :])
    b_col = per_chain(lambda d, h: cs[d][:, f_slot(d, h):f_slot(d, h) + 1])
    b_row = per_chain(lambda d, h: cs_t[d][f_slot(d, h):f_slot(d, h) + 1, :])
    bl = per_chain(lambda d, h: jnp.sum(gates_t[d][f_slot(d, h):f_slot(d, h) + 1, :], axis=1, keepdims=True))

    dmat = per_chain(lambda d, h: jnp.where(causal[d], b_col[d, h] - b_row[d, h] + li_row[d, h], -jnp.inf))
    inter = per_chain(lambda d, h: b_col[d, h] + m_prev[d, h])
    mj = per_chain(lambda d, h: jnp.maximum(inter[d, h], jnp.max(dmat[d, h], axis=1, keepdims=True)))
    qk = per_chain(lambda d, h: _dot_nt(q[d, h], k[d, h]))
    wmat = per_chain(lambda d, h: jnp.exp(dmat[d, h] - mj[d, h]) * qk[d, h])
    g = per_chain(lambda d, h: jnp.exp(inter[d, h] - mj[d, h]))
    qc = per_chain(lambda d, h: _dot(q[d, h], c_prev[d, h].astype(MXU)))
    wv = per_chain(lambda d, h: _dot(wmat[d, h].astype(MXU), v[d, h]))
    qn = per_chain(lambda d, h: jnp.sum(q[d, h].astype(F32) * n_prev[d, h], axis=1, keepdims=True))
    den = per_chain(lambda d, h: g[d, h] * qn[d, h] + jnp.sum(wmat[d, h], axis=1, keepdims=True))
    for d, h in chains:
        num = g[d, h] * qc[d, h] + wv[d, h]
        dirs[d][4][:, vs(h)] = num / jnp.maximum(jnp.abs(den[d, h]), jnp.exp(-mj[d, h]))

    ds = per_chain(lambda d, h: bl[d, h] - b_col[d, h] + li_col[d, h])
    m_new = per_chain(lambda d, h: jnp.maximum(bl[d, h] + m_prev[d, h], jnp.max(ds[d, h], axis=0, keepdims=True)))
    kw = per_chain(lambda d, h: k[d, h].astype(F32) * jnp.exp(ds[d, h] - m_new[d, h]))
    decay = per_chain(lambda d, h: jnp.exp(bl[d, h] + m_prev[d, h] - m_new[d, h]))
    kv = per_chain(lambda d, h: _dot_tn(kw[d, h].astype(MXU), v[d, h]))
    for d, h in chains:
        c_scr[d, h] = decay[d, h] * c_prev[d, h] + kv[d, h]
        n_scr[d, h] = decay[d, h] * n_prev[d, h] + jnp.sum(kw[d, h], axis=0, keepdims=True)
        m_scr[d, h] = m_new[d, h]


def _mlstm_scan(rs, q, k, v, gates):
    b, L = rs.batch, M_CHUNK
    dqk_all, dv_all = q.shape[1], v.shape[1]
    nc_ctx, nc_lat = rs.ctx_len // L, rs.seq // L
    rowblk = _scan_rowblock(b, nc_ctx, nc_lat)

    def specs(d):
        rb = lambda b_, c: (rowblk(b_, d, c), 0)
        return [pl.BlockSpec((L, dqk_all), rb), pl.BlockSpec((L, dqk_all), rb), pl.BlockSpec((L, dv_all), rb),
                pl.BlockSpec((L, LANES), rb)]

    out_shape = jax.ShapeDtypeStruct((rs.rows, dv_all), F32)
    dqk, dv = dqk_all // M_HEADS, dv_all // M_HEADS
    return pl.pallas_call(
        _mlstm_scan_kernel, grid=(b, nc_ctx + nc_lat),
        in_specs=specs(0) + specs(1),
        out_specs=[pl.BlockSpec((L, dv_all), lambda b_, c: (rowblk(b_, 0, c), 0)),
                   pl.BlockSpec((L, dv_all), lambda b_, c: (rowblk(b_, 1, c), 0))],
        out_shape=[out_shape, out_shape],
        scratch_shapes=[pltpu.VMEM((2, M_HEADS, dqk, dv), F32), pltpu.VMEM((2, M_HEADS, 1, dqk), F32),
                        pltpu.VMEM((2, M_HEADS, 1, 1), F32)],
        compiler_params=pltpu.CompilerParams(dimension_semantics=("arbitrary",) * 2,
                                             vmem_limit_bytes=_vmem_limit(40 * 2**20)),
        name="mlstm_scan",
    )(q, k, v, gates, q, k, v, gates)


def _mlstm_post_kernel(x_ref, s0_ref, s1_ref, mod_ref, wog_ref, ng_ref, wout_ref, g_ref, b_ref, o_ref, *, alpha):
    d = x_ref.shape[1]
    h = _mod(x_ref[...], mod_ref, 0, d).astype(MXU)
    og = _sigmoid(_dot(h, wog_ref[...]))
    a = _head_rms(og * (s0_ref[...] + s1_ref[...]), ng_ref[...], M_HEADS).astype(MXU)
    _residual_ln(x_ref, mod_ref, _dot(a, wout_ref[...]), g_ref, b_ref, o_ref, alpha)


def _mlstm_post(rs, n_tiles, x, s, mod, w_og, norm_g, w_out, ln_g, ln_b, alpha):
    d, inner = w_og.shape
    body = functools.partial(_mlstm_post_kernel, alpha=alpha)
    vmem = 2 * (w_og.size + w_out.size) + 10 * TM * inner * 4 + 16 * 2**20
    return _row_call(
        body, rs, n_tiles,
        [rs.tile(d), rs.tile(inner), rs.tile(inner), rs.mod(6 * d), _const_spec(w_og.shape), _const_spec((1, inner)),
         _const_spec(w_out.shape), _const_spec((1, d)), _const_spec((1, d))],
        rs.tile(d), jax.ShapeDtypeStruct((n_tiles * TM, d), F32), vmem=vmem, name="mlstm_post",
    )(x, s[0], s[1], mod, w_og, norm_g.reshape(1, inner), w_out, ln_g.reshape(1, d), ln_b.reshape(1, d))


def _mla_pre_kernel(x_ref, mod_ref, cos_ref, sin_ref, wdq_ref, qn_ref, wuq_ref, wdkv_ref, kvn_ref, wk_ref, wv_ref,
                    q_ref, k_ref, v_ref, *, q_scale):
    d = x_ref.shape[1]
    hd = A_DNOPE + LANES
    h = _mod(x_ref[...], mod_ref, 0, d).astype(MXU)
    cos, sin = cos_ref[...], sin_ref[...]
    cq = _rms(_dot(h, wdq_ref[...]), qn_ref[...]).astype(MXU)
    qa = _dot(cq, wuq_ref[...])
    part0 = A_HEADS * hd
    cos_q, sin_q = cos * q_scale, sin * q_scale
    for hh in range(A_HEADS):
        q_ref[:, hh * hd:hh * hd + A_DNOPE] = (qa[:, hh * hd:hh * hd + A_DNOPE] * q_scale).astype(q_ref.dtype)
        rot = qa[:, hh * hd + A_DNOPE:(hh + 1) * hd] * cos_q + qa[:, part0 + hh * LANES:part0 + (hh + 1) * LANES] * sin_q
        q_ref[:, hh * hd + A_DNOPE:(hh + 1) * hd] = rot.astype(q_ref.dtype)
    dk = _dot(h, wdkv_ref[...])
    k_rope = (dk[:, A_KVLORA:A_KVLORA + LANES] * cos + dk[:, A_KVLORA + LANES:] * sin).astype(k_ref.dtype)
    ckv = _rms(dk[:, :A_KVLORA], kvn_ref[...]).astype(MXU)
    k_nope = _dot(ckv, wk_ref[...])
    for hh in range(A_HEADS):
        k_ref[:, hh * hd:hh * hd + A_DNOPE] = k_nope[:, hh * A_DNOPE:(hh + 1) * A_DNOPE].astype(k_ref.dtype)
        k_ref[:, hh * hd + A_DNOPE:(hh + 1) * hd] = k_rope
    v_ref[...] = _dot(ckv, wv_ref[...]).astype(v_ref.dtype)


def _rope_tables(seq):
    n_freq = A_DROPE // 4
    inv_freq = ROPE_BASE ** (-jnp.arange(n_freq, dtype=F32) / n_freq)
    pos = jnp.arange(seq)
    ang_row = (pos // GRID_W).astype(F32)[:, None] * inv_freq
    ang_col = (pos % GRID_W).astype(F32)[:, None] * inv_freq
    cos = jnp.concatenate([jnp.cos(ang_row)] * 2 + [jnp.cos(ang_col)] * 2, axis=1)
    sin = jnp.concatenate([-jnp.sin(ang_row), jnp.sin(ang_row), -jnp.sin(ang_col), jnp.sin(ang_col)], axis=1)
    pad = LANES - A_DROPE
    cos = jnp.pad(cos, ((0, 0), (0, pad)), constant_values=1.0)
    sin = jnp.pad(sin, ((0, 0), (0, pad)))
    ident = (jnp.ones((TM, LANES), F32), jnp.zeros((TM, LANES), F32))
    return jnp.concatenate([ident[0], cos]), jnp.concatenate([ident[1], sin])


def _rope_partner_cols(w):
    idx = np.arange(A_DROPE)
    half = A_DROPE // 4
    partner = np.where((idx % (2 * half)) < half, idx + half, idx - half)
    return w[..., partner]


def _mla_pre(rs, x, mod, w_dq, q_norm, w_uq, w_dkv, kv_norm, w_ukv):
    d, qlora = w_dq.shape
    hd = A_DNOPE + LANES
    pad = LANES - A_DROPE
    wq = w_uq.reshape(qlora, A_HEADS, A_DNOPE + A_DROPE)
    wq_main = jnp.pad(wq, ((0, 0), (0, 0), (0, pad))).reshape(qlora, A_HEADS * hd)
    wq_part = jnp.pad(_rope_partner_cols(wq[..., A_DNOPE:]), ((0, 0), (0, 0), (0, pad))).reshape(qlora, A_HEADS * LANES)
    wuq_all = jnp.concatenate([wq_main, wq_part], axis=1).astype(MXU)
    w_kr = w_dkv[:, A_KVLORA:]
    wdkv_all = jnp.concatenate([w_dkv[:, :A_KVLORA], jnp.pad(w_kr, ((0, 0), (0, pad))),
                                jnp.pad(_rope_partner_cols(w_kr), ((0, 0), (0, pad)))], axis=1).astype(MXU)
    wkv = w_ukv.reshape(A_KVLORA, A_HEADS, A_DNOPE + A_DV)
    w_k = wkv[..., :A_DNOPE].reshape(A_KVLORA, A_HEADS * A_DNOPE).astype(MXU)
    w_v = wkv[..., A_DNOPE:].reshape(A_KVLORA, A_HEADS * A_DV).astype(MXU)
    cos, sin = _rope_tables(rs.seq)
    nl, tps = rs.n_lat_tiles, rs.tiles_per_seq
    tab = pl.BlockSpec((TM, LANES), lambda i: (jnp.where(i >= nl, 0, 1 + i % tps), 0))
    n, r = rs.n_tiles, rs.rows
    vmem = 2 * 2 * (w_dq.size + wuq_all.size + wdkv_all.size + w_k.size + w_v.size) + 12 * TM * A_HEADS * hd * 4
    return _row_call(
        functools.partial(_mla_pre_kernel, q_scale=float((A_DNOPE + A_DROPE) ** -0.5 * np.log2(np.e))), rs, n,
        [rs.tile(d), rs.mod(6 * d), tab, tab, _const_spec(w_dq.shape), _const_spec((1, qlora)), _const_spec(wuq_all.shape),
         _const_spec(wdkv_all.shape), _const_spec((1, A_KVLORA)), _const_spec(w_k.shape), _const_spec(w_v.shape)],
        [rs.tile(A_HEADS * hd), rs.tile(A_HEADS * hd), rs.tile(A_HEADS * A_DV)],
        [jax.ShapeDtypeStruct((r, A_HEADS * hd), MXU), jax.ShapeDtypeStruct((r, A_HEADS * hd), MXU),
         jax.ShapeDtypeStruct((r, A_HEADS * A_DV), MXU)],
        vmem=vmem, name="mla_pre",
    )(x, mod, cos, sin, w_dq.astype(MXU), q_norm.reshape(1, qlora), wuq_all, wdkv_all, kv_norm.reshape(1, A_KVLORA), w_k, w_v)


def _attn_kernel(*refs, with_latent, kv_chunk):
    if with_latent:
        q_ref, kc_ref, vc_ref, kl_ref, vl_ref, o_ref = refs
    else:
        q_ref, kc_ref, vc_ref, o_ref = refs
    hd, dv = A_DNOPE + LANES, A_DV
    heads = range(q_ref.shape[1] // hd)
    qs = lambda h: slice(h * hd, (h + 1) * hd)
    vs = lambda h: slice(h * dv, (h + 1) * dv)
    q = [q_ref[:, qs(h)] for h in heads]
    s = [_dot_nt(q[h], kc_ref[:, qs(h)]) for h in heads]
    m = [jnp.max(s[h], axis=1, keepdims=True) for h in heads]
    p = [jnp.exp2(s[h] - m[h]) for h in heads]
    l = [jnp.sum(p[h], axis=1, keepdims=True) for h in heads]
    acc = [_dot(p[h].astype(MXU), vc_ref[:, vs(h)]) for h in heads]
    if with_latent:
        for c0 in range(0, kl_ref.shape[0], kv_chunk):
            s = [_dot_nt(q[h], kl_ref[c0:c0 + kv_chunk, qs(h)]) for h in heads]
            m_new = [jnp.maximum(m[h], jnp.max(s[h], axis=1, keepdims=True)) for h in heads]
            corr = [jnp.exp2(m[h] - m_new[h]) for h in heads]
            p = [jnp.exp2(s[h] - m_new[h]) for h in heads]
            l = [l[h] * corr[h] + jnp.sum(p[h], axis=1, keepdims=True) for h in heads]
            acc = [acc[h] * corr[h] + _dot(p[h].astype(MXU), vl_ref[c0:c0 + kv_chunk, vs(h)]) for h in heads]
            m = m_new
    for h in heads:
        o_ref[:, vs(h)] = (acc[h] / l[h]).astype(o_ref.dtype)


def _mla_attention(rs, q, k, v):
    b, t = rs.batch, rs.seq
    hpb = 2
    hd, dv = hpb * (A_DNOPE + LANES), hpb * A_DV
    kv_chunk = min(1024, t)
    assert t % kv_chunk == 0 and A_HEADS % hpb == 0
    tc = rs.ctx_len
    ctx_blk0 = b * t // tc
    params = pltpu.CompilerParams(dimension_semantics=("arbitrary",) * 3, vmem_limit_bytes=_vmem_limit(48 * 2**20))
    o_ctx = pl.pallas_call(
        functools.partial(_attn_kernel, with_latent=False, kv_chunk=kv_chunk), grid=(b, A_HEADS // hpb, 1),
        in_specs=[pl.BlockSpec((tc, hd), lambda b_, h, i: (ctx_blk0 + b_, h)),
                  pl.BlockSpec((tc, hd), lambda b_, h, i: (ctx_blk0 + b_, h)),
                  pl.BlockSpec((tc, dv), lambda b_, h, i: (ctx_blk0 + b_, h))],
        out_specs=pl.BlockSpec((tc, dv), lambda b_, h, i: (b_, h)),
        out_shape=jax.ShapeDtypeStruct((b * tc, A_HEADS * A_DV), MXU), compiler_params=params, name="mla_attn_ctx",
    )(q, k, v)
    tq = ATTN_TQ
    nq = t // tq
    o_lat = pl.pallas_call(
        functools.partial(_attn_kernel, with_latent=True, kv_chunk=kv_chunk), grid=(b, A_HEADS // hpb, nq),
        in_specs=[pl.BlockSpec((tq, hd), lambda b_, h, i: (b_ * nq + i, h)),
                  pl.BlockSpec((tc, hd), lambda b_, h, i: (ctx_blk0 + b_, h)),
                  pl.BlockSpec((tc, dv), lambda b_, h, i: (ctx_blk0 + b_, h)),
                  pl.BlockSpec((t, hd), lambda b_, h, i: (b_, h)),
                  pl.BlockSpec((t, dv), lambda b_, h, i: (b_, h))],
        out_specs=pl.BlockSpec((tq, dv), lambda b_, h, i: (b_ * nq + i, h)),
        out_shape=jax.ShapeDtypeStruct((b * t, A_HEADS * A_DV), MXU), compiler_params=params, name="mla_attn_latent",
    )(q, k, v, k, v)
    return o_lat, o_ctx


def _proj_post_kernel(x_ref, al_ref, ac_ref, mod_ref, wout_ref, g_ref, b_ref, o_ref, *, alpha, n_lat_tiles):
    a = jnp.where(pl.program_id(0) >= n_lat_tiles, ac_ref[...], al_ref[...])
    _residual_ln(x_ref, mod_ref, _dot(a, wout_ref[...]), g_ref, b_ref, o_ref, alpha)


def _proj_post(rs, n_tiles, x, a_lat, a_ctx, mod, w_out, ln_g, ln_b, alpha):
    kdim, d = w_out.shape
    nl = rs.n_lat_tiles
    body = functools.partial(_proj_post_kernel, alpha=alpha, n_lat_tiles=nl)
    return _row_call(
        body, rs, n_tiles,
        [rs.tile(d), pl.BlockSpec((TM, kdim), lambda i: (jnp.minimum(i, nl - 1), 0)),
         pl.BlockSpec((TM, kdim), lambda i: (jnp.maximum(i - nl, 0), 0)), rs.mod(6 * d), _const_spec(w_out.shape),
         _const_spec((1, d)), _const_spec((1, d))],
        rs.tile(d), jax.ShapeDtypeStruct((n_tiles * TM, d), F32), vmem=32 * 2**20, name="proj_post",
    )(x, a_lat, a_ctx, mod, w_out, ln_g.reshape(1, d), ln_b.reshape(1, d))


def _gla_pre_kernel(x_ref, mod_ref, wqk_ref, wv_ref, wa1_ref, wa2_ref, ba_ref, q_ref, k_ref, v_ref, la_ref,
                    *, q_scale):
    d = x_ref.shape[1]
    h = _mod(x_ref[...], mod_ref, 0, d).astype(MXU)
    qk = _dot(h, wqk_ref[...])
    half = qk.shape[1] // 2
    q_ref[...] = qk[:, :half] * q_scale
    k_ref[...] = qk[:, half:]
    v_ref[...] = _dot(h, wv_ref[...]).astype(v_ref.dtype)
    a1 = _dot(h, wa1_ref[...]).astype(MXU)
    z = _dot(a1, wa2_ref[...]) + ba_ref[...]
    la_ref[...] = _log_sigmoid(z) * (1.0 / G_TAU)


def _gla_pre(rs, x, mod, w_qk, w_v, w_a1, w_a2, b_a):
    d = w_qk.shape[0]
    dk_all = w_qk.shape[1] // 2
    wa1 = jnp.pad(jnp.concatenate([w_a1[0], w_a1[1]], axis=1), ((0, 0), (0, LANES - 2 * G_RANK))).astype(MXU)
    wa2 = jnp.zeros((LANES, 2 * dk_all), F32)
    wa2 = wa2.at[:G_RANK, :dk_all].set(w_a2[0]).at[G_RANK:2 * G_RANK, dk_all:].set(w_a2[1]).astype(MXU)
    ba = jnp.concatenate([b_a[0], b_a[1]]).reshape(1, 2 * dk_all)
    body = functools.partial(_gla_pre_kernel, q_scale=float((dk_all // G_HEADS) ** -0.5))
    n, r = rs.n_tiles, rs.rows
    return _row_call(
        body, rs, n,
        [rs.tile(d), rs.mod(6 * d), _const_spec(w_qk.shape), _const_spec(w_v.shape), _const_spec(wa1.shape),
         _const_spec(wa2.shape), _const_spec(ba.shape)],
        [rs.tile(dk_all), rs.tile(dk_all), rs.tile(w_v.shape[1]), rs.tile(2 * dk_all)],
        [jax.ShapeDtypeStruct((r, dk_all), F32), jax.ShapeDtypeStruct((r, dk_all), F32),
         jax.ShapeDtypeStruct((r, w_v.shape[1]), MXU), jax.ShapeDtypeStruct((r, 2 * dk_all), F32)],
        vmem=40 * 2**20, name="gla_pre",
    )(x, mod, w_qk, w_v, wa1, wa2, ba)


def _gla_tables(L):
    t = np.arange(L)
    tau, taup = t[:, None], t[None, :]
    groups = [taup <= tau, taup > tau]
    masks = []
    c = L // 2
    while c >= 1:
        blk = t // (2 * c)
        mid = blk * 2 * c + c
        second = (t % (2 * c)) >= c
        q_side = second[:, None] & (taup >= mid[:, None]) & (taup <= tau)
        k_side = (~second)[:, None] & (taup > tau) & (taup <= mid[:, None] - 1)
        groups.append(q_side | k_side)
        masks.append(second[:, None] & (~second)[None, :] & (blk[:, None] == blk[None, :]))
        c //= 2
    flip = lambda g: g[::-1, ::-1]
    sums = np.stack([np.concatenate(groups, axis=0), np.concatenate([flip(g) for g in groups], axis=0)])
    lvl = np.stack([np.stack(masks), np.stack([flip(m) for m in masks])])
    return sums.astype(np.float32), lvl.astype(np.float32)


def _gla_scan_kernel(q0_ref, k0_ref, v0_ref, la0_ref, q1_ref, k1_ref, v1_ref, la1_ref, sums_ref, masks_ref,
                     o0_ref, o1_ref, st_scr):
    @pl.when(pl.program_id(1) == 0)
    def _():
        st_scr[...] = jnp.zeros_like(st_scr)

    L = q0_ref.shape[0]
    dk, dv = q0_ref.shape[1] // G_HEADS, v0_ref.shape[1] // G_HEADS
    levels = masks_ref.shape[1]
    dirs = ((q0_ref, k0_ref, v0_ref, la0_ref, o0_ref), (q1_ref, k1_ref, v1_ref, la1_ref, o1_ref))
    chains = [(d, h) for d in range(2) for h in range(G_HEADS)]
    row = lax.broadcasted_iota(jnp.int32, (L, L), 0)
    col = lax.broadcasted_iota(jnp.int32, (L, L), 1)
    eye = row == col

    st_prev = {(d, h): st_scr[d, h] for d, h in chains}
    e = [jnp.exp(_dot(sums_ref[d], jnp.concatenate(_split2(dirs[d][3][...]), axis=0))) for d in range(2)]
    q = [dirs[d][0][...] for d in range(2)]
    k = [dirs[d][1][...] for d in range(2)]
    qe = [(q[d] * e[d][0:L]).astype(MXU) for d in range(2)]
    kd = [(k[d] * e[d][L:2 * L]).astype(MXU) for d in range(2)]
    qk_diag = [q[d] * k[d] for d in range(2)]
    qt = [[(q[d] * e[d][(2 + lv) * L:(3 + lv) * L]).astype(MXU) for lv in range(levels)] for d in range(2)]
    kt = [[(k[d] * e[d][(2 + lv) * L:(3 + lv) * L]).astype(MXU) for lv in range(levels)] for d in range(2)]
    decay = [jnp.exp(jnp.sum(dirs[d][3][...], axis=0, keepdims=True)) for d in range(2)]

    att = {}
    for d, h in chains:
        ks = slice(h * dk, (h + 1) * dk)
        a = jnp.where(eye, jnp.sum(qk_diag[d][:, ks], axis=1, keepdims=True), 0.0)
        for lv in range(levels):
            a = a + masks_ref[d, lv] * _dot_nt(qt[d][lv][:, ks], kt[d][lv][:, ks])
        att[d, h] = a.astype(MXU)
    for d, h in chains:
        ks, vs = slice(h * dk, (h + 1) * dk), slice(h * dv, (h + 1) * dv)
        v = dirs[d][2][:, vs]
        dirs[d][4][:, vs] = _dot_nt(qe[d][:, ks], st_prev[d, h].astype(MXU)) + _dot(att[d, h], v)
        st_scr[d, h] = st_prev[d, h] * decay[d][:, ks] + _dot_tn(v, kd[d][:, ks])


def _gla_scan(rs, q, k, v, la):
    b, L = rs.batch, G_CHUNK
    dk_all, dv_all = q.shape[1], v.shape[1]
    nc_ctx, nc_lat = rs.ctx_len // L, rs.seq // L
    rowblk = _scan_rowblock(b, nc_ctx, nc_lat)
    sums, masks = _gla_tables(L)
    sums, masks = jnp.asarray(np.concatenate([sums, sums], axis=2), MXU), jnp.asarray(masks, F32)

    def specs(d):
        rb = lambda b_, c: (rowblk(b_, d, c), 0)
        return [pl.BlockSpec((L, dk_all), rb), pl.BlockSpec((L, dk_all), rb), pl.BlockSpec((L, dv_all), rb),
                pl.BlockSpec((L, dk_all), lambda b_, c: (rowblk(b_, d, c), d))]

    out_shape = jax.ShapeDtypeStruct((rs.rows, dv_all), F32)
    return pl.pallas_call(
        _gla_scan_kernel, grid=(b, nc_ctx + nc_lat),
        in_specs=specs(0) + specs(1) + [_const_spec(sums.shape), _const_spec(masks.shape)],
        out_specs=[pl.BlockSpec((L, dv_all), lambda b_, c: (rowblk(b_, 0, c), 0)),
                   pl.BlockSpec((L, dv_all), lambda b_, c: (rowblk(b_, 1, c), 0))],
        out_shape=[out_shape, out_shape],
        scratch_shapes=[pltpu.VMEM((2, G_HEADS, dv_all // G_HEADS, dk_all // G_HEADS), F32)],
        compiler_params=pltpu.CompilerParams(dimension_semantics=("arbitrary",) * 2,
                                             vmem_limit_bytes=_vmem_limit(32 * 2**20)),
        name="gla_scan",
    )(q, k, v, la, q, k, v, la, sums, masks)


def _gla_post_kernel(x_ref, s0_ref, s1_ref, mod_ref, wr_ref, ng_ref, wout_ref, g_ref, b_ref, o_ref, *, alpha):
    d = x_ref.shape[1]
    h = _mod(x_ref[...], mod_ref, 0, d).astype(MXU)
    r = _silu(_dot(h, wr_ref[...]))
    a = (_head_rms(s0_ref[...] + s1_ref[...], ng_ref[...], G_HEADS) * r).astype(MXU)
    _residual_ln(x_ref, mod_ref, _dot(a, wout_ref[...]), g_ref, b_ref, o_ref, alpha)


def _gla_post(rs, n_tiles, x, s, mod, w_r, norm_g, w_out, ln_g, ln_b, alpha):
    d, dv_all = w_r.shape
    body = functools.partial(_gla_post_kernel, alpha=alpha)
    return _row_call(
        body, rs, n_tiles,
        [rs.tile(d), rs.tile(dv_all), rs.tile(dv_all), rs.mod(6 * d), _const_spec(w_r.shape), _const_spec((1, dv_all)), _const_spec(w_out.shape),
         _const_spec((1, d)), _const_spec((1, d))],
        rs.tile(d), jax.ShapeDtypeStruct((n_tiles * TM, d), F32), vmem=40 * 2**20, name="gla_post",
    )(x, s[0], s[1], mod, w_r, norm_g.reshape(1, dv_all), w_out, ln_g.reshape(1, d), ln_b.reshape(1, d))


def kernel(x, c, ctx, c_ctx, ada_w, ada_b, ln_g, ln_b, ffn_w_in, ffn_conv_w, ffn_conv_b, ffn_w_out, m_w_up, m_conv_w, m_conv_b, m_w_qk, m_w_v, m_w_gates, m_b_gates, m_w_og, m_norm_g, m_w_out, a_w_dq, a_q_norm, a_w_uq, a_w_dkv, a_kv_norm, a_w_ukv, a_w_out, g_w_qk, g_w_v, g_w_r, g_w_a1, g_w_a2, g_b_a, g_norm_g, g_w_out):
    batch, seq, d = x.shape
    depth = ada_w.shape[0]
    n_mixers = 3
    alpha = float((2 * depth) ** 0.25)
    rs = _Rows(batch, seq, ctx.shape[1])
    bf = lambda w: w.astype(MXU)

    cond_rows = -(-(batch + 1) // SUBLANES) * SUBLANES
    cond = jnp.zeros((cond_rows, d), F32).at[:batch].set(c).at[batch].set(c_ctx)
    mods = _modulation(cond, bf(ada_w), ada_b).reshape(depth, cond_rows, 1, 6 * d)

    xa = jnp.concatenate([x.reshape(batch * seq, d), ctx.reshape(-1, d)], axis=0)
    for i in range(depth):
        need_ctx = i < depth - 1
        n_tiles = rs.n_tiles if need_ctx else rs.n_lat_tiles
        kind, j = i % n_mixers, i // n_mixers
        mod = mods[i]
        if kind == 0:
            q, k, v, gates = _mlstm_pre(rs, xa, mod, bf(m_w_up[j]), m_conv_w[j], m_conv_b[j], bf(m_w_qk[j]),
                                        _weight_product(m_w_up[j], m_w_v[j]), m_w_gates[j], m_b_gates[j])
            s = _mlstm_scan(rs, q, k, v, gates)
            xa = _mlstm_post(rs, n_tiles, xa, s, mod, bf(m_w_og[j]), m_norm_g[j], bf(m_w_out[j]), ln_g[i, 0], ln_b[i, 0], alpha)
        elif kind == 1:
            q, k, v = _mla_pre(rs, xa, mod, a_w_dq[j], a_q_norm[j], a_w_uq[j], a_w_dkv[j], a_kv_norm[j], a_w_ukv[j])
            o_lat, o_ctx = _mla_attention(rs, q, k, v)
            xa = _proj_post(rs, n_tiles, xa, o_lat, o_ctx, mod, bf(a_w_out[j]), ln_g[i, 0], ln_b[i, 0], alpha)
        else:
            q, k, v, la = _gla_pre(rs, xa, mod, bf(g_w_qk[j]), bf(g_w_v[j]), g_w_a1[j], g_w_a2[j], g_b_a[j])
            s = _gla_scan(rs, q, k, v, la)
            xa = _gla_post(rs, n_tiles, xa, s, mod, bf(g_w_r[j]), g_norm_g[j], bf(g_w_out[j]), ln_g[i, 0], ln_b[i, 0], alpha)
        xa = _ffn(rs, n_tiles, xa, mod, bf(ffn_w_in[i]), ffn_conv_w[i], ffn_conv_b[i], bf(ffn_w_out[i]),
                  ln_g[i, 1], ln_b[i, 1], alpha)
    return xa.reshape(batch, seq, d)
```

```python
import functools

import numpy as np
import jax
import jax.numpy as jnp
from jax import lax
from jax.experimental import pallas as pl
from jax.experimental.pallas import tpu as pltpu

F32 = jnp.float32
MXU = jnp.bfloat16

V7X_VMEM_BYTES = 64 * 2**20
SUBLANES = 8
LANES = 128

TM = 512
ATTN_TQ = 512
HEAD_ROWS = 256
TAIL_ROWS = 128
HALO = SUBLANES
GRID_W = 64
EPS = 1e-6
ROPE_BASE = 10000.0
G_TAU = 16.0

M_HEADS, A_HEADS, G_HEADS = 4, 8, 4
A_DNOPE, A_DROPE, A_DV, A_KVLORA = 128, 64, 128, 256
G_RANK = 16
M_CHUNK = 256
M_SCAN_GROUP = 8
G_CHUNK = 128


def _vmem_limit(nbytes):
    return int(min(max(nbytes, 16 * 2**20), V7X_VMEM_BYTES - 8 * 2**20))


def _const_spec(shape):
    nd = len(shape)
    return pl.BlockSpec(shape, lambda *_: (0,) * nd, pipeline_mode=pl.Buffered(1))


def _dot(a, b):
    return jnp.dot(a, b, preferred_element_type=F32)


def _dot_nt(a, b):
    return lax.dot_general(a, b, (((1,), (1,)), ((), ())), preferred_element_type=F32)


def _dot_tn(a, b):
    return lax.dot_general(a, b, (((0,), (0,)), ((), ())), preferred_element_type=F32)


def _split3(x):
    hi = x.astype(MXU)
    r1 = x - hi.astype(F32)
    mid = r1.astype(MXU)
    lo = (r1 - mid.astype(F32)).astype(MXU)
    return hi, mid, lo


def _split2(x):
    hi = x.astype(MXU)
    return hi, (x - hi.astype(F32)).astype(MXU)


def _sigmoid(x):
    return 1.0 / (1.0 + jnp.exp(-x))


def _silu(x):
    return x * _sigmoid(x)


def _log_sigmoid(x):
    return jnp.minimum(x, 0.0) - jnp.log1p(jnp.exp(-jnp.abs(x)))


def _layer_norm(z, g, b):
    mu = jnp.mean(z, -1, keepdims=True)
    zc = z - mu
    var = jnp.mean(zc * zc, -1, keepdims=True)
    return zc * lax.rsqrt(var + EPS) * g + b


def _rms(x, g):
    return x * lax.rsqrt(jnp.mean(x * x, -1, keepdims=True) + EPS) * g


def _head_rms(x, g, heads):
    d = x.shape[-1] // heads
    return jnp.concatenate([_rms(x[:, h * d:(h + 1) * d], g[:, h * d:(h + 1) * d]) for h in range(heads)], axis=-1)


def _mod(x, mod_ref, k, d):
    return x * (1.0 + mod_ref[:, (k + 1) * d:(k + 2) * d]) + mod_ref[:, k * d:(k + 1) * d]


def _halo_rows(xp_ref, x_ref, xn_ref, mod_ref, k, d, geom):
    n_lat_tiles, tiles_per_seq, _ = geom
    i = pl.program_id(0)
    is_ctx = i >= n_lat_tiles
    pos = i % tiles_per_seq
    first = jnp.logical_or(is_ctx, pos == 0)
    last = jnp.logical_or(is_ctx, pos == tiles_per_seq - 1)
    hp = jnp.where(first, 0.0, _mod(xp_ref[...], mod_ref, k, d))
    hn = jnp.where(last, 0.0, _mod(xn_ref[...], mod_ref, k, d))
    return jnp.concatenate([hp, _mod(x_ref[...], mod_ref, k, d), hn], axis=0)


def _seq_edges(geom):
    n_lat_tiles, _, ctx_len = geom
    is_ctx = pl.program_id(0) >= n_lat_tiles
    r = lax.broadcasted_iota(jnp.int32, (TM, 1), 0) % ctx_len
    return jnp.logical_and(is_ctx, r == 0), jnp.logical_and(is_ctx, r == ctx_len - 1)


def _dwconv3(g_ext, w_ref, b_ref, c0, c1, edges):
    n = g_ext.shape[0] - 2 * HALO
    starts, ends = edges
    prev = jnp.where(starts, 0.0, g_ext[HALO - 1:HALO - 1 + n])
    nxt = jnp.where(ends, 0.0, g_ext[HALO + 1:HALO + 1 + n])
    return (w_ref[0:1, c0:c1] * prev + w_ref[1:2, c0:c1] * g_ext[HALO:HALO + n] + w_ref[2:3, c0:c1] * nxt
            + b_ref[:, c0:c1])


class _Rows:
    def __init__(self, batch, seq, ctx_len):
        assert TM % ctx_len == 0 and (batch * ctx_len) % TM == 0 and seq % TM == 0
        self.batch, self.seq, self.ctx_len = batch, seq, ctx_len
        self.tiles_per_seq = seq // TM
        self.n_lat_tiles = batch * self.tiles_per_seq
        self.n_tiles = self.n_lat_tiles + batch * ctx_len // TM
        self.rows = self.n_tiles * TM
        self.geom = (self.n_lat_tiles, self.tiles_per_seq, ctx_len)

    def tile(self, width):
        return pl.BlockSpec((TM, width), lambda i: (i, 0))

    def halo_prev(self, width):
        per = TM // HALO
        return pl.BlockSpec((HALO, width), lambda i: (jnp.maximum(i * per - 1, 0), 0))

    def halo_next(self, width, n_rows):
        per, nblk = TM // HALO, n_rows // HALO
        return pl.BlockSpec((HALO, width), lambda i: (jnp.minimum((i + 1) * per, nblk - 1), 0))

    def mod(self, width):
        nl, tps, b = self.n_lat_tiles, self.tiles_per_seq, self.batch
        return pl.BlockSpec((None, 1, width), lambda i: (jnp.where(i >= nl, b, i // tps), 0, 0))


def _row_call(body, rows, n_tiles, in_specs, out_specs, out_shape, scratch=(), vmem=0, name=None):
    return pl.pallas_call(
        body, grid=(n_tiles,), in_specs=in_specs, out_specs=out_specs, out_shape=out_shape,
        scratch_shapes=list(scratch), name=name,
        compiler_params=pltpu.CompilerParams(dimension_semantics=("arbitrary",), vmem_limit_bytes=_vmem_limit(vmem)))


def _modulation_kernel(c_ref, w_ref, b_ref, o_ref):
    o_ref[...] = _dot(_silu(c_ref[...]).astype(MXU), w_ref[...].astype(MXU)) + b_ref[...]


def _modulation(cond, ada_w, ada_b):
    depth, d, n = ada_w.shape
    tn = d
    return pl.pallas_call(
        _modulation_kernel, grid=(depth, n // tn),
        in_specs=[pl.BlockSpec(cond.shape, lambda l, j: (0, 0)),
                  pl.BlockSpec((None, d, tn), lambda l, j: (l, 0, j)),
                  pl.BlockSpec((None, 1, tn), lambda l, j: (l, 0, j))],
        out_specs=pl.BlockSpec((None, cond.shape[0], tn), lambda l, j: (l, 0, j)),
        out_shape=jax.ShapeDtypeStruct((depth, cond.shape[0], n), F32), name="modulation",
    )(cond, ada_w, ada_b.reshape(depth, 1, n))


def _ffn_kernel(xp_ref, x_ref, xn_ref, mod_ref, win_ref, cw_ref, cb_ref, wout_ref, g_ref, b_ref, o_ref, a_scr,
                *, geom, alpha, ffn, tf):
    d = x_ref.shape[1]
    hext = _halo_rows(xp_ref, x_ref, xn_ref, mod_ref, 3, d, geom).astype(MXU)
    edges = _seq_edges(geom)
    for f0 in range(0, ffn, tf):
        g_ext = _dot(hext, win_ref[:, f0:f0 + tf])
        up = _dot(hext, win_ref[:, ffn + f0:ffn + f0 + tf])[HALO:HALO + TM]
        a_scr[:, f0:f0 + tf] = (_silu(_dwconv3(g_ext, cw_ref, cb_ref, f0, f0 + tf, edges)) * up).astype(MXU)
    for r0 in range(0, TM, TAIL_ROWS):
        rows = slice(r0, r0 + TAIL_ROWS)
        f = _dot(a_scr[rows, :], wout_ref[...])
        z = alpha * x_ref[rows, :] + mod_ref[:, 5 * d:6 * d] * f
        o_ref[rows, :] = _layer_norm(z, g_ref[...], b_ref[...])


def _ffn(rs, n_tiles, x, mod, w_in, conv_w, conv_b, w_out, ln_g, ln_b, alpha):
    d, ffn = w_out.shape[1], w_out.shape[0]
    tf = 256
    assert ffn % tf == 0
    body = functools.partial(_ffn_kernel, geom=rs.geom, alpha=alpha, ffn=ffn, tf=tf)
    vmem = 2 * (w_in.size + w_out.size) + 6 * TM * d * 4 + TM * ffn * 2 + 24 * 2**20
    return _row_call(
        body, rs, n_tiles,
        [rs.halo_prev(d), rs.tile(d), rs.halo_next(d, x.shape[0]), rs.mod(6 * d), _const_spec(w_in.shape), _const_spec(conv_w.shape),
         _const_spec((1, ffn)), _const_spec(w_out.shape), _const_spec((1, d)), _const_spec((1, d))],
        rs.tile(d), jax.ShapeDtypeStruct((n_tiles * TM, d), F32),
        scratch=[pltpu.VMEM((TM, ffn), MXU)], vmem=vmem, name="conv_ffn",
    )(x, x, x, mod, w_in, conv_w, conv_b.reshape(1, ffn), w_out, ln_g.reshape(1, d), ln_b.reshape(1, d))


def _residual_ln(x_ref, mod_ref, a, wout_ref, g_ref, b_ref, o_ref, alpha):
    d = x_ref.shape[1]
    for r0 in range(0, TM, TAIL_ROWS):
        rows = slice(r0, r0 + TAIL_ROWS)
        z = alpha * x_ref[rows, :] + mod_ref[:, 2 * d:3 * d] * _dot(a[rows, :], wout_ref[...])
        o_ref[rows, :] = _layer_norm(z, g_ref[...], b_ref[...])


def _weight_product_kernel(a_ref, b_ref, o_ref):
    o_ref[...] = _dot(a_ref[...].astype(MXU), b_ref[...].astype(MXU)).astype(o_ref.dtype)


def _weight_product(a, b):
    m, kdim = a.shape
    n = b.shape[1]
    tn = 512
    assert n % tn == 0
    return pl.pallas_call(
        _weight_product_kernel, grid=(n // tn,),
        in_specs=[pl.BlockSpec((m, kdim), lambda j: (0, 0)), pl.BlockSpec((kdim, tn), lambda j: (0, j))],
        out_specs=pl.BlockSpec((m, tn), lambda j: (0, j)),
        out_shape=jax.ShapeDtypeStruct((m, n), MXU),
        compiler_params=pltpu.CompilerParams(dimension_semantics=("arbitrary",), vmem_limit_bytes=_vmem_limit(40 * 2**20)),
        name="weight_product",
    )(a, b)


def _mlstm_pre_kernel(xp_ref, x_ref, xn_ref, mod_ref, wup_ref, cw_ref, cb_ref, wqk_ref, wv_ref, wg_ref, bg_ref,
                      q_ref, k_ref, v_ref, gates_ref, *, geom, k_scale):
    d = x_ref.shape[1]
    hext = _halo_rows(xp_ref, x_ref, xn_ref, mod_ref, 0, d, geom).astype(MXU)
    xm_ext = _dot(hext, wup_ref[...])
    inner = xm_ext.shape[1]
    xc = _silu(_dwconv3(xm_ext, cw_ref, cb_ref, 0, inner, _seq_edges(geom))).astype(MXU)
    qk = _dot(xc, wqk_ref[...])
    half = qk.shape[1] // 2
    q_ref[...] = qk[:, :half].astype(q_ref.dtype)
    k_ref[...] = (qk[:, half:] * k_scale).astype(k_ref.dtype)
    v_ref[...] = _dot(_mod(x_ref[...], mod_ref, 0, d).astype(MXU), wv_ref[...]).astype(v_ref.dtype)
    two_h = 2 * M_HEADS
    gates = _dot(xc, wg_ref[...]) + bg_ref[...]
    lane = lax.broadcasted_iota(jnp.int32, gates.shape, 1)
    gates_ref[...] = jnp.where(lane % two_h >= M_HEADS, _log_sigmoid(gates), gates)


def _mlstm_pre(rs, x, mod, w_up, conv_w, conv_b, w_qk, w_v, w_gates, b_gates):
    d, inner = w_up.shape
    dqk_all = w_qk.shape[1] // 2
    n_g = 2 * 2 * M_HEADS
    wg = jnp.concatenate([w_gates[0], w_gates[1]], axis=1)
    wg_pad = jnp.pad(wg, ((0, 0), (0, LANES - n_g))).astype(MXU)
    bg = jnp.concatenate([b_gates[0], b_gates[1]])
    bg_pad = jnp.pad(bg, (0, LANES - n_g)).reshape(1, LANES)
    body = functools.partial(_mlstm_pre_kernel, geom=rs.geom, k_scale=float((dqk_all // M_HEADS) ** -0.5))
    n, r = rs.n_tiles, rs.rows
    vmem = 2 * (w_up.size + w_qk.size + w_v.size) + 8 * (TM + 2 * HALO) * inner * 4 + 16 * 2**20
    return _row_call(
        body, rs, n,
        [rs.halo_prev(d), rs.tile(d), rs.halo_next(d, x.shape[0]), rs.mod(6 * d), _const_spec(w_up.shape), _const_spec(conv_w.shape),
         _const_spec((1, inner)), _const_spec(w_qk.shape), _const_spec(w_v.shape), _const_spec((inner, LANES)),
         _const_spec((1, LANES))],
        [rs.tile(dqk_all), rs.tile(dqk_all), rs.tile(inner), rs.tile(LANES)],
        [jax.ShapeDtypeStruct((r, dqk_all), MXU), jax.ShapeDtypeStruct((r, dqk_all), MXU),
         jax.ShapeDtypeStruct((r, inner), MXU), jax.ShapeDtypeStruct((r, LANES), F32)],
        vmem=vmem, name="mlstm_pre",
    )(x, x, x, mod, w_up, conv_w, conv_b.reshape(1, inner), w_qk, w_v, wg_pad, bg_pad)


def _scan_rowblock(batch, nc_ctx, nc_lat):
    def rowblk(b, d, c):
        cc = jnp.where(d == 1, nc_ctx - 1 - c, c)
        lc = c - nc_ctx
        lc = jnp.where(d == 1, nc_lat - 1 - lc, lc)
        return jnp.where(c < nc_ctx, batch * nc_lat + b * nc_ctx + cc, b * nc_lat + lc)
    return rowblk


def _mlstm_scan_kernel(q0_ref, k0_ref, v0_ref, g0_ref, q1_ref, k1_ref, v1_ref, g1_ref,
                       o0_ref, o1_ref, c_scr, n_scr, m_scr):
    @pl.when(pl.program_id(1) == 0)
    def _():
        c_scr[...] = jnp.zeros_like(c_scr)
        n_scr[...] = jnp.zeros_like(n_scr)
        m_scr[...] = jnp.zeros_like(m_scr)

    L = q0_ref.shape[0]
    dqk, dv = q0_ref.shape[1] // M_HEADS, v0_ref.shape[1] // M_HEADS
    two_h = 2 * M_HEADS
    dirs = ((q0_ref, k0_ref, v0_ref, g0_ref, o0_ref), (q1_ref, k1_ref, v1_ref, g1_ref, o1_ref))
    chains = [(d, h) for d in range(2) for h in range(M_HEADS)]
    row = lax.broadcasted_iota(jnp.int32, (L, L), 0)
    col = lax.broadcasted_iota(jnp.int32, (L, L), 1)
    causal = [col <= row, col >= row]
    causal_t = [row <= col, row >= col]
    eye = jnp.where(row == col, 1.0, 0.0).astype(MXU)
    gates = [dirs[d][3][...] for d in range(2)]
    pieces = [_split3(gates[d]) for d in range(2)]
    cs = [sum(_dot(jnp.where(causal[d], 1.0, 0.0).astype(MXU), p) for p in pieces[d]) for d in range(2)]
    cs_t = [sum(_dot_tn(p, jnp.where(causal_t[d], 1.0, 0.0).astype(MXU)) for p in pieces[d]) for d in range(2)]
    gates_t = [sum(_dot_tn(p, eye) for p in pieces[d]) for d in range(2)]

    i_slot = lambda d, h: d * two_h + h
    f_slot = lambda d, h: d * two_h + M_HEADS + h
    qs = lambda h: slice(h * dqk, (h + 1) * dqk)
    vs = lambda h: slice(h * dv, (h + 1) * dv)
    for g0 in range(0, len(chains), M_SCAN_GROUP):
        _mlstm_advance(chains[g0:g0 + M_SCAN_GROUP], dirs, causal, gates, gates_t, cs, cs_t, i_slot, f_slot, qs, vs,
                       c_scr, n_scr, m_scr)


def _mlstm_advance(chains, dirs, causal, gates, gates_t, cs, cs_t, i_slot, f_slot, qs, vs, c_scr, n_scr, m_scr):
    def per_chain(fn):
        return {ch: fn(*ch) for ch in chains}

    c_prev = per_chain(lambda d, h: c_scr[d, h])
    n_prev = per_chain(lambda d, h: n_scr[d, h])
    m_prev = per_chain(lambda d, h: m_scr[d, h])
    q = per_chain(lambda d, h: dirs[d][0][:, qs(h)])
    k = per_chain(lambda d, h: dirs[d][1][:, qs(h)])
    v = per_chain(lambda d, h: dirs[d][2][:, vs(h)])
    li_col = per_chain(lambda d, h: gates[d][:, i_slot(d, h):i_slot(d, h) + 1])
    li_row = per_chain(lambda d, h: gates_t[d][i_slot(d, h):i_slot(d, h) + 1, :])
    b_col = per_chain(lambda d, h: cs[d][:, f_slot(d, h):f_slot(d, h) + 1])
    b_row = per_chain(lambda d, h: cs_t[d][f_slot(d, h):f_slot(d, h) + 1, :])
    bl = per_chain(lambda d, h: jnp.sum(gates_t[d][f_slot(d, h):f_slot(d, h) + 1, :], axis=1, keepdims=True))

    dmat = per_chain(lambda d, h: jnp.where(causal[d], b_col[d, h] - b_row[d, h] + li_row[d, h], -jnp.inf))
    inter = per_chain(lambda d, h: b_col[d, h] + m_prev[d, h])
    mj = per_chain(lambda d, h: jnp.maximum(inter[d, h], jnp.max(dmat[d, h], axis=1, keepdims=True)))
    qk = per_chain(lambda d, h: _dot_nt(q[d, h], k[d, h]))
    wmat = per_chain(lambda d, h: jnp.exp(dmat[d, h] - mj[d, h]) * qk[d, h])
    g = per_chain(lambda d, h: jnp.exp(inter[d, h] - mj[d, h]))
    qc = per_chain(lambda d, h: _dot(q[d, h], c_prev[d, h].astype(MXU)))
    wv = per_chain(lambda d, h: _dot(wmat[d, h].astype(MXU), v[d, h]))
    qn = per_chain(lambda d, h: jnp.sum(q[d, h].astype(F32) * n_prev[d, h], axis=1, keepdims=True))
    den = per_chain(lambda d, h: g[d, h] * qn[d, h] + jnp.sum(wmat[d, h], axis=1, keepdims=True))
    for d, h in chains:
        num = g[d, h] * qc[d, h] + wv[d, h]
        dirs[d][4][:, vs(h)] = num / jnp.maximum(jnp.abs(den[d, h]), jnp.exp(-mj[d, h]))

    ds = per_chain(lambda d, h: bl[d, h] - b_col[d, h] + li_col[d, h])
    m_new = per_chain(lambda d, h: jnp.maximum(bl[d, h] + m_prev[d, h], jnp.max(ds[d, h], axis=0, keepdims=True)))
    kw = per_chain(lambda d, h: k[d, h].astype(F32) * jnp.exp(ds[d, h] - m_new[d, h]))
    decay = per_chain(lambda d, h: jnp.exp(bl[d, h] + m_prev[d, h] - m_new[d, h]))
    kv = per_chain(lambda d, h: _dot_tn(kw[d, h].astype(MXU), v[d, h]))
    for d, h in chains:
        c_scr[d, h] = decay[d, h] * c_prev[d, h] + kv[d, h]
        n_scr[d, h] = decay[d, h] * n_prev[d, h] + jnp.sum(kw[d, h], axis=0, keepdims=True)
        m_scr[d, h] = m_new[d, h]


def _mlstm_scan(rs, q, k, v, gates):
    b, L = rs.batch, M_CHUNK
    dqk_all, dv_all = q.shape[1], v.shape[1]
    nc_ctx, nc_lat = rs.ctx_len // L, rs.seq // L
    rowblk = _scan_rowblock(b, nc_ctx, nc_lat)

    def specs(d):
        rb = lambda b_, c: (rowblk(b_, d, c), 0)
        return [pl.BlockSpec((L, dqk_all), rb), pl.BlockSpec((L, dqk_all), rb), pl.BlockSpec((L, dv_all), rb),
                pl.BlockSpec((L, LANES), rb)]

    out_shape = jax.ShapeDtypeStruct((rs.rows, dv_all), F32)
    dqk, dv = dqk_all // M_HEADS, dv_all // M_HEADS
    return pl.pallas_call(
        _mlstm_scan_kernel, grid=(b, nc_ctx + nc_lat),
        in_specs=specs(0) + specs(1),
        out_specs=[pl.BlockSpec((L, dv_all), lambda b_, c: (rowblk(b_, 0, c), 0)),
                   pl.BlockSpec((L, dv_all), lambda b_, c: (rowblk(b_, 1, c), 0))],
        out_shape=[out_shape, out_shape],
        scratch_shapes=[pltpu.VMEM((2, M_HEADS, dqk, dv), F32), pltpu.VMEM((2, M_HEADS, 1, dqk), F32),
                        pltpu.VMEM((2, M_HEADS, 1, 1), F32)],
        compiler_params=pltpu.CompilerParams(dimension_semantics=("arbitrary",) * 2,
                                             vmem_limit_bytes=_vmem_limit(40 * 2**20)),
        name="mlstm_scan",
    )(q, k, v, gates, q, k, v, gates)


def _mlstm_post_kernel(x_ref, s0_ref, s1_ref, mod_ref, wog_ref, ng_ref, wout_ref, g_ref, b_ref, o_ref, *, alpha):
    d = x_ref.shape[1]
    chunks = [slice(r0, r0 + HEAD_ROWS) for r0 in range(0, TM, HEAD_ROWS)]
    h = [_mod(x_ref[r, :], mod_ref, 0, d).astype(MXU) for r in chunks]
    og = [_sigmoid(_dot(hc, wog_ref[...])) for hc in h]
    a = [_head_rms(og[i] * (s0_ref[r, :] + s1_ref[r, :]), ng_ref[...], M_HEADS).astype(MXU) for i, r in enumerate(chunks)]
    _residual_ln(x_ref, mod_ref, jnp.concatenate(a, axis=0), wout_ref, g_ref, b_ref, o_ref, alpha)


def _mlstm_post(rs, n_tiles, x, s, mod, w_og, norm_g, w_out, ln_g, ln_b, alpha):
    d, inner = w_og.shape
    body = functools.partial(_mlstm_post_kernel, alpha=alpha)
    vmem = 2 * (w_og.size + w_out.size) + 10 * TM * inner * 4 + 16 * 2**20
    return _row_call(
        body, rs, n_tiles,
        [rs.tile(d), rs.tile(inner), rs.tile(inner), rs.mod(6 * d), _const_spec(w_og.shape), _const_spec((1, inner)),
         _const_spec(w_out.shape), _const_spec((1, d)), _const_spec((1, d))],
        rs.tile(d), jax.ShapeDtypeStruct((n_tiles * TM, d), F32), vmem=vmem, name="mlstm_post",
    )(x, s[0], s[1], mod, w_og, norm_g.reshape(1, inner), w_out, ln_g.reshape(1, d), ln_b.reshape(1, d))


def _mla_pre_kernel(x_ref, mod_ref, cos_ref, sin_ref, wdq_ref, qn_ref, wuq_ref, wdkv_ref, kvn_ref, wk_ref, wv_ref,
                    q_ref, k_ref, v_ref, *, q_scale):
    d = x_ref.shape[1]
    hd = A_DNOPE + LANES
    h = _mod(x_ref[...], mod_ref, 0, d).astype(MXU)
    cos, sin = cos_ref[...], sin_ref[...]
    cq = _rms(_dot(h, wdq_ref[...]), qn_ref[...]).astype(MXU)
    qa = _dot(cq, wuq_ref[...])
    part0 = A_HEADS * hd
    cos_q, sin_q = cos * q_scale, sin * q_scale
    for hh in range(A_HEADS):
        q_ref[:, hh * hd:hh * hd + A_DNOPE] = (qa[:, hh * hd:hh * hd + A_DNOPE] * q_scale).astype(q_ref.dtype)
        rot = qa[:, hh * hd + A_DNOPE:(hh + 1) * hd] * cos_q + qa[:, part0 + hh * LANES:part0 + (hh + 1) * LANES] * sin_q
        q_ref[:, hh * hd + A_DNOPE:(hh + 1) * hd] = rot.astype(q_ref.dtype)
    dk = _dot(h, wdkv_ref[...])
    k_rope = (dk[:, A_KVLORA:A_KVLORA + LANES] * cos + dk[:, A_KVLORA + LANES:] * sin).astype(k_ref.dtype)
    ckv = _rms(dk[:, :A_KVLORA], kvn_ref[...]).astype(MXU)
    k_nope = _dot(ckv, wk_ref[...])
    for hh in range(A_HEADS):
        k_ref[:, hh * hd:hh * hd + A_DNOPE] = k_nope[:, hh * A_DNOPE:(hh + 1) * A_DNOPE].astype(k_ref.dtype)
        k_ref[:, hh * hd + A_DNOPE:(hh + 1) * hd] = k_rope
    v_ref[...] = _dot(ckv, wv_ref[...]).astype(v_ref.dtype)


def _rope_tables(seq):
    n_freq = A_DROPE // 4
    inv_freq = ROPE_BASE ** (-jnp.arange(n_freq, dtype=F32) / n_freq)
    pos = jnp.arange(seq)
    ang_row = (pos // GRID_W).astype(F32)[:, None] * inv_freq
    ang_col = (pos % GRID_W).astype(F32)[:, None] * inv_freq
    cos = jnp.concatenate([jnp.cos(ang_row)] * 2 + [jnp.cos(ang_col)] * 2, axis=1)
    sin = jnp.concatenate([-jnp.sin(ang_row), jnp.sin(ang_row), -jnp.sin(ang_col), jnp.sin(ang_col)], axis=1)
    pad = LANES - A_DROPE
    cos = jnp.pad(cos, ((0, 0), (0, pad)), constant_values=1.0)
    sin = jnp.pad(sin, ((0, 0), (0, pad)))
    ident = (jnp.ones((TM, LANES), F32), jnp.zeros((TM, LANES), F32))
    return jnp.concatenate([ident[0], cos]), jnp.concatenate([ident[1], sin])


def _rope_partner_cols(w):
    idx = np.arange(A_DROPE)
    half = A_DROPE // 4
    partner = np.where((idx % (2 * half)) < half, idx + half, idx - half)
    return w[..., partner]


def _mla_pre(rs, x, mod, w_dq, q_norm, w_uq, w_dkv, kv_norm, w_ukv):
    d, qlora = w_dq.shape
    hd = A_DNOPE + LANES
    pad = LANES - A_DROPE
    wq = w_uq.reshape(qlora, A_HEADS, A_DNOPE + A_DROPE)
    wq_main = jnp.pad(wq, ((0, 0), (0, 0), (0, pad))).reshape(qlora, A_HEADS * hd)
    wq_part = jnp.pad(_rope_partner_cols(wq[..., A_DNOPE:]), ((0, 0), (0, 0), (0, pad))).reshape(qlora, A_HEADS * LANES)
    wuq_all = jnp.concatenate([wq_main, wq_part], axis=1).astype(MXU)
    w_kr = w_dkv[:, A_KVLORA:]
    wdkv_all = jnp.concatenate([w_dkv[:, :A_KVLORA], jnp.pad(w_kr, ((0, 0), (0, pad))),
                                jnp.pad(_rope_partner_cols(w_kr), ((0, 0), (0, pad)))], axis=1).astype(MXU)
    wkv = w_ukv.reshape(A_KVLORA, A_HEADS, A_DNOPE + A_DV)
    w_k = wkv[..., :A_DNOPE].reshape(A_KVLORA, A_HEADS * A_DNOPE).astype(MXU)
    w_v = wkv[..., A_DNOPE:].reshape(A_KVLORA, A_HEADS * A_DV).astype(MXU)
    cos, sin = _rope_tables(rs.seq)
    nl, tps = rs.n_lat_tiles, rs.tiles_per_seq
    tab = pl.BlockSpec((TM, LANES), lambda i: (jnp.where(i >= nl, 0, 1 + i % tps), 0))
    n, r = rs.n_tiles, rs.rows
    vmem = 2 * 2 * (w_dq.size + wuq_all.size + wdkv_all.size + w_k.size + w_v.size) + 12 * TM * A_HEADS * hd * 4
    return _row_call(
        functools.partial(_mla_pre_kernel, q_scale=float((A_DNOPE + A_DROPE) ** -0.5 * np.log2(np.e))), rs, n,
        [rs.tile(d), rs.mod(6 * d), tab, tab, _const_spec(w_dq.shape), _const_spec((1, qlora)), _const_spec(wuq_all.shape),
         _const_spec(wdkv_all.shape), _const_spec((1, A_KVLORA)), _const_spec(w_k.shape), _const_spec(w_v.shape)],
        [rs.tile(A_HEADS * hd), rs.tile(A_HEADS * hd), rs.tile(A_HEADS * A_DV)],
        [jax.ShapeDtypeStruct((r, A_HEADS * hd), MXU), jax.ShapeDtypeStruct((r, A_HEADS * hd), MXU),
         jax.ShapeDtypeStruct((r, A_HEADS * A_DV), MXU)],
        vmem=vmem, name="mla_pre",
    )(x, mod, cos, sin, w_dq.astype(MXU), q_norm.reshape(1, qlora), wuq_all, wdkv_all, kv_norm.reshape(1, A_KVLORA), w_k, w_v)


def _attn_kernel(*refs, with_latent, kv_chunk):
    if with_latent:
        q_ref, kc_ref, vc_ref, kl_ref, vl_ref, o_ref = refs
    else:
        q_ref, kc_ref, vc_ref, o_ref = refs
    hd, dv = A_DNOPE + LANES, A_DV
    heads = range(q_ref.shape[1] // hd)
    qs = lambda h: slice(h * hd, (h + 1) * hd)
    vs = lambda h: slice(h * dv, (h + 1) * dv)
    q = [q_ref[:, qs(h)] for h in heads]
    s = [_dot_nt(q[h], kc_ref[:, qs(h)]) for h in heads]
    m = [jnp.max(s[h], axis=1, keepdims=True) for h in heads]
    p = [jnp.exp2(s[h] - m[h]) for h in heads]
    l = [jnp.sum(p[h], axis=1, keepdims=True) for h in heads]
    acc = [_dot(p[h].astype(MXU), vc_ref[:, vs(h)]) for h in heads]
    if with_latent:
        for c0 in range(0, kl_ref.shape[0], kv_chunk):
            s = [_dot_nt(q[h], kl_ref[c0:c0 + kv_chunk, qs(h)]) for h in heads]
            m_new = [jnp.maximum(m[h], jnp.max(s[h], axis=1, keepdims=True)) for h in heads]
            corr = [jnp.exp2(m[h] - m_new[h]) for h in heads]
            p = [jnp.exp2(s[h] - m_new[h]) for h in heads]
            l = [l[h] * corr[h] + jnp.sum(p[h], axis=1, keepdims=True) for h in heads]
            acc = [acc[h] * corr[h] + _dot(p[h].astype(MXU), vl_ref[c0:c0 + kv_chunk, vs(h)]) for h in heads]
            m = m_new
    for h in heads:
        o_ref[:, vs(h)] = (acc[h] / l[h]).astype(o_ref.dtype)


def _mla_attention(rs, q, k, v):
    b, t = rs.batch, rs.seq
    hpb = 2
    hd, dv = hpb * (A_DNOPE + LANES), hpb * A_DV
    kv_chunk = min(1024, t)
    assert t % kv_chunk == 0 and A_HEADS % hpb == 0
    tc = rs.ctx_len
    ctx_blk0 = b * t // tc
    params = pltpu.CompilerParams(dimension_semantics=("arbitrary",) * 3, vmem_limit_bytes=_vmem_limit(48 * 2**20))
    o_ctx = pl.pallas_call(
        functools.partial(_attn_kernel, with_latent=False, kv_chunk=kv_chunk), grid=(b, A_HEADS // hpb, 1),
        in_specs=[pl.BlockSpec((tc, hd), lambda b_, h, i: (ctx_blk0 + b_, h)),
                  pl.BlockSpec((tc, hd), lambda b_, h, i: (ctx_blk0 + b_, h)),
                  pl.BlockSpec((tc, dv), lambda b_, h, i: (ctx_blk0 + b_, h))],
        out_specs=pl.BlockSpec((tc, dv), lambda b_, h, i: (b_, h)),
        out_shape=jax.ShapeDtypeStruct((b * tc, A_HEADS * A_DV), MXU), compiler_params=params, name="mla_attn_ctx",
    )(q, k, v)
    tq = ATTN_TQ
    nq = t // tq
    o_lat = pl.pallas_call(
        functools.partial(_attn_kernel, with_latent=True, kv_chunk=kv_chunk), grid=(b, A_HEADS // hpb, nq),
        in_specs=[pl.BlockSpec((tq, hd), lambda b_, h, i: (b_ * nq + i, h)),
                  pl.BlockSpec((tc, hd), lambda b_, h, i: (ctx_blk0 + b_, h)),
                  pl.BlockSpec((tc, dv), lambda b_, h, i: (ctx_blk0 + b_, h)),
                  pl.BlockSpec((t, hd), lambda b_, h, i: (b_, h)),
                  pl.BlockSpec((t, dv), lambda b_, h, i: (b_, h))],
        out_specs=pl.BlockSpec((tq, dv), lambda b_, h, i: (b_ * nq + i, h)),
        out_shape=jax.ShapeDtypeStruct((b * t, A_HEADS * A_DV), MXU), compiler_params=params, name="mla_attn_latent",
    )(q, k, v, k, v)
    return o_lat, o_ctx


def _proj_post_kernel(x_ref, al_ref, ac_ref, mod_ref, wout_ref, g_ref, b_ref, o_ref, *, alpha, n_lat_tiles):
    a = jnp.where(pl.program_id(0) >= n_lat_tiles, ac_ref[...], al_ref[...])
    _residual_ln(x_ref, mod_ref, a, wout_ref, g_ref, b_ref, o_ref, alpha)


def _proj_post(rs, n_tiles, x, a_lat, a_ctx, mod, w_out, ln_g, ln_b, alpha):
    kdim, d = w_out.shape
    nl = rs.n_lat_tiles
    body = functools.partial(_proj_post_kernel, alpha=alpha, n_lat_tiles=nl)
    return _row_call(
        body, rs, n_tiles,
        [rs.tile(d), pl.BlockSpec((TM, kdim), lambda i: (jnp.minimum(i, nl - 1), 0)),
         pl.BlockSpec((TM, kdim), lambda i: (jnp.maximum(i - nl, 0), 0)), rs.mod(6 * d), _const_spec(w_out.shape),
         _const_spec((1, d)), _const_spec((1, d))],
        rs.tile(d), jax.ShapeDtypeStruct((n_tiles * TM, d), F32), vmem=32 * 2**20, name="proj_post",
    )(x, a_lat, a_ctx, mod, w_out, ln_g.reshape(1, d), ln_b.reshape(1, d))


def _gla_pre_kernel(x_ref, mod_ref, wqk_ref, wv_ref, wa1_ref, wa2_ref, ba_ref, q_ref, k_ref, v_ref, la_ref,
                    *, q_scale):
    d = x_ref.shape[1]
    half = q_ref.shape[1]
    chunks = [slice(r0, r0 + HEAD_ROWS) for r0 in range(0, TM, HEAD_ROWS)]
    h = [_mod(x_ref[r, :], mod_ref, 0, d).astype(MXU) for r in chunks]
    a1 = [_dot(hc, wa1_ref[...]).astype(MXU) for hc in h]
    z = [_dot(a, wa2_ref[...]) + ba_ref[...] for a in a1]
    qk = [_dot(hc, wqk_ref[...]) for hc in h]
    for i, r in enumerate(chunks):
        la_ref[r, :] = _log_sigmoid(z[i]) * (1.0 / G_TAU)
        q_ref[r, :] = qk[i][:, :half] * q_scale
        k_ref[r, :] = qk[i][:, half:]
    for i, r in enumerate(chunks):
        v_ref[r, :] = _dot(h[i], wv_ref[...]).astype(v_ref.dtype)


def _gla_pre(rs, x, mod, w_qk, w_v, w_a1, w_a2, b_a):
    d = w_qk.shape[0]
    dk_all = w_qk.shape[1] // 2
    wa1 = jnp.pad(jnp.concatenate([w_a1[0], w_a1[1]], axis=1), ((0, 0), (0, LANES - 2 * G_RANK))).astype(MXU)
    wa2 = jnp.zeros((LANES, 2 * dk_all), F32)
    wa2 = wa2.at[:G_RANK, :dk_all].set(w_a2[0]).at[G_RANK:2 * G_RANK, dk_all:].set(w_a2[1]).astype(MXU)
    ba = jnp.concatenate([b_a[0], b_a[1]]).reshape(1, 2 * dk_all)
    body = functools.partial(_gla_pre_kernel, q_scale=float((dk_all // G_HEADS) ** -0.5))
    n, r = rs.n_tiles, rs.rows
    return _row_call(
        body, rs, n,
        [rs.tile(d), rs.mod(6 * d), _const_spec(w_qk.shape), _const_spec(w_v.shape), _const_spec(wa1.shape),
         _const_spec(wa2.shape), _const_spec(ba.shape)],
        [rs.tile(dk_all), rs.tile(dk_all), rs.tile(w_v.shape[1]), rs.tile(2 * dk_all)],
        [jax.ShapeDtypeStruct((r, dk_all), F32), jax.ShapeDtypeStruct((r, dk_all), F32),
         jax.ShapeDtypeStruct((r, w_v.shape[1]), MXU), jax.ShapeDtypeStruct((r, 2 * dk_all), F32)],
        vmem=40 * 2**20, name="gla_pre",
    )(x, mod, w_qk, w_v, wa1, wa2, ba)


def _gla_tables(L):
    t = np.arange(L)
    tau, taup = t[:, None], t[None, :]
    groups = [taup <= tau, taup > tau]
    masks = []
    c = L // 2
    while c >= 1:
        blk = t // (2 * c)
        mid = blk * 2 * c + c
        second = (t % (2 * c)) >= c
        q_side = second[:, None] & (taup >= mid[:, None]) & (taup <= tau)
        k_side = (~second)[:, None] & (taup > tau) & (taup <= mid[:, None] - 1)
        groups.append(q_side | k_side)
        masks.append(second[:, None] & (~second)[None, :] & (blk[:, None] == blk[None, :]))
        c //= 2
    flip = lambda g: g[::-1, ::-1]
    sums = np.stack([np.concatenate(groups, axis=0), np.concatenate([flip(g) for g in groups], axis=0)])
    lvl = np.stack([np.stack(masks), np.stack([flip(m) for m in masks])])
    return sums.astype(np.float32), lvl.astype(np.float32)


def _gla_scan_kernel(q0_ref, k0_ref, v0_ref, la0_ref, q1_ref, k1_ref, v1_ref, la1_ref, sums_ref, masks_ref,
                     o0_ref, o1_ref, st_scr):
    @pl.when(pl.program_id(1) == 0)
    def _():
        st_scr[...] = jnp.zeros_like(st_scr)

    L = q0_ref.shape[0]
    dk, dv = q0_ref.shape[1] // G_HEADS, v0_ref.shape[1] // G_HEADS
    levels = masks_ref.shape[1]
    dirs = ((q0_ref, k0_ref, v0_ref, la0_ref, o0_ref), (q1_ref, k1_ref, v1_ref, la1_ref, o1_ref))
    chains = [(d, h) for d in range(2) for h in range(G_HEADS)]
    row = lax.broadcasted_iota(jnp.int32, (L, L), 0)
    col = lax.broadcasted_iota(jnp.int32, (L, L), 1)
    eye = row == col

    st_prev = {(d, h): st_scr[d, h] for d, h in chains}
    e = [jnp.exp(_dot(sums_ref[d], jnp.concatenate(_split2(dirs[d][3][...]), axis=0))) for d in range(2)]
    q = [dirs[d][0][...] for d in range(2)]
    k = [dirs[d][1][...] for d in range(2)]
    qe = [(q[d] * e[d][0:L]).astype(MXU) for d in range(2)]
    kd = [(k[d] * e[d][L:2 * L]).astype(MXU) for d in range(2)]
    qk_diag = [q[d] * k[d] for d in range(2)]
    qt = [[(q[d] * e[d][(2 + lv) * L:(3 + lv) * L]).astype(MXU) for lv in range(levels)] for d in range(2)]
    kt = [[(k[d] * e[d][(2 + lv) * L:(3 + lv) * L]).astype(MXU) for lv in range(levels)] for d in range(2)]
    decay = [jnp.exp(jnp.sum(dirs[d][3][...], axis=0, keepdims=True)) for d in range(2)]

    att = {}
    for d, h in chains:
        ks = slice(h * dk, (h + 1) * dk)
        a = jnp.where(eye, jnp.sum(qk_diag[d][:, ks], axis=1, keepdims=True), 0.0)
        for lv in range(levels):
            a = a + masks_ref[d, lv] * _dot_nt(qt[d][lv][:, ks], kt[d][lv][:, ks])
        att[d, h] = a.astype(MXU)
    for d, h in chains:
        ks, vs = slice(h * dk, (h + 1) * dk), slice(h * dv, (h + 1) * dv)
        v = dirs[d][2][:, vs]
        dirs[d][4][:, vs] = _dot_nt(qe[d][:, ks], st_prev[d, h].astype(MXU)) + _dot(att[d, h], v)
        st_scr[d, h] = st_prev[d, h] * decay[d][:, ks] + _dot_tn(v, kd[d][:, ks])


def _gla_scan(rs, q, k, v, la):
    b, L = rs.batch, G_CHUNK
    dk_all, dv_all = q.shape[1], v.shape[1]
    nc_ctx, nc_lat = rs.ctx_len // L, rs.seq // L
    rowblk = _scan_rowblock(b, nc_ctx, nc_lat)
    sums, masks = _gla_tables(L)
    sums, masks = jnp.asarray(np.concatenate([sums, sums], axis=2), MXU), jnp.asarray(masks, F32)

    def specs(d):
        rb = lambda b_, c: (rowblk(b_, d, c), 0)
        return [pl.BlockSpec((L, dk_all), rb), pl.BlockSpec((L, dk_all), rb), pl.BlockSpec((L, dv_all), rb),
                pl.BlockSpec((L, dk_all), lambda b_, c: (rowblk(b_, d, c), d))]

    out_shape = jax.ShapeDtypeStruct((rs.rows, dv_all), F32)
    return pl.pallas_call(
        _gla_scan_kernel, grid=(b, nc_ctx + nc_lat),
        in_specs=specs(0) + specs(1) + [_const_spec(sums.shape), _const_spec(masks.shape)],
        out_specs=[pl.BlockSpec((L, dv_all), lambda b_, c: (rowblk(b_, 0, c), 0)),
                   pl.BlockSpec((L, dv_all), lambda b_, c: (rowblk(b_, 1, c), 0))],
        out_shape=[out_shape, out_shape],
        scratch_shapes=[pltpu.VMEM((2, G_HEADS, dv_all // G_HEADS, dk_all // G_HEADS), F32)],
        compiler_params=pltpu.CompilerParams(dimension_semantics=("arbitrary",) * 2,
                                             vmem_limit_bytes=_vmem_limit(32 * 2**20)),
        name="gla_scan",
    )(q, k, v, la, q, k, v, la, sums, masks)


def _gla_post_kernel(x_ref, s0_ref, s1_ref, mod_ref, wr_ref, ng_ref, wout_ref, g_ref, b_ref, o_ref, *, alpha):
    d = x_ref.shape[1]
    chunks = [slice(r0, r0 + HEAD_ROWS) for r0 in range(0, TM, HEAD_ROWS)]
    h = [_mod(x_ref[r, :], mod_ref, 0, d).astype(MXU) for r in chunks]
    gate = [_silu(_dot(hc, wr_ref[...])) for hc in h]
    a = [(_head_rms(s0_ref[r, :] + s1_ref[r, :], ng_ref[...], G_HEADS) * gate[i]).astype(MXU) for i, r in enumerate(chunks)]
    _residual_ln(x_ref, mod_ref, jnp.concatenate(a, axis=0), wout_ref, g_ref, b_ref, o_ref, alpha)


def _gla_post(rs, n_tiles, x, s, mod, w_r, norm_g, w_out, ln_g, ln_b, alpha):
    d, dv_all = w_r.shape
    body = functools.partial(_gla_post_kernel, alpha=alpha)
    return _row_call(
        body, rs, n_tiles,
        [rs.tile(d), rs.tile(dv_all), rs.tile(dv_all), rs.mod(6 * d), _const_spec(w_r.shape), _const_spec((1, dv_all)), _const_spec(w_out.shape),
         _const_spec((1, d)), _const_spec((1, d))],
        rs.tile(d), jax.ShapeDtypeStruct((n_tiles * TM, d), F32), vmem=40 * 2**20, name="gla_post",
    )(x, s[0], s[1], mod, w_r, norm_g.reshape(1, dv_all), w_out, ln_g.reshape(1, d), ln_b.reshape(1, d))


def kernel(x, c, ctx, c_ctx, ada_w, ada_b, ln_g, ln_b, ffn_w_in, ffn_conv_w, ffn_conv_b, ffn_w_out, m_w_up, m_conv_w, m_conv_b, m_w_qk, m_w_v, m_w_gates, m_b_gates, m_w_og, m_norm_g, m_w_out, a_w_dq, a_q_norm, a_w_uq, a_w_dkv, a_kv_norm, a_w_ukv, a_w_out, g_w_qk, g_w_v, g_w_r, g_w_a1, g_w_a2, g_b_a, g_norm_g, g_w_out):
    batch, seq, d = x.shape
    depth = ada_w.shape[0]
    n_mixers = 3
    alpha = float((2 * depth) ** 0.25)
    rs = _Rows(batch, seq, ctx.shape[1])
    bf = lambda w: w.astype(MXU)

    cond_rows = -(-(batch + 1) // SUBLANES) * SUBLANES
    cond = jnp.zeros((cond_rows, d), F32).at[:batch].set(c).at[batch].set(c_ctx)
    mods = _modulation(cond, ada_w, ada_b).reshape(depth, cond_rows, 1, 6 * d)

    xa = jnp.concatenate([x.reshape(batch * seq, d), ctx.reshape(-1, d)], axis=0)
    for i in range(depth):
        need_ctx = i < depth - 1
        n_tiles = rs.n_tiles if need_ctx else rs.n_lat_tiles
        kind, j = i % n_mixers, i // n_mixers
        mod = mods[i]
        if kind == 0:
            q, k, v, gates = _mlstm_pre(rs, xa, mod, bf(m_w_up[j]), m_conv_w[j], m_conv_b[j], bf(m_w_qk[j]),
                                        _weight_product(m_w_up[j], m_w_v[j]), m_w_gates[j], m_b_gates[j])
            s = _mlstm_scan(rs, q, k, v, gates)
            xa = _mlstm_post(rs, n_tiles, xa, s, mod, bf(m_w_og[j]), m_norm_g[j], bf(m_w_out[j]), ln_g[i, 0], ln_b[i, 0], alpha)
        elif kind == 1:
            q, k, v = _mla_pre(rs, xa, mod, a_w_dq[j], a_q_norm[j], a_w_uq[j], a_w_dkv[j], a_kv_norm[j], a_w_ukv[j])
            o_lat, o_ctx = _mla_attention(rs, q, k, v)
            xa = _proj_post(rs, n_tiles, xa, o_lat, o_ctx, mod, bf(a_w_out[j]), ln_g[i, 0], ln_b[i, 0], alpha)
        else:
            q, k, v, la = _gla_pre(rs, xa, mod, bf(g_w_qk[j]), bf(g_w_v[j]), g_w_a1[j], g_w_a2[j], g_b_a[j])
            s = _gla_scan(rs, q, k, v, la)
            xa = _gla_post(rs, n_tiles, xa, s, mod, bf(g_w_r[j]), g_norm_g[j], bf(g_w_out[j]), ln_g[i, 0], ln_b[i, 0], alpha)
        xa = _ffn(rs, n_tiles, xa, mod, bf(ffn_w_in[i]), ffn_conv_w[i], ffn_conv_b[i], bf(ffn_w_out[i]),
                  ln_g[i, 1], ln_b[i, 1], alpha)
    return xa.reshape(batch, seq, d)
```

```python
import functools

import numpy as np
import jax
import jax.numpy as jnp
from jax import lax
from jax.experimental import pallas as pl
from jax.experimental.pallas import tpu as pltpu

F32 = jnp.float32
MXU = jnp.bfloat16

V7X_VMEM_BYTES = 64 * 2**20
SUBLANES = 8
LANES = 128

TM = 512
FFN_TM = 1024
ATTN_TQ = 1024
HEAD_ROWS = 256
TAIL_ROWS = 128
HALO = SUBLANES
GRID_W = 64
EPS = 1e-6
ROPE_BASE = 10000.0
G_TAU = 16.0

M_HEADS, A_HEADS, G_HEADS = 4, 8, 4
A_DNOPE, A_DROPE, A_DV, A_KVLORA = 128, 64, 128, 256
G_RANK = 16
M_CHUNK = 256
M_SCAN_GROUP = 8
G_CHUNK = 128


def _vmem_limit(nbytes):
    return int(min(max(nbytes, 16 * 2**20), V7X_VMEM_BYTES - 8 * 2**20))


def _const_spec(shape):
    nd = len(shape)
    return pl.BlockSpec(shape, lambda *_: (0,) * nd, pipeline_mode=pl.Buffered(1))


def _dot(a, b):
    return jnp.dot(a, b, preferred_element_type=F32)


def _dot_nt(a, b):
    return lax.dot_general(a, b, (((1,), (1,)), ((), ())), preferred_element_type=F32)


def _dot_tn(a, b):
    return lax.dot_general(a, b, (((0,), (0,)), ((), ())), preferred_element_type=F32)


def _split3(x):
    hi = x.astype(MXU)
    r1 = x - hi.astype(F32)
    mid = r1.astype(MXU)
    lo = (r1 - mid.astype(F32)).astype(MXU)
    return hi, mid, lo


def _split2(x):
    hi = x.astype(MXU)
    return hi, (x - hi.astype(F32)).astype(MXU)


def _sigmoid(x):
    return 1.0 / (1.0 + jnp.exp(-x))


def _silu(x):
    return x * _sigmoid(x)


def _log_sigmoid(x):
    return jnp.minimum(x, 0.0) - jnp.log1p(jnp.exp(-jnp.abs(x)))


def _layer_norm(z, g, b):
    mu = jnp.mean(z, -1, keepdims=True)
    zc = z - mu
    var = jnp.mean(zc * zc, -1, keepdims=True)
    return zc * lax.rsqrt(var + EPS) * g + b


def _rms(x, g):
    return x * lax.rsqrt(jnp.mean(x * x, -1, keepdims=True) + EPS) * g


def _head_rms(x, g, heads):
    d = x.shape[-1] // heads
    return jnp.concatenate([_rms(x[:, h * d:(h + 1) * d], g[:, h * d:(h + 1) * d]) for h in range(heads)], axis=-1)


def _mod(x, mod_ref, k, d):
    return x * (1.0 + mod_ref[:, (k + 1) * d:(k + 2) * d]) + mod_ref[:, k * d:(k + 1) * d]


def _halo_rows(xp_ref, x_ref, xn_ref, mod_ref, k, d, geom):
    n_lat_tiles, tiles_per_seq, _, _ = geom
    i = pl.program_id(0)
    is_ctx = i >= n_lat_tiles
    pos = i % tiles_per_seq
    first = jnp.logical_or(is_ctx, pos == 0)
    last = jnp.logical_or(is_ctx, pos == tiles_per_seq - 1)
    hp = jnp.where(first, 0.0, _mod(xp_ref[...], mod_ref, k, d))
    hn = jnp.where(last, 0.0, _mod(xn_ref[...], mod_ref, k, d))
    return jnp.concatenate([hp, _mod(x_ref[...], mod_ref, k, d), hn], axis=0)


def _seq_edges(geom):
    n_lat_tiles, _, ctx_len, tm = geom
    is_ctx = pl.program_id(0) >= n_lat_tiles
    r = lax.broadcasted_iota(jnp.int32, (tm, 1), 0) % ctx_len
    return jnp.logical_and(is_ctx, r == 0), jnp.logical_and(is_ctx, r == ctx_len - 1)


def _dwconv3(g_ext, w_ref, b_ref, c0, c1, edges):
    n = g_ext.shape[0] - 2 * HALO
    starts, ends = edges
    prev = jnp.where(starts, 0.0, g_ext[HALO - 1:HALO - 1 + n])
    nxt = jnp.where(ends, 0.0, g_ext[HALO + 1:HALO + 1 + n])
    return (w_ref[0:1, c0:c1] * prev + w_ref[1:2, c0:c1] * g_ext[HALO:HALO + n] + w_ref[2:3, c0:c1] * nxt
            + b_ref[:, c0:c1])


class _Rows:
    def __init__(self, batch, seq, ctx_len, tm):
        assert tm % ctx_len == 0 and (batch * ctx_len) % tm == 0 and seq % tm == 0
        self.batch, self.seq, self.ctx_len, self.tm = batch, seq, ctx_len, tm
        self.tiles_per_seq = seq // tm
        self.n_lat_tiles = batch * self.tiles_per_seq
        self.n_tiles = self.n_lat_tiles + batch * ctx_len // tm
        self.rows = self.n_tiles * tm
        self.geom = (self.n_lat_tiles, self.tiles_per_seq, ctx_len, tm)

    def tile(self, width):
        return pl.BlockSpec((self.tm, width), lambda i: (i, 0))

    def halo_prev(self, width):
        per = self.tm // HALO
        return pl.BlockSpec((HALO, width), lambda i: (jnp.maximum(i * per - 1, 0), 0))

    def halo_next(self, width, n_rows):
        per, nblk = self.tm // HALO, n_rows // HALO
        return pl.BlockSpec((HALO, width), lambda i: (jnp.minimum((i + 1) * per, nblk - 1), 0))

    def mod(self, width):
        nl, tps, b = self.n_lat_tiles, self.tiles_per_seq, self.batch
        return pl.BlockSpec((None, 1, width), lambda i: (jnp.where(i >= nl, b, i // tps), 0, 0))


def _row_call(body, rows, n_tiles, in_specs, out_specs, out_shape, scratch=(), vmem=0, name=None):
    return pl.pallas_call(
        body, grid=(n_tiles,), in_specs=in_specs, out_specs=out_specs, out_shape=out_shape,
        scratch_shapes=list(scratch), name=name,
        compiler_params=pltpu.CompilerParams(dimension_semantics=("arbitrary",), vmem_limit_bytes=_vmem_limit(vmem)))


def _modulation_kernel(c_ref, w_ref, b_ref, o_ref):
    o_ref[...] = _dot(_silu(c_ref[...]).astype(MXU), w_ref[...].astype(MXU)) + b_ref[...]


def _modulation(cond, ada_w, ada_b):
    depth, d, n = ada_w.shape
    tn = d
    return pl.pallas_call(
        _modulation_kernel, grid=(depth, n // tn),
        in_specs=[pl.BlockSpec(cond.shape, lambda l, j: (0, 0)),
                  pl.BlockSpec((None, d, tn), lambda l, j: (l, 0, j)),
                  pl.BlockSpec((None, 1, tn), lambda l, j: (l, 0, j))],
        out_specs=pl.BlockSpec((None, cond.shape[0], tn), lambda l, j: (l, 0, j)),
        out_shape=jax.ShapeDtypeStruct((depth, cond.shape[0], n), F32), name="modulation",
    )(cond, ada_w, ada_b.reshape(depth, 1, n))


def _ffn_kernel(xp_ref, x_ref, xn_ref, mod_ref, win_ref, cw_ref, cb_ref, wout_ref, g_ref, b_ref, o_ref, a_scr,
                *, geom, alpha, ffn, tf):
    tm, d = x_ref.shape
    hext = _halo_rows(xp_ref, x_ref, xn_ref, mod_ref, 3, d, geom).astype(MXU)
    edges = _seq_edges(geom)
    for f0 in range(0, ffn, tf):
        g_ext = _dot(hext, win_ref[:, f0:f0 + tf])
        up = _dot(hext, win_ref[:, ffn + f0:ffn + f0 + tf])[HALO:HALO + tm]
        a_scr[:, f0:f0 + tf] = (_silu(_dwconv3(g_ext, cw_ref, cb_ref, f0, f0 + tf, edges)) * up).astype(MXU)
    for r0 in range(0, tm, TAIL_ROWS):
        rows = slice(r0, r0 + TAIL_ROWS)
        f = _dot(a_scr[rows, :], wout_ref[...])
        z = alpha * x_ref[rows, :] + mod_ref[:, 5 * d:6 * d] * f
        o_ref[rows, :] = _layer_norm(z, g_ref[...], b_ref[...])


def _ffn(rs, n_tiles, x, mod, w_in, conv_w, conv_b, w_out, ln_g, ln_b, alpha):
    d, ffn = w_out.shape[1], w_out.shape[0]
    tf = 256
    assert ffn % tf == 0
    body = functools.partial(_ffn_kernel, geom=rs.geom, alpha=alpha, ffn=ffn, tf=tf)
    vmem = 2 * (w_in.size + w_out.size) + 6 * rs.tm * d * 4 + rs.tm * ffn * 2 + 24 * 2**20
    return _row_call(
        body, rs, n_tiles,
        [rs.halo_prev(d), rs.tile(d), rs.halo_next(d, x.shape[0]), rs.mod(6 * d), _const_spec(w_in.shape), _const_spec(conv_w.shape),
         _const_spec((1, ffn)), _const_spec(w_out.shape), _const_spec((1, d)), _const_spec((1, d))],
        rs.tile(d), jax.ShapeDtypeStruct((n_tiles * rs.tm, d), F32),
        scratch=[pltpu.VMEM((rs.tm, ffn), MXU)], vmem=vmem, name="conv_ffn",
    )(x, x, x, mod, w_in, conv_w, conv_b.reshape(1, ffn), w_out, ln_g.reshape(1, d), ln_b.reshape(1, d))


def _residual_ln(x_ref, mod_ref, a, wout_ref, g_ref, b_ref, o_ref, alpha):
    d = x_ref.shape[1]
    for r0 in range(0, TM, TAIL_ROWS):
        rows = slice(r0, r0 + TAIL_ROWS)
        z = alpha * x_ref[rows, :] + mod_ref[:, 2 * d:3 * d] * _dot(a[rows, :], wout_ref[...])
        o_ref[rows, :] = _layer_norm(z, g_ref[...], b_ref[...])


def _weight_product_kernel(a_ref, b_ref, o_ref):
    o_ref[...] = _dot(a_ref[...].astype(MXU), b_ref[...].astype(MXU)).astype(o_ref.dtype)


def _weight_product(a, b):
    m, kdim = a.shape
    n = b.shape[1]
    tn = 512
    assert n % tn == 0
    return pl.pallas_call(
        _weight_product_kernel, grid=(n // tn,),
        in_specs=[pl.BlockSpec((m, kdim), lambda j: (0, 0)), pl.BlockSpec((kdim, tn), lambda j: (0, j))],
        out_specs=pl.BlockSpec((m, tn), lambda j: (0, j)),
        out_shape=jax.ShapeDtypeStruct((m, n), MXU),
        compiler_params=pltpu.CompilerParams(dimension_semantics=("arbitrary",), vmem_limit_bytes=_vmem_limit(40 * 2**20)),
        name="weight_product",
    )(a, b)


def _mlstm_pre_kernel(xp_ref, x_ref, xn_ref, mod_ref, wup_ref, cw_ref, cb_ref, wqk_ref, wv_ref, wg_ref, bg_ref,
                      q_ref, k_ref, v_ref, gates_ref, *, geom, k_scale):
    d = x_ref.shape[1]
    hext = _halo_rows(xp_ref, x_ref, xn_ref, mod_ref, 0, d, geom).astype(MXU)
    xm_ext = _dot(hext, wup_ref[...])
    inner = xm_ext.shape[1]
    xc = _silu(_dwconv3(xm_ext, cw_ref, cb_ref, 0, inner, _seq_edges(geom))).astype(MXU)
    qk = _dot(xc, wqk_ref[...])
    half = qk.shape[1] // 2
    q_ref[...] = qk[:, :half].astype(q_ref.dtype)
    k_ref[...] = (qk[:, half:] * k_scale).astype(k_ref.dtype)
    v_ref[...] = _dot(_mod(x_ref[...], mod_ref, 0, d).astype(MXU), wv_ref[...]).astype(v_ref.dtype)
    two_h = 2 * M_HEADS
    gates = _dot(xc, wg_ref[...]) + bg_ref[...]
    lane = lax.broadcasted_iota(jnp.int32, gates.shape, 1)
    gates_ref[...] = jnp.where(lane % two_h >= M_HEADS, _log_sigmoid(gates), gates)


def _mlstm_pre(rs, x, mod, w_up, conv_w, conv_b, w_qk, w_v, w_gates, b_gates):
    d, inner = w_up.shape
    dqk_all = w_qk.shape[1] // 2
    n_g = 2 * 2 * M_HEADS
    wg = jnp.concatenate([w_gates[0], w_gates[1]], axis=1)
    wg_pad = jnp.pad(wg, ((0, 0), (0, LANES - n_g))).astype(MXU)
    bg = jnp.concatenate([b_gates[0], b_gates[1]])
    bg_pad = jnp.pad(bg, (0, LANES - n_g)).reshape(1, LANES)
    body = functools.partial(_mlstm_pre_kernel, geom=rs.geom, k_scale=float((dqk_all // M_HEADS) ** -0.5))
    n, r = rs.n_tiles, rs.rows
    vmem = 2 * (w_up.size + w_qk.size + w_v.size) + 8 * (TM + 2 * HALO) * inner * 4 + 16 * 2**20
    return _row_call(
        body, rs, n,
        [rs.halo_prev(d), rs.tile(d), rs.halo_next(d, x.shape[0]), rs.mod(6 * d), _const_spec(w_up.shape), _const_spec(conv_w.shape),
         _const_spec((1, inner)), _const_spec(w_qk.shape), _const_spec(w_v.shape), _const_spec((inner, LANES)),
         _const_spec((1, LANES))],
        [rs.tile(dqk_all), rs.tile(dqk_all), rs.tile(inner), rs.tile(LANES)],
        [jax.ShapeDtypeStruct((r, dqk_all), MXU), jax.ShapeDtypeStruct((r, dqk_all), MXU),
         jax.ShapeDtypeStruct((r, inner), MXU), jax.ShapeDtypeStruct((r, LANES), F32)],
        vmem=vmem, name="mlstm_pre",
    )(x, x, x, mod, w_up, conv_w, conv_b.reshape(1, inner), w_qk, w_v, wg_pad, bg_pad)


def _scan_rowblock(batch, nc_ctx, nc_lat):
    def rowblk(b, d, c):
        cc = jnp.where(d == 1, nc_ctx - 1 - c, c)
        lc = c - nc_ctx
        lc = jnp.where(d == 1, nc_lat - 1 - lc, lc)
        return jnp.where(c < nc_ctx, batch * nc_lat + b * nc_ctx + cc, b * nc_lat + lc)
    return rowblk


def _mlstm_scan_kernel(q0_ref, k0_ref, v0_ref, g0_ref, q1_ref, k1_ref, v1_ref, g1_ref,
                       o0_ref, o1_ref, c_scr, n_scr, m_scr):
    @pl.when(pl.program_id(1) == 0)
    def _():
        c_scr[...] = jnp.zeros_like(c_scr)
        n_scr[...] = jnp.zeros_like(n_scr)
        m_scr[...] = jnp.zeros_like(m_scr)

    L = q0_ref.shape[0]
    dqk, dv = q0_ref.shape[1] // M_HEADS, v0_ref.shape[1] // M_HEADS
    two_h = 2 * M_HEADS
    dirs = ((q0_ref, k0_ref, v0_ref, g0_ref, o0_ref), (q1_ref, k1_ref, v1_ref, g1_ref, o1_ref))
    chains = [(d, h) for d in range(2) for h in range(M_HEADS)]
    row = lax.broadcasted_iota(jnp.int32, (L, L), 0)
    col = lax.broadcasted_iota(jnp.int32, (L, L), 1)
    causal = [col <= row, col >= row]
    causal_t = [row <= col, row >= col]
    eye = jnp.where(row == col, 1.0, 0.0).astype(MXU)
    gates = [dirs[d][3][...] for d in range(2)]
    pieces = [_split3(gates[d]) for d in range(2)]
    cs = [sum(_dot(jnp.where(causal[d], 1.0, 0.0).astype(MXU), p) for p in pieces[d]) for d in range(2)]
    cs_t = [sum(_dot_tn(p, jnp.where(causal_t[d], 1.0, 0.0).astype(MXU)) for p in pieces[d]) for d in range(2)]
    gates_t = [sum(_dot_tn(p, eye) for p in pieces[d]) for d in range(2)]

    i_slot = lambda d, h: d * two_h + h
    f_slot = lambda d, h: d * two_h + M_HEADS + h
    qs = lambda h: slice(h * dqk, (h + 1) * dqk)
    vs = lambda h: slice(h * dv, (h + 1) * dv)
    for g0 in range(0, len(chains), M_SCAN_GROUP):
        _mlstm_advance(chains[g0:g0 + M_SCAN_GROUP], dirs, causal, gates, gates_t, cs, cs_t, i_slot, f_slot, qs, vs,
                       c_scr, n_scr, m_scr)


def _mlstm_advance(chains, dirs, causal, gates, gates_t, cs, cs_t, i_slot, f_slot, qs, vs, c_scr, n_scr, m_scr):
    def per_chain(fn):
        return {ch: fn(*ch) for ch in chains}

    c_prev = per_chain(lambda d, h: c_scr[d, h])
    n_prev = per_chain(lambda d, h: n_scr[d, h])
    m_prev = per_chain(lambda d, h: m_scr[d, h])
    q = per_chain(lambda d, h: dirs[d][0][:, qs(h)])
    k = per_chain(lambda d, h: dirs[d][1][:, qs(h)])
    v = per_chain(lambda d, h: dirs[d][2][:, vs(h)])
    li_col = per_chain(lambda d, h: gates[d][:, i_slot(d, h):i_slot(d, h) + 1])
    li_row = per_chain(lambda d, h: gates_t[d][i_slot(d, h):i_slot(d, h) + 1, :])
    b_col = per_chain(lambda d, h: cs[d][:, f_slot(d, h):f_slot(d, h) + 1])
    b_row = per_chain(lambda d, h: cs_t[d][f_slot(d, h):f_slot(d, h) + 1, :])
    bl = per_chain(lambda d, h: jnp.sum(gates_t[d][f_slot(d, h):f_slot(d, h) + 1, :], axis=1, keepdims=True))

    dmat = per_chain(lambda d, h: jnp.where(causal[d], b_col[d, h] - b_row[d, h] + li_row[d, h], -jnp.inf))
    inter = per_chain(lambda d, h: b_col[d, h] + m_prev[d, h])
    mj = per_chain(lambda d, h: jnp.maximum(inter[d, h], jnp.max(dmat[d, h], axis=1, keepdims=True)))
    qk = per_chain(lambda d, h: _dot_nt(q[d, h], k[d, h]))
    wmat = per_chain(lambda d, h: jnp.exp(dmat[d, h] - mj[d, h]) * qk[d, h])
    g = per_chain(lambda d, h: jnp.exp(inter[d, h] - mj[d, h]))
    qc = per_chain(lambda d, h: _dot(q[d, h], c_prev[d, h].astype(MXU)))
    wv = per_chain(lambda d, h: _dot(wmat[d, h].astype(MXU), v[d, h]))
    qn = per_chain(lambda d, h: jnp.sum(q[d, h].astype(F32) * n_prev[d, h], axis=1, keepdims=True))
    den = per_chain(lambda d, h: g[d, h] * qn[d, h] + jnp.sum(wmat[d, h], axis=1, keepdims=True))
    for d, h in chains:
        num = g[d, h] * qc[d, h] + wv[d, h]
        dirs[d][4][:, vs(h)] = num / jnp.maximum(jnp.abs(den[d, h]), jnp.exp(-mj[d, h]))

    ds = per_chain(lambda d, h: bl[d, h] - b_col[d, h] + li_col[d, h])
    m_new = per_chain(lambda d, h: jnp.maximum(bl[d, h] + m_prev[d, h], jnp.max(ds[d, h], axis=0, keepdims=True)))
    kw = per_chain(lambda d, h: k[d, h].astype(F32) * jnp.exp(ds[d, h] - m_new[d, h]))
    decay = per_chain(lambda d, h: jnp.exp(bl[d, h] + m_prev[d, h] - m_new[d, h]))
    kv = per_chain(lambda d, h: _dot_tn(kw[d, h].astype(MXU), v[d, h]))
    for d, h in chains:
        c_scr[d, h] = decay[d, h] * c_prev[d, h] + kv[d, h]
        n_scr[d, h] = decay[d, h] * n_prev[d, h] + jnp.sum(kw[d, h], axis=0, keepdims=True)
        m_scr[d, h] = m_new[d, h]


def _mlstm_scan(rs, q, k, v, gates):
    b, L = rs.batch, M_CHUNK
    dqk_all, dv_all = q.shape[1], v.shape[1]
    nc_ctx, nc_lat = rs.ctx_len // L, rs.seq // L
    rowblk = _scan_rowblock(b, nc_ctx, nc_lat)

    def specs(d):
        rb = lambda b_, c: (rowblk(b_, d, c), 0)
        return [pl.BlockSpec((L, dqk_all), rb), pl.BlockSpec((L, dqk_all), rb), pl.BlockSpec((L, dv_all), rb),
                pl.BlockSpec((L, LANES), rb)]

    out_shape = jax.ShapeDtypeStruct((rs.rows, dv_all), F32)
    dqk, dv = dqk_all // M_HEADS, dv_all // M_HEADS
    return pl.pallas_call(
        _mlstm_scan_kernel, grid=(b, nc_ctx + nc_lat),
        in_specs=specs(0) + specs(1),
        out_specs=[pl.BlockSpec((L, dv_all), lambda b_, c: (rowblk(b_, 0, c), 0)),
                   pl.BlockSpec((L, dv_all), lambda b_, c: (rowblk(b_, 1, c), 0))],
        out_shape=[out_shape, out_shape],
        scratch_shapes=[pltpu.VMEM((2, M_HEADS, dqk, dv), F32), pltpu.VMEM((2, M_HEADS, 1, dqk), F32),
                        pltpu.VMEM((2, M_HEADS, 1, 1), F32)],
        compiler_params=pltpu.CompilerParams(dimension_semantics=("arbitrary",) * 2,
                                             vmem_limit_bytes=_vmem_limit(40 * 2**20)),
        name="mlstm_scan",
    )(q, k, v, gates, q, k, v, gates)


def _mlstm_post_kernel(x_ref, s0_ref, s1_ref, mod_ref, wog_ref, ng_ref, wout_ref, g_ref, b_ref, o_ref, *, alpha):
    d = x_ref.shape[1]
    chunks = [slice(r0, r0 + HEAD_ROWS) for r0 in range(0, TM, HEAD_ROWS)]
    h = [_mod(x_ref[r, :], mod_ref, 0, d).astype(MXU) for r in chunks]
    og = [_sigmoid(_dot(hc, wog_ref[...])) for hc in h]
    a = [_head_rms(og[i] * (s0_ref[r, :] + s1_ref[r, :]), ng_ref[...], M_HEADS).astype(MXU) for i, r in enumerate(chunks)]
    _residual_ln(x_ref, mod_ref, jnp.concatenate(a, axis=0), wout_ref, g_ref, b_ref, o_ref, alpha)


def _mlstm_post(rs, n_tiles, x, s, mod, w_og, norm_g, w_out, ln_g, ln_b, alpha):
    d, inner = w_og.shape
    body = functools.partial(_mlstm_post_kernel, alpha=alpha)
    vmem = 2 * (w_og.size + w_out.size) + 10 * TM * inner * 4 + 16 * 2**20
    return _row_call(
        body, rs, n_tiles,
        [rs.tile(d), rs.tile(inner), rs.tile(inner), rs.mod(6 * d), _const_spec(w_og.shape), _const_spec((1, inner)),
         _const_spec(w_out.shape), _const_spec((1, d)), _const_spec((1, d))],
        rs.tile(d), jax.ShapeDtypeStruct((n_tiles * TM, d), F32), vmem=vmem, name="mlstm_post",
    )(x, s[0], s[1], mod, w_og, norm_g.reshape(1, inner), w_out, ln_g.reshape(1, d), ln_b.reshape(1, d))


def _mla_pre_kernel(x_ref, mod_ref, cos_ref, sin_ref, wdq_ref, qn_ref, wuq_ref, wdkv_ref, kvn_ref, wk_ref, wv_ref,
                    q_ref, k_ref, v_ref, *, q_scale):
    d = x_ref.shape[1]
    hd = A_DNOPE + LANES
    h = _mod(x_ref[...], mod_ref, 0, d).astype(MXU)
    cos, sin = cos_ref[...], sin_ref[...]
    cq = _rms(_dot(h, wdq_ref[...]), qn_ref[...]).astype(MXU)
    qa = _dot(cq, wuq_ref[...])
    part0 = A_HEADS * hd
    cos_q, sin_q = cos * q_scale, sin * q_scale
    for hh in range(A_HEADS):
        q_ref[:, hh * hd:hh * hd + A_DNOPE] = (qa[:, hh * hd:hh * hd + A_DNOPE] * q_scale).astype(q_ref.dtype)
        rot = qa[:, hh * hd + A_DNOPE:(hh + 1) * hd] * cos_q + qa[:, part0 + hh * LANES:part0 + (hh + 1) * LANES] * sin_q
        q_ref[:, hh * hd + A_DNOPE:(hh + 1) * hd] = rot.astype(q_ref.dtype)
    dk = _dot(h, wdkv_ref[...])
    k_rope = (dk[:, A_KVLORA:A_KVLORA + LANES] * cos + dk[:, A_KVLORA + LANES:] * sin).astype(k_ref.dtype)
    ckv = _rms(dk[:, :A_KVLORA], kvn_ref[...]).astype(MXU)
    k_nope = _dot(ckv, wk_ref[...])
    for hh in range(A_HEADS):
        k_ref[:, hh * hd:hh * hd + A_DNOPE] = k_nope[:, hh * A_DNOPE:(hh + 1) * A_DNOPE].astype(k_ref.dtype)
        k_ref[:, hh * hd + A_DNOPE:(hh + 1) * hd] = k_rope
    v_ref[...] = _dot(ckv, wv_ref[...]).astype(v_ref.dtype)


def _rope_tables(seq):
    n_freq = A_DROPE // 4
    inv_freq = ROPE_BASE ** (-jnp.arange(n_freq, dtype=F32) / n_freq)
    pos = jnp.arange(seq)
    ang_row = (pos // GRID_W).astype(F32)[:, None] * inv_freq
    ang_col = (pos % GRID_W).astype(F32)[:, None] * inv_freq
    cos = jnp.concatenate([jnp.cos(ang_row)] * 2 + [jnp.cos(ang_col)] * 2, axis=1)
    sin = jnp.concatenate([-jnp.sin(ang_row), jnp.sin(ang_row), -jnp.sin(ang_col), jnp.sin(ang_col)], axis=1)
    pad = LANES - A_DROPE
    cos = jnp.pad(cos, ((0, 0), (0, pad)), constant_values=1.0)
    sin = jnp.pad(sin, ((0, 0), (0, pad)))
    ident = (jnp.ones((TM, LANES), F32), jnp.zeros((TM, LANES), F32))
    return jnp.concatenate([ident[0], cos]), jnp.concatenate([ident[1], sin])


def _rope_partner_cols(w):
    idx = np.arange(A_DROPE)
    half = A_DROPE // 4
    partner = np.where((idx % (2 * half)) < half, idx + half, idx - half)
    return w[..., partner]


def _mla_pre(rs, x, mod, w_dq, q_norm, w_uq, w_dkv, kv_norm, w_ukv):
    d, qlora = w_dq.shape
    hd = A_DNOPE + LANES
    pad = LANES - A_DROPE
    wq = w_uq.reshape(qlora, A_HEADS, A_DNOPE + A_DROPE)
    wq_main = jnp.pad(wq, ((0, 0), (0, 0), (0, pad))).reshape(qlora, A_HEADS * hd)
    wq_part = jnp.pad(_rope_partner_cols(wq[..., A_DNOPE:]), ((0, 0), (0, 0), (0, pad))).reshape(qlora, A_HEADS * LANES)
    wuq_all = jnp.concatenate([wq_main, wq_part], axis=1).astype(MXU)
    w_kr = w_dkv[:, A_KVLORA:]
    wdkv_all = jnp.concatenate([w_dkv[:, :A_KVLORA], jnp.pad(w_kr, ((0, 0), (0, pad))),
                                jnp.pad(_rope_partner_cols(w_kr), ((0, 0), (0, pad)))], axis=1).astype(MXU)
    wkv = w_ukv.reshape(A_KVLORA, A_HEADS, A_DNOPE + A_DV)
    w_k = wkv[..., :A_DNOPE].reshape(A_KVLORA, A_HEADS * A_DNOPE).astype(MXU)
    w_v = wkv[..., A_DNOPE:].reshape(A_KVLORA, A_HEADS * A_DV).astype(MXU)
    cos, sin = _rope_tables(rs.seq)
    nl, tps = rs.n_lat_tiles, rs.tiles_per_seq
    tab = pl.BlockSpec((TM, LANES), lambda i: (jnp.where(i >= nl, 0, 1 + i % tps), 0))
    n, r = rs.n_tiles, rs.rows
    vmem = 2 * 2 * (w_dq.size + wuq_all.size + wdkv_all.size + w_k.size + w_v.size) + 12 * TM * A_HEADS * hd * 4
    return _row_call(
        functools.partial(_mla_pre_kernel, q_scale=float((A_DNOPE + A_DROPE) ** -0.5 * np.log2(np.e))), rs, n,
        [rs.tile(d), rs.mod(6 * d), tab, tab, _const_spec(w_dq.shape), _const_spec((1, qlora)), _const_spec(wuq_all.shape),
         _const_spec(wdkv_all.shape), _const_spec((1, A_KVLORA)), _const_spec(w_k.shape), _const_spec(w_v.shape)],
        [rs.tile(A_HEADS * hd), rs.tile(A_HEADS * hd), rs.tile(A_HEADS * A_DV)],
        [jax.ShapeDtypeStruct((r, A_HEADS * hd), MXU), jax.ShapeDtypeStruct((r, A_HEADS * hd), MXU),
         jax.ShapeDtypeStruct((r, A_HEADS * A_DV), MXU)],
        vmem=vmem, name="mla_pre",
    )(x, mod, cos, sin, w_dq.astype(MXU), q_norm.reshape(1, qlora), wuq_all, wdkv_all, kv_norm.reshape(1, A_KVLORA), w_k, w_v)


def _attn_kernel(*refs, with_latent, kv_chunk):
    if with_latent:
        q_ref, kc_ref, vc_ref, kl_ref, vl_ref, o_ref = refs
    else:
        q_ref, kc_ref, vc_ref, o_ref = refs
    hd, dv = A_DNOPE + LANES, A_DV
    heads = range(q_ref.shape[1] // hd)
    qs = lambda h: slice(h * hd, (h + 1) * hd)
    vs = lambda h: slice(h * dv, (h + 1) * dv)
    q = [q_ref[:, qs(h)] for h in heads]
    s = [_dot_nt(q[h], kc_ref[:, qs(h)]) for h in heads]
    m = [jnp.max(s[h], axis=1, keepdims=True) for h in heads]
    p = [jnp.exp2(s[h] - m[h]) for h in heads]
    l = [jnp.sum(p[h], axis=1, keepdims=True) for h in heads]
    acc = [_dot(p[h].astype(MXU), vc_ref[:, vs(h)]) for h in heads]
    if with_latent:
        for c0 in range(0, kl_ref.shape[0], kv_chunk):
            s = [_dot_nt(q[h], kl_ref[c0:c0 + kv_chunk, qs(h)]) for h in heads]
            m_new = [jnp.maximum(m[h], jnp.max(s[h], axis=1, keepdims=True)) for h in heads]
            corr = [jnp.exp2(m[h] - m_new[h]) for h in heads]
            p = [jnp.exp2(s[h] - m_new[h]) for h in heads]
            l = [l[h] * corr[h] + jnp.sum(p[h], axis=1, keepdims=True) for h in heads]
            acc = [acc[h] * corr[h] + _dot(p[h].astype(MXU), vl_ref[c0:c0 + kv_chunk, vs(h)]) for h in heads]
            m = m_new
    for h in heads:
        o_ref[:, vs(h)] = (acc[h] / l[h]).astype(o_ref.dtype)


def _mla_attention(rs, q, k, v):
    b, t = rs.batch, rs.seq
    hpb = 2
    hd, dv = hpb * (A_DNOPE + LANES), hpb * A_DV
    kv_chunk = min(1024, t)
    assert t % kv_chunk == 0 and A_HEADS % hpb == 0
    tc = rs.ctx_len
    ctx_blk0 = b * t // tc
    params = pltpu.CompilerParams(dimension_semantics=("arbitrary",) * 3, vmem_limit_bytes=_vmem_limit(48 * 2**20))
    o_ctx = pl.pallas_call(
        functools.partial(_attn_kernel, with_latent=False, kv_chunk=kv_chunk), grid=(b, A_HEADS // hpb, 1),
        in_specs=[pl.BlockSpec((tc, hd), lambda b_, h, i: (ctx_blk0 + b_, h)),
                  pl.BlockSpec((tc, hd), lambda b_, h, i: (ctx_blk0 + b_, h)),
                  pl.BlockSpec((tc, dv), lambda b_, h, i: (ctx_blk0 + b_, h))],
        out_specs=pl.BlockSpec((tc, dv), lambda b_, h, i: (b_, h)),
        out_shape=jax.ShapeDtypeStruct((b * tc, A_HEADS * A_DV), MXU), compiler_params=params, name="mla_attn_ctx",
    )(q, k, v)
    tq = ATTN_TQ
    nq = t // tq
    o_lat = pl.pallas_call(
        functools.partial(_attn_kernel, with_latent=True, kv_chunk=kv_chunk), grid=(b, A_HEADS // hpb, nq),
        in_specs=[pl.BlockSpec((tq, hd), lambda b_, h, i: (b_ * nq + i, h)),
                  pl.BlockSpec((tc, hd), lambda b_, h, i: (ctx_blk0 + b_, h)),
                  pl.BlockSpec((tc, dv), lambda b_, h, i: (ctx_blk0 + b_, h)),
                  pl.BlockSpec((t, hd), lambda b_, h, i: (b_, h)),
                  pl.BlockSpec((t, dv), lambda b_, h, i: (b_, h))],
        out_specs=pl.BlockSpec((tq, dv), lambda b_, h, i: (b_ * nq + i, h)),
        out_shape=jax.ShapeDtypeStruct((b * t, A_HEADS * A_DV), MXU), compiler_params=params, name="mla_attn_latent",
    )(q, k, v, k, v)
    return o_lat, o_ctx


def _proj_post_kernel(x_ref, al_ref, ac_ref, mod_ref, wout_ref, g_ref, b_ref, o_ref, *, alpha, n_lat_tiles):
    a = jnp.where(pl.program_id(0) >= n_lat_tiles, ac_ref[...], al_ref[...])
    _residual_ln(x_ref, mod_ref, a, wout_ref, g_ref, b_ref, o_ref, alpha)


def _proj_post(rs, n_tiles, x, a_lat, a_ctx, mod, w_out, ln_g, ln_b, alpha):
    kdim, d = w_out.shape
    nl = rs.n_lat_tiles
    body = functools.partial(_proj_post_kernel, alpha=alpha, n_lat_tiles=nl)
    return _row_call(
        body, rs, n_tiles,
        [rs.tile(d), pl.BlockSpec((TM, kdim), lambda i: (jnp.minimum(i, nl - 1), 0)),
         pl.BlockSpec((TM, kdim), lambda i: (jnp.maximum(i - nl, 0), 0)), rs.mod(6 * d), _const_spec(w_out.shape),
         _const_spec((1, d)), _const_spec((1, d))],
        rs.tile(d), jax.ShapeDtypeStruct((n_tiles * TM, d), F32), vmem=32 * 2**20, name="proj_post",
    )(x, a_lat, a_ctx, mod, w_out, ln_g.reshape(1, d), ln_b.reshape(1, d))


def _gla_pre_kernel(x_ref, mod_ref, wqk_ref, wv_ref, wa1_ref, wa2_ref, ba_ref, q_ref, k_ref, v_ref, la_ref,
                    *, q_scale):
    d = x_ref.shape[1]
    half = q_ref.shape[1]
    chunks = [slice(r0, r0 + HEAD_ROWS) for r0 in range(0, TM, HEAD_ROWS)]
    h = [_mod(x_ref[r, :], mod_ref, 0, d).astype(MXU) for r in chunks]
    a1 = [_dot(hc, wa1_ref[...]).astype(MXU) for hc in h]
    z = [_dot(a, wa2_ref[...]) + ba_ref[...] for a in a1]
    qk = [_dot(hc, wqk_ref[...]) for hc in h]
    for i, r in enumerate(chunks):
        la_ref[r, :] = _log_sigmoid(z[i]) * (1.0 / G_TAU)
        q_ref[r, :] = qk[i][:, :half] * q_scale
        k_ref[r, :] = qk[i][:, half:]
    for i, r in enumerate(chunks):
        v_ref[r, :] = _dot(h[i], wv_ref[...]).astype(v_ref.dtype)


def _gla_pre(rs, x, mod, w_qk, w_v, w_a1, w_a2, b_a):
    d = w_qk.shape[0]
    dk_all = w_qk.shape[1] // 2
    wa1 = jnp.pad(jnp.concatenate([w_a1[0], w_a1[1]], axis=1), ((0, 0), (0, LANES - 2 * G_RANK))).astype(MXU)
    wa2 = jnp.zeros((LANES, 2 * dk_all), F32)
    wa2 = wa2.at[:G_RANK, :dk_all].set(w_a2[0]).at[G_RANK:2 * G_RANK, dk_all:].set(w_a2[1]).astype(MXU)
    ba = jnp.concatenate([b_a[0], b_a[1]]).reshape(1, 2 * dk_all)
    body = functools.partial(_gla_pre_kernel, q_scale=float((dk_all // G_HEADS) ** -0.5))
    n, r = rs.n_tiles, rs.rows
    return _row_call(
        body, rs, n,
        [rs.tile(d), rs.mod(6 * d), _const_spec(w_qk.shape), _const_spec(w_v.shape), _const_spec(wa1.shape),
         _const_spec(wa2.shape), _const_spec(ba.shape)],
        [rs.tile(dk_all), rs.tile(dk_all), rs.tile(w_v.shape[1]), rs.tile(2 * dk_all)],
        [jax.ShapeDtypeStruct((r, dk_all), F32), jax.ShapeDtypeStruct((r, dk_all), F32),
         jax.ShapeDtypeStruct((r, w_v.shape[1]), MXU), jax.ShapeDtypeStruct((r, 2 * dk_all), F32)],
        vmem=40 * 2**20, name="gla_pre",
    )(x, mod, w_qk, w_v, wa1, wa2, ba)


def _gla_tables(L):
    t = np.arange(L)
    tau, taup = t[:, None], t[None, :]
    groups = [taup <= tau, taup > tau]
    masks = []
    c = L // 2
    while c >= 1:
        blk = t // (2 * c)
        mid = blk * 2 * c + c
        second = (t % (2 * c)) >= c
        q_side = second[:, None] & (taup >= mid[:, None]) & (taup <= tau)
        k_side = (~second)[:, None] & (taup > tau) & (taup <= mid[:, None] - 1)
        groups.append(q_side | k_side)
        masks.append(second[:, None] & (~second)[None, :] & (blk[:, None] == blk[None, :]))
        c //= 2
    flip = lambda g: g[::-1, ::-1]
    sums = np.stack([np.concatenate(groups, axis=0), np.concatenate([flip(g) for g in groups], axis=0)])
    lvl = np.stack([np.stack(masks), np.stack([flip(m) for m in masks])])
    return sums.astype(np.float32), lvl.astype(np.float32)


def _gla_scan_kernel(q0_ref, k0_ref, v0_ref, la0_ref, q1_ref, k1_ref, v1_ref, la1_ref, sums_ref, masks_ref,
                     o0_ref, o1_ref, st_scr):
    @pl.when(pl.program_id(1) == 0)
    def _():
        st_scr[...] = jnp.zeros_like(st_scr)

    L = q0_ref.shape[0]
    dk, dv = q0_ref.shape[1] // G_HEADS, v0_ref.shape[1] // G_HEADS
    levels = masks_ref.shape[1]
    dirs = ((q0_ref, k0_ref, v0_ref, la0_ref, o0_ref), (q1_ref, k1_ref, v1_ref, la1_ref, o1_ref))
    chains = [(d, h) for d in range(2) for h in range(G_HEADS)]
    row = lax.broadcasted_iota(jnp.int32, (L, L), 0)
    col = lax.broadcasted_iota(jnp.int32, (L, L), 1)
    eye = row == col

    st_prev = {(d, h): st_scr[d, h] for d, h in chains}
    e = [jnp.exp(_dot(sums_ref[d], jnp.concatenate(_split2(dirs[d][3][...]), axis=0))) for d in range(2)]
    q = [dirs[d][0][...] for d in range(2)]
    k = [dirs[d][1][...] for d in range(2)]
    qe = [(q[d] * e[d][0:L]).astype(MXU) for d in range(2)]
    kd = [(k[d] * e[d][L:2 * L]).astype(MXU) for d in range(2)]
    qk_diag = [q[d] * k[d] for d in range(2)]
    qt = [[(q[d] * e[d][(2 + lv) * L:(3 + lv) * L]).astype(MXU) for lv in range(levels)] for d in range(2)]
    kt = [[(k[d] * e[d][(2 + lv) * L:(3 + lv) * L]).astype(MXU) for lv in range(levels)] for d in range(2)]
    decay = [jnp.exp(jnp.sum(dirs[d][3][...], axis=0, keepdims=True)) for d in range(2)]

    att = {}
    for d, h in chains:
        ks = slice(h * dk, (h + 1) * dk)
        a = jnp.where(eye, jnp.sum(qk_diag[d][:, ks], axis=1, keepdims=True), 0.0)
        for lv in range(levels):
            a = a + masks_ref[d, lv] * _dot_nt(qt[d][lv][:, ks], kt[d][lv][:, ks])
        att[d, h] = a.astype(MXU)
    for d, h in chains:
        ks, vs = slice(h * dk, (h + 1) * dk), slice(h * dv, (h + 1) * dv)
        v = dirs[d][2][:, vs]
        dirs[d][4][:, vs] = _dot_nt(qe[d][:, ks], st_prev[d, h].astype(MXU)) + _dot(att[d, h], v)
        st_scr[d, h] = st_prev[d, h] * decay[d][:, ks] + _dot_tn(v, kd[d][:, ks])


def _gla_scan(rs, q, k, v, la):
    b, L = rs.batch, G_CHUNK
    dk_all, dv_all = q.shape[1], v.shape[1]
    nc_ctx, nc_lat = rs.ctx_len // L, rs.seq // L
    rowblk = _scan_rowblock(b, nc_ctx, nc_lat)
    sums, masks = _gla_tables(L)
    sums, masks = jnp.asarray(np.concatenate([sums, sums], axis=2), MXU), jnp.asarray(masks, F32)

    def specs(d):
        rb = lambda b_, c: (rowblk(b_, d, c), 0)
        return [pl.BlockSpec((L, dk_all), rb), pl.BlockSpec((L, dk_all), rb), pl.BlockSpec((L, dv_all), rb),
                pl.BlockSpec((L, dk_all), lambda b_, c: (rowblk(b_, d, c), d))]

    out_shape = jax.ShapeDtypeStruct((rs.rows, dv_all), F32)
    return pl.pallas_call(
        _gla_scan_kernel, grid=(b, nc_ctx + nc_lat),
        in_specs=specs(0) + specs(1) + [_const_spec(sums.shape), _const_spec(masks.shape)],
        out_specs=[pl.BlockSpec((L, dv_all), lambda b_, c: (rowblk(b_, 0, c), 0)),
                   pl.BlockSpec((L, dv_all), lambda b_, c: (rowblk(b_, 1, c), 0))],
        out_shape=[out_shape, out_shape],
        scratch_shapes=[pltpu.VMEM((2, G_HEADS, dv_all // G_HEADS, dk_all // G_HEADS), F32)],
        compiler_params=pltpu.CompilerParams(dimension_semantics=("arbitrary",) * 2,
                                             vmem_limit_bytes=_vmem_limit(32 * 2**20)),
        name="gla_scan",
    )(q, k, v, la, q, k, v, la, sums, masks)


def _gla_post_kernel(x_ref, s0_ref, s1_ref, mod_ref, wr_ref, ng_ref, wout_ref, g_ref, b_ref, o_ref, *, alpha):
    d = x_ref.shape[1]
    chunks = [slice(r0, r0 + HEAD_ROWS) for r0 in range(0, TM, HEAD_ROWS)]
    h = [_mod(x_ref[r, :], mod_ref, 0, d).astype(MXU) for r in chunks]
    gate = [_silu(_dot(hc, wr_ref[...])) for hc in h]
    a = [(_head_rms(s0_ref[r, :] + s1_ref[r, :], ng_ref[...], G_HEADS) * gate[i]).astype(MXU) for i, r in enumerate(chunks)]
    _residual_ln(x_ref, mod_ref, jnp.concatenate(a, axis=0), wout_ref, g_ref, b_ref, o_ref, alpha)


def _gla_post(rs, n_tiles, x, s, mod, w_r, norm_g, w_out, ln_g, ln_b, alpha):
    d, dv_all = w_r.shape
    body = functools.partial(_gla_post_kernel, alpha=alpha)
    return _row_call(
        body, rs, n_tiles,
        [rs.tile(d), rs.tile(dv_all), rs.tile(dv_all), rs.mod(6 * d), _const_spec(w_r.shape), _const_spec((1, dv_all)), _const_spec(w_out.shape),
         _const_spec((1, d)), _const_spec((1, d))],
        rs.tile(d), jax.ShapeDtypeStruct((n_tiles * TM, d), F32), vmem=40 * 2**20, name="gla_post",
    )(x, s[0], s[1], mod, w_r, norm_g.reshape(1, dv_all), w_out, ln_g.reshape(1, d), ln_b.reshape(1, d))


def kernel(x, c, ctx, c_ctx, ada_w, ada_b, ln_g, ln_b, ffn_w_in, ffn_conv_w, ffn_conv_b, ffn_w_out, m_w_up, m_conv_w, m_conv_b, m_w_qk, m_w_v, m_w_gates, m_b_gates, m_w_og, m_norm_g, m_w_out, a_w_dq, a_q_norm, a_w_uq, a_w_dkv, a_kv_norm, a_w_ukv, a_w_out, g_w_qk, g_w_v, g_w_r, g_w_a1, g_w_a2, g_b_a, g_norm_g, g_w_out):
    batch, seq, d = x.shape
    depth = ada_w.shape[0]
    n_mixers = 3
    alpha = float((2 * depth) ** 0.25)
    rs = _Rows(batch, seq, ctx.shape[1], TM)
    rs_ffn = _Rows(batch, seq, ctx.shape[1], FFN_TM)
    bf = lambda w: w.astype(MXU)

    cond_rows = -(-(batch + 1) // SUBLANES) * SUBLANES
    cond = jnp.zeros((cond_rows, d), F32).at[:batch].set(c).at[batch].set(c_ctx)
    mods = _modulation(cond, ada_w, ada_b).reshape(depth, cond_rows, 1, 6 * d)

    xa = jnp.concatenate([x.reshape(batch * seq, d), ctx.reshape(-1, d)], axis=0)
    for i in range(depth):
        need_ctx = i < depth - 1
        n_tiles = rs.n_tiles if need_ctx else rs.n_lat_tiles
        kind, j = i % n_mixers, i // n_mixers
        mod = mods[i]
        if kind == 0:
            q, k, v, gates = _mlstm_pre(rs, xa, mod, bf(m_w_up[j]), m_conv_w[j], m_conv_b[j], bf(m_w_qk[j]),
                                        _weight_product(m_w_up[j], m_w_v[j]), m_w_gates[j], m_b_gates[j])
            s = _mlstm_scan(rs, q, k, v, gates)
            xa = _mlstm_post(rs, n_tiles, xa, s, mod, bf(m_w_og[j]), m_norm_g[j], bf(m_w_out[j]), ln_g[i, 0], ln_b[i, 0], alpha)
        elif kind == 1:
            q, k, v = _mla_pre(rs, xa, mod, a_w_dq[j], a_q_norm[j], a_w_uq[j], a_w_dkv[j], a_kv_norm[j], a_w_ukv[j])
            o_lat, o_ctx = _mla_attention(rs, q, k, v)
            xa = _proj_post(rs, n_tiles, xa, o_lat, o_ctx, mod, bf(a_w_out[j]), ln_g[i, 0], ln_b[i, 0], alpha)
        else:
            q, k, v, la = _gla_pre(rs, xa, mod, bf(g_w_qk[j]), bf(g_w_v[j]), g_w_a1[j], g_w_a2[j], g_b_a[j])
            s = _gla_scan(rs, q, k, v, la)
            xa = _gla_post(rs, n_tiles, xa, s, mod, bf(g_w_r[j]), g_norm_g[j], bf(g_w_out[j]), ln_g[i, 0], ln_b[i, 0], alpha)
        xa = _ffn(rs_ffn, rs_ffn.n_tiles if need_ctx else rs_ffn.n_lat_tiles, xa, mod, bf(ffn_w_in[i]), ffn_conv_w[i],
                  ffn_conv_b[i], bf(ffn_w_out[i]), ln_g[i, 1], ln_b[i, 1], alpha)
    return xa.reshape(batch, seq, d)
```

```python
import functools

import numpy as np
import jax
import jax.numpy as jnp
from jax import lax
from jax.experimental import pallas as pl
from jax.experimental.pallas import tpu as pltpu

F32 = jnp.float32
MXU = jnp.bfloat16

V7X_VMEM_BYTES = 64 * 2**20
SUBLANES = 8
LANES = 128

TM = 512
FFN_TM = 1024
ATTN_TQ = 1024
HEAD_ROWS = 256
TAIL_ROWS = 128
HALO = SUBLANES
GRID_W = 64
EPS = 1e-6
ROPE_BASE = 10000.0
G_TAU = 16.0
LOG2E = float(np.log2(np.e))

M_HEADS, A_HEADS, G_HEADS = 4, 8, 4
A_DNOPE, A_DROPE, A_DV, A_KVLORA = 128, 64, 128, 256
G_RANK = 16
M_CHUNK = 256
M_SCAN_GROUP = 8
G_CHUNK = 128


def _vmem_limit(nbytes):
    return int(min(max(nbytes, 16 * 2**20), V7X_VMEM_BYTES - 8 * 2**20))


def _const_spec(shape):
    nd = len(shape)
    return pl.BlockSpec(shape, lambda *_: (0,) * nd, pipeline_mode=pl.Buffered(1))


def _dot(a, b):
    return jnp.dot(a, b, preferred_element_type=F32)


def _dot_nt(a, b):
    return lax.dot_general(a, b, (((1,), (1,)), ((), ())), preferred_element_type=F32)


def _dot_tn(a, b):
    return lax.dot_general(a, b, (((0,), (0,)), ((), ())), preferred_element_type=F32)


def _split3(x):
    hi = x.astype(MXU)
    r1 = x - hi.astype(F32)
    mid = r1.astype(MXU)
    lo = (r1 - mid.astype(F32)).astype(MXU)
    return hi, mid, lo


def _split2(x):
    hi = x.astype(MXU)
    return hi, (x - hi.astype(F32)).astype(MXU)


def _sigmoid(x):
    return 1.0 / (1.0 + jnp.exp(-x))


def _silu(x):
    return x * _sigmoid(x)


def _log_sigmoid(x):
    return jnp.minimum(x, 0.0) - jnp.log1p(jnp.exp(-jnp.abs(x)))


def _layer_norm(z, g, b):
    mu = jnp.mean(z, -1, keepdims=True)
    zc = z - mu
    var = jnp.mean(zc * zc, -1, keepdims=True)
    return zc * lax.rsqrt(var + EPS) * g + b


def _rms(x, g):
    return x * lax.rsqrt(jnp.mean(x * x, -1, keepdims=True) + EPS) * g


def _head_rms(x, g, heads):
    d = x.shape[-1] // heads
    return jnp.concatenate([_rms(x[:, h * d:(h + 1) * d], g[:, h * d:(h + 1) * d]) for h in range(heads)], axis=-1)


def _mod(x, mod_ref, k, d):
    return x * (1.0 + mod_ref[:, (k + 1) * d:(k + 2) * d]) + mod_ref[:, k * d:(k + 1) * d]


def _halo_rows(xp_ref, x_ref, xn_ref, mod_ref, k, d, geom):
    n_lat_tiles, tiles_per_seq, _, _ = geom
    i = pl.program_id(0)
    is_ctx = i >= n_lat_tiles
    pos = i % tiles_per_seq
    first = jnp.logical_or(is_ctx, pos == 0)
    last = jnp.logical_or(is_ctx, pos == tiles_per_seq - 1)
    hp = jnp.where(first, 0.0, _mod(xp_ref[...], mod_ref, k, d))
    hn = jnp.where(last, 0.0, _mod(xn_ref[...], mod_ref, k, d))
    return jnp.concatenate([hp, _mod(x_ref[...], mod_ref, k, d), hn], axis=0)


def _seq_edges(geom):
    n_lat_tiles, _, ctx_len, tm = geom
    is_ctx = pl.program_id(0) >= n_lat_tiles
    r = lax.broadcasted_iota(jnp.int32, (tm, 1), 0) % ctx_len
    return jnp.logical_and(is_ctx, r == 0), jnp.logical_and(is_ctx, r == ctx_len - 1)


def _dwconv3(g_ext, w_ref, b_ref, c0, c1, edges):
    n = g_ext.shape[0] - 2 * HALO
    starts, ends = edges
    prev = jnp.where(starts, 0.0, g_ext[HALO - 1:HALO - 1 + n])
    nxt = jnp.where(ends, 0.0, g_ext[HALO + 1:HALO + 1 + n])
    return (w_ref[0:1, c0:c1] * prev + w_ref[1:2, c0:c1] * g_ext[HALO:HALO + n] + w_ref[2:3, c0:c1] * nxt
            + b_ref[:, c0:c1])


class _Rows:
    def __init__(self, batch, seq, ctx_len, tm):
        assert tm % ctx_len == 0 and (batch * ctx_len) % tm == 0 and seq % tm == 0
        self.batch, self.seq, self.ctx_len, self.tm = batch, seq, ctx_len, tm
        self.tiles_per_seq = seq // tm
        self.n_lat_tiles = batch * self.tiles_per_seq
        self.n_tiles = self.n_lat_tiles + batch * ctx_len // tm
        self.rows = self.n_tiles * tm
        self.geom = (self.n_lat_tiles, self.tiles_per_seq, ctx_len, tm)

    def tile(self, width):
        return pl.BlockSpec((self.tm, width), lambda i: (i, 0))

    def halo_prev(self, width):
        per = self.tm // HALO
        return pl.BlockSpec((HALO, width), lambda i: (jnp.maximum(i * per - 1, 0), 0))

    def halo_next(self, width, n_rows):
        per, nblk = self.tm // HALO, n_rows // HALO
        return pl.BlockSpec((HALO, width), lambda i: (jnp.minimum((i + 1) * per, nblk - 1), 0))

    def mod(self, width):
        nl, tps, b = self.n_lat_tiles, self.tiles_per_seq, self.batch
        return pl.BlockSpec((None, 1, width), lambda i: (jnp.where(i >= nl, b, i // tps), 0, 0))


def _row_call(body, rows, n_tiles, in_specs, out_specs, out_shape, scratch=(), vmem=0, name=None):
    return pl.pallas_call(
        body, grid=(n_tiles,), in_specs=in_specs, out_specs=out_specs, out_shape=out_shape,
        scratch_shapes=list(scratch), name=name,
        compiler_params=pltpu.CompilerParams(dimension_semantics=("arbitrary",), vmem_limit_bytes=_vmem_limit(vmem)))


def _modulation_kernel(c_ref, w_ref, b_ref, o_ref):
    o_ref[...] = _dot(_silu(c_ref[...]).astype(MXU), w_ref[...].astype(MXU)) + b_ref[...]


def _modulation(cond, ada_w, ada_b):
    depth, d, n = ada_w.shape
    tn = d
    return pl.pallas_call(
        _modulation_kernel, grid=(depth, n // tn),
        in_specs=[pl.BlockSpec(cond.shape, lambda l, j: (0, 0)),
                  pl.BlockSpec((None, d, tn), lambda l, j: (l, 0, j)),
                  pl.BlockSpec((None, 1, tn), lambda l, j: (l, 0, j))],
        out_specs=pl.BlockSpec((None, cond.shape[0], tn), lambda l, j: (l, 0, j)),
        out_shape=jax.ShapeDtypeStruct((depth, cond.shape[0], n), F32), name="modulation",
    )(cond, ada_w, ada_b.reshape(depth, 1, n))


def _ffn_kernel(xp_ref, x_ref, xn_ref, mod_ref, win_ref, cw_ref, cb_ref, wout_ref, g_ref, b_ref, o_ref, a_scr,
                *, geom, alpha, ffn, tf):
    tm, d = x_ref.shape
    hext = _halo_rows(xp_ref, x_ref, xn_ref, mod_ref, 3, d, geom).astype(MXU)
    edges = _seq_edges(geom)
    for f0 in range(0, ffn, tf):
        g_ext = _dot(hext, win_ref[:, f0:f0 + tf])
        up = _dot(hext, win_ref[:, ffn + f0:ffn + f0 + tf])[HALO:HALO + tm]
        a_scr[:, f0:f0 + tf] = (_silu(_dwconv3(g_ext, cw_ref, cb_ref, f0, f0 + tf, edges)) * up).astype(MXU)
    for r0 in range(0, tm, TAIL_ROWS):
        rows = slice(r0, r0 + TAIL_ROWS)
        f = _dot(a_scr[rows, :], wout_ref[...])
        z = alpha * x_ref[rows, :] + mod_ref[:, 5 * d:6 * d] * f
        o_ref[rows, :] = _layer_norm(z, g_ref[...], b_ref[...])


def _ffn(rs, n_tiles, x, mod, w_in, conv_w, conv_b, w_out, ln_g, ln_b, alpha):
    d, ffn = w_out.shape[1], w_out.shape[0]
    tf = 256
    assert ffn % tf == 0
    body = functools.partial(_ffn_kernel, geom=rs.geom, alpha=alpha, ffn=ffn, tf=tf)
    vmem = 2 * (w_in.size + w_out.size) + 6 * rs.tm * d * 4 + rs.tm * ffn * 2 + 24 * 2**20
    return _row_call(
        body, rs, n_tiles,
        [rs.halo_prev(d), rs.tile(d), rs.halo_next(d, x.shape[0]), rs.mod(6 * d), _const_spec(w_in.shape), _const_spec(conv_w.shape),
         _const_spec((1, ffn)), _const_spec(w_out.shape), _const_spec((1, d)), _const_spec((1, d))],
        rs.tile(d), jax.ShapeDtypeStruct((n_tiles * rs.tm, d), F32),
        scratch=[pltpu.VMEM((rs.tm, ffn), MXU)], vmem=vmem, name="conv_ffn",
    )(x, x, x, mod, w_in, conv_w, conv_b.reshape(1, ffn), w_out, ln_g.reshape(1, d), ln_b.reshape(1, d))


def _residual_ln(x_ref, mod_ref, a, wout_ref, g_ref, b_ref, o_ref, alpha):
    d = x_ref.shape[1]
    for r0 in range(0, TM, TAIL_ROWS):
        rows = slice(r0, r0 + TAIL_ROWS)
        z = alpha * x_ref[rows, :] + mod_ref[:, 2 * d:3 * d] * _dot(a[rows, :], wout_ref[...])
        o_ref[rows, :] = _layer_norm(z, g_ref[...], b_ref[...])


def _weight_product_kernel(a_ref, b_ref, o_ref):
    o_ref[...] = _dot(a_ref[...].astype(MXU), b_ref[...].astype(MXU)).astype(o_ref.dtype)


def _weight_product(a, b):
    m, kdim = a.shape
    n = b.shape[1]
    tn = 512
    assert n % tn == 0
    return pl.pallas_call(
        _weight_product_kernel, grid=(n // tn,),
        in_specs=[pl.BlockSpec((m, kdim), lambda j: (0, 0)), pl.BlockSpec((kdim, tn), lambda j: (0, j))],
        out_specs=pl.BlockSpec((m, tn), lambda j: (0, j)),
        out_shape=jax.ShapeDtypeStruct((m, n), MXU),
        compiler_params=pltpu.CompilerParams(dimension_semantics=("arbitrary",), vmem_limit_bytes=_vmem_limit(40 * 2**20)),
        name="weight_product",
    )(a, b)


def _mlstm_pre_kernel(xp_ref, x_ref, xn_ref, mod_ref, wup_ref, cw_ref, cb_ref, wqk_ref, wv_ref, wg_ref, bg_ref,
                      q_ref, k_ref, v_ref, gates_ref, *, geom, k_scale):
    d = x_ref.shape[1]
    hext = _halo_rows(xp_ref, x_ref, xn_ref, mod_ref, 0, d, geom).astype(MXU)
    xm_ext = _dot(hext, wup_ref[...])
    inner = xm_ext.shape[1]
    xc = _silu(_dwconv3(xm_ext, cw_ref, cb_ref, 0, inner, _seq_edges(geom))).astype(MXU)
    qk = _dot(xc, wqk_ref[...])
    half = qk.shape[1] // 2
    q_ref[...] = qk[:, :half].astype(q_ref.dtype)
    k_ref[...] = (qk[:, half:] * k_scale).astype(k_ref.dtype)
    v_ref[...] = _dot(_mod(x_ref[...], mod_ref, 0, d).astype(MXU), wv_ref[...]).astype(v_ref.dtype)
    two_h = 2 * M_HEADS
    gates = _dot(xc, wg_ref[...]) + bg_ref[...]
    lane = lax.broadcasted_iota(jnp.int32, gates.shape, 1)
    gates_ref[...] = jnp.where(lane % two_h >= M_HEADS, _log_sigmoid(gates), gates) * LOG2E


def _mlstm_pre(rs, x, mod, w_up, conv_w, conv_b, w_qk, w_v, w_gates, b_gates):
    d, inner = w_up.shape
    dqk_all = w_qk.shape[1] // 2
    n_g = 2 * 2 * M_HEADS
    wg = jnp.concatenate([w_gates[0], w_gates[1]], axis=1)
    wg_pad = jnp.pad(wg, ((0, 0), (0, LANES - n_g))).astype(MXU)
    bg = jnp.concatenate([b_gates[0], b_gates[1]])
    bg_pad = jnp.pad(bg, (0, LANES - n_g)).reshape(1, LANES)
    body = functools.partial(_mlstm_pre_kernel, geom=rs.geom, k_scale=float((dqk_all // M_HEADS) ** -0.5))
    n, r = rs.n_tiles, rs.rows
    vmem = 2 * (w_up.size + w_qk.size + w_v.size) + 8 * (TM + 2 * HALO) * inner * 4 + 16 * 2**20
    return _row_call(
        body, rs, n,
        [rs.halo_prev(d), rs.tile(d), rs.halo_next(d, x.shape[0]), rs.mod(6 * d), _const_spec(w_up.shape), _const_spec(conv_w.shape),
         _const_spec((1, inner)), _const_spec(w_qk.shape), _const_spec(w_v.shape), _const_spec((inner, LANES)),
         _const_spec((1, LANES))],
        [rs.tile(dqk_all), rs.tile(dqk_all), rs.tile(inner), rs.tile(LANES)],
        [jax.ShapeDtypeStruct((r, dqk_all), MXU), jax.ShapeDtypeStruct((r, dqk_all), MXU),
         jax.ShapeDtypeStruct((r, inner), MXU), jax.ShapeDtypeStruct((r, LANES), F32)],
        vmem=vmem, name="mlstm_pre",
    )(x, x, x, mod, w_up, conv_w, conv_b.reshape(1, inner), w_qk, w_v, wg_pad, bg_pad)


def _scan_rowblock(batch, nc_ctx, nc_lat):
    def rowblk(b, d, c):
        cc = jnp.where(d == 1, nc_ctx - 1 - c, c)
        lc = c - nc_ctx
        lc = jnp.where(d == 1, nc_lat - 1 - lc, lc)
        return jnp.where(c < nc_ctx, batch * nc_lat + b * nc_ctx + cc, b * nc_lat + lc)
    return rowblk


def _mlstm_scan_kernel(q0_ref, k0_ref, v0_ref, g0_ref, q1_ref, k1_ref, v1_ref, g1_ref,
                       o0_ref, o1_ref, c_scr, n_scr, m_scr):
    @pl.when(pl.program_id(1) == 0)
    def _():
        c_scr[...] = jnp.zeros_like(c_scr)
        n_scr[...] = jnp.zeros_like(n_scr)
        m_scr[...] = jnp.zeros_like(m_scr)

    L = q0_ref.shape[0]
    dqk, dv = q0_ref.shape[1] // M_HEADS, v0_ref.shape[1] // M_HEADS
    two_h = 2 * M_HEADS
    dirs = ((q0_ref, k0_ref, v0_ref, g0_ref, o0_ref), (q1_ref, k1_ref, v1_ref, g1_ref, o1_ref))
    chains = [(d, h) for d in range(2) for h in range(M_HEADS)]
    row = lax.broadcasted_iota(jnp.int32, (L, L), 0)
    col = lax.broadcasted_iota(jnp.int32, (L, L), 1)
    causal = [col <= row, col >= row]
    causal_t = [row <= col, row >= col]
    eye = jnp.where(row == col, 1.0, 0.0).astype(MXU)
    gates = [dirs[d][3][...] for d in range(2)]
    pieces = [_split3(gates[d]) for d in range(2)]
    cs = [sum(_dot(jnp.where(causal[d], 1.0, 0.0).astype(MXU), p) for p in pieces[d]) for d in range(2)]
    cs_t = [sum(_dot_tn(p, jnp.where(causal_t[d], 1.0, 0.0).astype(MXU)) for p in pieces[d]) for d in range(2)]
    gates_t = [sum(_dot_tn(p, eye) for p in pieces[d]) for d in range(2)]

    i_slot = lambda d, h: d * two_h + h
    f_slot = lambda d, h: d * two_h + M_HEADS + h
    qs = lambda h: slice(h * dqk, (h + 1) * dqk)
    vs = lambda h: slice(h * dv, (h + 1) * dv)
    for g0 in range(0, len(chains), M_SCAN_GROUP):
        _mlstm_advance(chains[g0:g0 + M_SCAN_GROUP], dirs, causal, gates, gates_t, cs, cs_t, i_slot, f_slot, qs, vs,
                       c_scr, n_scr, m_scr)


def _mlstm_advance(chains, dirs, causal, gates, gates_t, cs, cs_t, i_slot, f_slot, qs, vs, c_scr, n_scr, m_scr):
    def per_chain(fn):
        return {ch: fn(*ch) for ch in chains}

    c_prev = per_chain(lambda d, h: c_scr[d, h])
    n_prev = per_chain(lambda d, h: n_scr[d, h])
    m_prev = per_chain(lambda d, h: m_scr[d, h])
    q = per_chain(lambda d, h: dirs[d][0][:, qs(h)])
    k = per_chain(lambda d, h: dirs[d][1][:, qs(h)])
    v = per_chain(lambda d, h: dirs[d][2][:, vs(h)])
    li_col = per_chain(lambda d, h: gates[d][:, i_slot(d, h):i_slot(d, h) + 1])
    li_row = per_chain(lambda d, h: gates_t[d][i_slot(d, h):i_slot(d, h) + 1, :])
    b_col = per_chain(lambda d, h: cs[d][:, f_slot(d, h):f_slot(d, h) + 1])
    b_row = per_chain(lambda d, h: cs_t[d][f_slot(d, h):f_slot(d, h) + 1, :])
    bl = per_chain(lambda d, h: jnp.sum(gates_t[d][f_slot(d, h):f_slot(d, h) + 1, :], axis=1, keepdims=True))

    dmat = per_chain(lambda d, h: jnp.where(causal[d], b_col[d, h] - b_row[d, h] + li_row[d, h], -jnp.inf))
    inter = per_chain(lambda d, h: b_col[d, h] + m_prev[d, h])
    mj = per_chain(lambda d, h: jnp.maximum(inter[d, h], jnp.max(dmat[d, h], axis=1, keepdims=True)))
    qk = per_chain(lambda d, h: _dot_nt(q[d, h], k[d, h]))
    wmat = per_chain(lambda d, h: jnp.exp2(dmat[d, h] - mj[d, h]) * qk[d, h])
    g = per_chain(lambda d, h: jnp.exp2(inter[d, h] - mj[d, h]))
    qc = per_chain(lambda d, h: _dot(q[d, h], c_prev[d, h].astype(MXU)))
    wv = per_chain(lambda d, h: _dot(wmat[d, h].astype(MXU), v[d, h]))
    qn = per_chain(lambda d, h: jnp.sum(q[d, h].astype(F32) * n_prev[d, h], axis=1, keepdims=True))
    den = per_chain(lambda d, h: g[d, h] * qn[d, h] + jnp.sum(wmat[d, h], axis=1, keepdims=True))
    for d, h in chains:
        num = g[d, h] * qc[d, h] + wv[d, h]
        dirs[d][4][:, vs(h)] = num / jnp.maximum(jnp.abs(den[d, h]), jnp.exp2(-mj[d, h]))

    ds = per_chain(lambda d, h: bl[d, h] - b_col[d, h] + li_col[d, h])
    m_new = per_chain(lambda d, h: jnp.maximum(bl[d, h] + m_prev[d, h], jnp.max(ds[d, h], axis=0, keepdims=True)))
    kw = per_chain(lambda d, h: k[d, h].astype(F32) * jnp.exp2(ds[d, h] - m_new[d, h]))
    decay = per_chain(lambda d, h: jnp.exp2(bl[d, h] + m_prev[d, h] - m_new[d, h]))
    kv = per_chain(lambda d, h: _dot_tn(kw[d, h].astype(MXU), v[d, h]))
    for d, h in chains:
        c_scr[d, h] = decay[d, h] * c_prev[d, h] + kv[d, h]
        n_scr[d, h] = decay[d, h] * n_prev[d, h] + jnp.sum(kw[d, h], axis=0, keepdims=True)
        m_scr[d, h] = m_new[d, h]


def _mlstm_scan(rs, q, k, v, gates):
    b, L = rs.batch, M_CHUNK
    dqk_all, dv_all = q.shape[1], v.shape[1]
    nc_ctx, nc_lat = rs.ctx_len // L, rs.seq // L
    rowblk = _scan_rowblock(b, nc_ctx, nc_lat)

    def specs(d):
        rb = lambda b_, c: (rowblk(b_, d, c), 0)
        return [pl.BlockSpec((L, dqk_all), rb), pl.BlockSpec((L, dqk_all), rb), pl.BlockSpec((L, dv_all), rb),
                pl.BlockSpec((L, LANES), rb)]

    out_shape = jax.ShapeDtypeStruct((rs.rows, dv_all), F32)
    dqk, dv = dqk_all // M_HEADS, dv_all // M_HEADS
    return pl.pallas_call(
        _mlstm_scan_kernel, grid=(b, nc_ctx + nc_lat),
        in_specs=specs(0) + specs(1),
        out_specs=[pl.BlockSpec((L, dv_all), lambda b_, c: (rowblk(b_, 0, c), 0)),
                   pl.BlockSpec((L, dv_all), lambda b_, c: (rowblk(b_, 1, c), 0))],
        out_shape=[out_shape, out_shape],
        scratch_shapes=[pltpu.VMEM((2, M_HEADS, dqk, dv), F32), pltpu.VMEM((2, M_HEADS, 1, dqk), F32),
                        pltpu.VMEM((2, M_HEADS, 1, 1), F32)],
        compiler_params=pltpu.CompilerParams(dimension_semantics=("arbitrary",) * 2,
                                             vmem_limit_bytes=_vmem_limit(40 * 2**20)),
        name="mlstm_scan",
    )(q, k, v, gates, q, k, v, gates)


def _mlstm_post_kernel(x_ref, s0_ref, s1_ref, mod_ref, wog_ref, ng_ref, wout_ref, g_ref, b_ref, o_ref, *, alpha):
    d = x_ref.shape[1]
    chunks = [slice(r0, r0 + HEAD_ROWS) for r0 in range(0, TM, HEAD_ROWS)]
    h = [_mod(x_ref[r, :], mod_ref, 0, d).astype(MXU) for r in chunks]
    og = [_sigmoid(_dot(hc, wog_ref[...])) for hc in h]
    a = [_head_rms(og[i] * (s0_ref[r, :] + s1_ref[r, :]), ng_ref[...], M_HEADS).astype(MXU) for i, r in enumerate(chunks)]
    _residual_ln(x_ref, mod_ref, jnp.concatenate(a, axis=0), wout_ref, g_ref, b_ref, o_ref, alpha)


def _mlstm_post(rs, n_tiles, x, s, mod, w_og, norm_g, w_out, ln_g, ln_b, alpha):
    d, inner = w_og.shape
    body = functools.partial(_mlstm_post_kernel, alpha=alpha)
    vmem = 2 * (w_og.size + w_out.size) + 10 * TM * inner * 4 + 16 * 2**20
    return _row_call(
        body, rs, n_tiles,
        [rs.tile(d), rs.tile(inner), rs.tile(inner), rs.mod(6 * d), _const_spec(w_og.shape), _const_spec((1, inner)),
         _const_spec(w_out.shape), _const_spec((1, d)), _const_spec((1, d))],
        rs.tile(d), jax.ShapeDtypeStruct((n_tiles * TM, d), F32), vmem=vmem, name="mlstm_post",
    )(x, s[0], s[1], mod, w_og, norm_g.reshape(1, inner), w_out, ln_g.reshape(1, d), ln_b.reshape(1, d))


def _mla_pre_kernel(x_ref, mod_ref, cos_ref, sin_ref, wdq_ref, qn_ref, wuq_ref, wdkv_ref, kvn_ref, wk_ref, wv_ref,
                    q_ref, k_ref, v_ref, *, q_scale):
    d = x_ref.shape[1]
    hd = A_DNOPE + LANES
    h = _mod(x_ref[...], mod_ref, 0, d).astype(MXU)
    cos, sin = cos_ref[...], sin_ref[...]
    cq = _rms(_dot(h, wdq_ref[...]), qn_ref[...]).astype(MXU)
    qa = _dot(cq, wuq_ref[...])
    part0 = A_HEADS * hd
    cos_q, sin_q = cos * q_scale, sin * q_scale
    for hh in range(A_HEADS):
        q_ref[:, hh * hd:hh * hd + A_DNOPE] = (qa[:, hh * hd:hh * hd + A_DNOPE] * q_scale).astype(q_ref.dtype)
        rot = qa[:, hh * hd + A_DNOPE:(hh + 1) * hd] * cos_q + qa[:, part0 + hh * LANES:part0 + (hh + 1) * LANES] * sin_q
        q_ref[:, hh * hd + A_DNOPE:(hh + 1) * hd] = rot.astype(q_ref.dtype)
    dk = _dot(h, wdkv_ref[...])
    k_rope = (dk[:, A_KVLORA:A_KVLORA + LANES] * cos + dk[:, A_KVLORA + LANES:] * sin).astype(k_ref.dtype)
    ckv = _rms(dk[:, :A_KVLORA], kvn_ref[...]).astype(MXU)
    k_nope = _dot(ckv, wk_ref[...])
    for hh in range(A_HEADS):
        k_ref[:, hh * hd:hh * hd + A_DNOPE] = k_nope[:, hh * A_DNOPE:(hh + 1) * A_DNOPE].astype(k_ref.dtype)
        k_ref[:, hh * hd + A_DNOPE:(hh + 1) * hd] = k_rope
    v_ref[...] = _dot(ckv, wv_ref[...]).astype(v_ref.dtype)


def _rope_tables(seq):
    n_freq = A_DROPE // 4
    inv_freq = ROPE_BASE ** (-jnp.arange(n_freq, dtype=F32) / n_freq)
    pos = jnp.arange(seq)
    ang_row = (pos // GRID_W).astype(F32)[:, None] * inv_freq
    ang_col = (pos % GRID_W).astype(F32)[:, None] * inv_freq
    cos = jnp.concatenate([jnp.cos(ang_row)] * 2 + [jnp.cos(ang_col)] * 2, axis=1)
    sin = jnp.concatenate([-jnp.sin(ang_row), jnp.sin(ang_row), -jnp.sin(ang_col), jnp.sin(ang_col)], axis=1)
    pad = LANES - A_DROPE
    cos = jnp.pad(cos, ((0, 0), (0, pad)), constant_values=1.0)
    sin = jnp.pad(sin, ((0, 0), (0, pad)))
    ident = (jnp.ones((TM, LANES), F32), jnp.zeros((TM, LANES), F32))
    return jnp.concatenate([ident[0], cos]), jnp.concatenate([ident[1], sin])


def _rope_partner_cols(w):
    idx = np.arange(A_DROPE)
    half = A_DROPE // 4
    partner = np.where((idx % (2 * half)) < half, idx + half, idx - half)
    return w[..., partner]


def _mla_pre(rs, x, mod, w_dq, q_norm, w_uq, w_dkv, kv_norm, w_ukv):
    d, qlora = w_dq.shape
    hd = A_DNOPE + LANES
    pad = LANES - A_DROPE
    wq = w_uq.reshape(qlora, A_HEADS, A_DNOPE + A_DROPE)
    wq_main = jnp.pad(wq, ((0, 0), (0, 0), (0, pad))).reshape(qlora, A_HEADS * hd)
    wq_part = jnp.pad(_rope_partner_cols(wq[..., A_DNOPE:]), ((0, 0), (0, 0), (0, pad))).reshape(qlora, A_HEADS * LANES)
    wuq_all = jnp.concatenate([wq_main, wq_part], axis=1).astype(MXU)
    w_kr = w_dkv[:, A_KVLORA:]
    wdkv_all = jnp.concatenate([w_dkv[:, :A_KVLORA], jnp.pad(w_kr, ((0, 0), (0, pad))),
                                jnp.pad(_rope_partner_cols(w_kr), ((0, 0), (0, pad)))], axis=1).astype(MXU)
    wkv = w_ukv.reshape(A_KVLORA, A_HEADS, A_DNOPE + A_DV)
    w_k = wkv[..., :A_DNOPE].reshape(A_KVLORA, A_HEADS * A_DNOPE).astype(MXU)
    w_v = wkv[..., A_DNOPE:].reshape(A_KVLORA, A_HEADS * A_DV).astype(MXU)
    cos, sin = _rope_tables(rs.seq)
    nl, tps = rs.n_lat_tiles, rs.tiles_per_seq
    tab = pl.BlockSpec((TM, LANES), lambda i: (jnp.where(i >= nl, 0, 1 + i % tps), 0))
    n, r = rs.n_tiles, rs.rows
    vmem = 2 * 2 * (w_dq.size + wuq_all.size + wdkv_all.size + w_k.size + w_v.size) + 12 * TM * A_HEADS * hd * 4
    return _row_call(
        functools.partial(_mla_pre_kernel, q_scale=float((A_DNOPE + A_DROPE) ** -0.5 * np.log2(np.e))), rs, n,
        [rs.tile(d), rs.mod(6 * d), tab, tab, _const_spec(w_dq.shape), _const_spec((1, qlora)), _const_spec(wuq_all.shape),
         _const_spec(wdkv_all.shape), _const_spec((1, A_KVLORA)), _const_spec(w_k.shape), _const_spec(w_v.shape)],
        [rs.tile(A_HEADS * hd), rs.tile(A_HEADS * hd), rs.tile(A_HEADS * A_DV)],
        [jax.ShapeDtypeStruct((r, A_HEADS * hd), MXU), jax.ShapeDtypeStruct((r, A_HEADS * hd), MXU),
         jax.ShapeDtypeStruct((r, A_HEADS * A_DV), MXU)],
        vmem=vmem, name="mla_pre",
    )(x, mod, cos, sin, w_dq.astype(MXU), q_norm.reshape(1, qlora), wuq_all, wdkv_all, kv_norm.reshape(1, A_KVLORA), w_k, w_v)


def _attn_kernel(*refs, with_latent, kv_chunk):
    if with_latent:
        q_ref, kc_ref, vc_ref, kl_ref, vl_ref, o_ref = refs
    else:
        q_ref, kc_ref, vc_ref, o_ref = refs
    hd, dv = A_DNOPE + LANES, A_DV
    heads = range(q_ref.shape[1] // hd)
    qs = lambda h: slice(h * hd, (h + 1) * hd)
    vs = lambda h: slice(h * dv, (h + 1) * dv)
    q = [q_ref[:, qs(h)] for h in heads]
    s = [_dot_nt(q[h], kc_ref[:, qs(h)]) for h in heads]
    m = [jnp.max(s[h], axis=1, keepdims=True) for h in heads]
    p = [jnp.exp2(s[h] - m[h]) for h in heads]
    l = [jnp.sum(p[h], axis=1, keepdims=True) for h in heads]
    acc = [_dot(p[h].astype(MXU), vc_ref[:, vs(h)]) for h in heads]
    if with_latent:
        for c0 in range(0, kl_ref.shape[0], kv_chunk):
            s = [_dot_nt(q[h], kl_ref[c0:c0 + kv_chunk, qs(h)]) for h in heads]
            m_new = [jnp.maximum(m[h], jnp.max(s[h], axis=1, keepdims=True)) for h in heads]
            corr = [jnp.exp2(m[h] - m_new[h]) for h in heads]
            p = [jnp.exp2(s[h] - m_new[h]) for h in heads]
            l = [l[h] * corr[h] + jnp.sum(p[h], axis=1, keepdims=True) for h in heads]
            acc = [acc[h] * corr[h] + _dot(p[h].astype(MXU), vl_ref[c0:c0 + kv_chunk, vs(h)]) for h in heads]
            m = m_new
    for h in heads:
        o_ref[:, vs(h)] = (acc[h] / l[h]).astype(o_ref.dtype)


def _mla_attention(rs, q, k, v):
    b, t = rs.batch, rs.seq
    hpb = 2
    hd, dv = hpb * (A_DNOPE + LANES), hpb * A_DV
    kv_chunk = min(1024, t)
    assert t % kv_chunk == 0 and A_HEADS % hpb == 0
    tc = rs.ctx_len
    ctx_blk0 = b * t // tc
    params = pltpu.CompilerParams(dimension_semantics=("arbitrary",) * 3, vmem_limit_bytes=_vmem_limit(48 * 2**20))
    o_ctx = pl.pallas_call(
        functools.partial(_attn_kernel, with_latent=False, kv_chunk=kv_chunk), grid=(b, A_HEADS // hpb, 1),
        in_specs=[pl.BlockSpec((tc, hd), lambda b_, h, i: (ctx_blk0 + b_, h)),
                  pl.BlockSpec((tc, hd), lambda b_, h, i: (ctx_blk0 + b_, h)),
                  pl.BlockSpec((tc, dv), lambda b_, h, i: (ctx_blk0 + b_, h))],
        out_specs=pl.BlockSpec((tc, dv), lambda b_, h, i: (b_, h)),
        out_shape=jax.ShapeDtypeStruct((b * tc, A_HEADS * A_DV), MXU), compiler_params=params, name="mla_attn_ctx",
    )(q, k, v)
    tq = ATTN_TQ
    nq = t // tq
    o_lat = pl.pallas_call(
        functools.partial(_attn_kernel, with_latent=True, kv_chunk=kv_chunk), grid=(b, A_HEADS // hpb, nq),
        in_specs=[pl.BlockSpec((tq, hd), lambda b_, h, i: (b_ * nq + i, h)),
                  pl.BlockSpec((tc, hd), lambda b_, h, i: (ctx_blk0 + b_, h)),
                  pl.BlockSpec((tc, dv), lambda b_, h, i: (ctx_blk0 + b_, h)),
                  pl.BlockSpec((t, hd), lambda b_, h, i: (b_, h)),
                  pl.BlockSpec((t, dv), lambda b_, h, i: (b_, h))],
        out_specs=pl.BlockSpec((tq, dv), lambda b_, h, i: (b_ * nq + i, h)),
        out_shape=jax.ShapeDtypeStruct((b * t, A_HEADS * A_DV), MXU), compiler_params=params, name="mla_attn_latent",
    )(q, k, v, k, v)
    return o_lat, o_ctx


def _proj_post_kernel(x_ref, al_ref, ac_ref, mod_ref, wout_ref, g_ref, b_ref, o_ref, *, alpha, n_lat_tiles):
    a = jnp.where(pl.program_id(0) >= n_lat_tiles, ac_ref[...], al_ref[...])
    _residual_ln(x_ref, mod_ref, a, wout_ref, g_ref, b_ref, o_ref, alpha)


def _proj_post(rs, n_tiles, x, a_lat, a_ctx, mod, w_out, ln_g, ln_b, alpha):
    kdim, d = w_out.shape
    nl = rs.n_lat_tiles
    body = functools.partial(_proj_post_kernel, alpha=alpha, n_lat_tiles=nl)
    return _row_call(
        body, rs, n_tiles,
        [rs.tile(d), pl.BlockSpec((TM, kdim), lambda i: (jnp.minimum(i, nl - 1), 0)),
         pl.BlockSpec((TM, kdim), lambda i: (jnp.maximum(i - nl, 0), 0)), rs.mod(6 * d), _const_spec(w_out.shape),
         _const_spec((1, d)), _const_spec((1, d))],
        rs.tile(d), jax.ShapeDtypeStruct((n_tiles * TM, d), F32), vmem=32 * 2**20, name="proj_post",
    )(x, a_lat, a_ctx, mod, w_out, ln_g.reshape(1, d), ln_b.reshape(1, d))


def _gla_pre_kernel(x_ref, mod_ref, wqk_ref, wv_ref, wa1_ref, wa2_ref, ba_ref, q_ref, k_ref, v_ref, la_ref,
                    *, q_scale):
    d = x_ref.shape[1]
    half = q_ref.shape[1]
    chunks = [slice(r0, r0 + HEAD_ROWS) for r0 in range(0, TM, HEAD_ROWS)]
    h = [_mod(x_ref[r, :], mod_ref, 0, d).astype(MXU) for r in chunks]
    a1 = [_dot(hc, wa1_ref[...]).astype(MXU) for hc in h]
    z = [_dot(a, wa2_ref[...]) + ba_ref[...] for a in a1]
    qk = [_dot(hc, wqk_ref[...]) for hc in h]
    for i, r in enumerate(chunks):
        la_ref[r, :] = _log_sigmoid(z[i]) * (LOG2E / G_TAU)
        q_ref[r, :] = qk[i][:, :half] * q_scale
        k_ref[r, :] = qk[i][:, half:]
    for i, r in enumerate(chunks):
        v_ref[r, :] = _dot(h[i], wv_ref[...]).astype(v_ref.dtype)


def _gla_pre(rs, x, mod, w_qk, w_v, w_a1, w_a2, b_a):
    d = w_qk.shape[0]
    dk_all = w_qk.shape[1] // 2
    wa1 = jnp.pad(jnp.concatenate([w_a1[0], w_a1[1]], axis=1), ((0, 0), (0, LANES - 2 * G_RANK))).astype(MXU)
    wa2 = jnp.zeros((LANES, 2 * dk_all), F32)
    wa2 = wa2.at[:G_RANK, :dk_all].set(w_a2[0]).at[G_RANK:2 * G_RANK, dk_all:].set(w_a2[1]).astype(MXU)
    ba = jnp.concatenate([b_a[0], b_a[1]]).reshape(1, 2 * dk_all)
    body = functools.partial(_gla_pre_kernel, q_scale=float((dk_all // G_HEADS) ** -0.5))
    n, r = rs.n_tiles, rs.rows
    return _row_call(
        body, rs, n,
        [rs.tile(d), rs.mod(6 * d), _const_spec(w_qk.shape), _const_spec(w_v.shape), _const_spec(wa1.shape),
         _const_spec(wa2.shape), _const_spec(ba.shape)],
        [rs.tile(dk_all), rs.tile(dk_all), rs.tile(w_v.shape[1]), rs.tile(2 * dk_all)],
        [jax.ShapeDtypeStruct((r, dk_all), F32), jax.ShapeDtypeStruct((r, dk_all), F32),
         jax.ShapeDtypeStruct((r, w_v.shape[1]), MXU), jax.ShapeDtypeStruct((r, 2 * dk_all), F32)],
        vmem=40 * 2**20, name="gla_pre",
    )(x, mod, w_qk, w_v, wa1, wa2, ba)


def _gla_tables(L):
    t = np.arange(L)
    tau, taup = t[:, None], t[None, :]
    groups = [taup <= tau, taup > tau]
    masks = []
    c = L // 2
    while c >= 1:
        blk = t // (2 * c)
        mid = blk * 2 * c + c
        second = (t % (2 * c)) >= c
        q_side = second[:, None] & (taup >= mid[:, None]) & (taup <= tau)
        k_side = (~second)[:, None] & (taup > tau) & (taup <= mid[:, None] - 1)
        groups.append(q_side | k_side)
        masks.append(second[:, None] & (~second)[None, :] & (blk[:, None] == blk[None, :]))
        c //= 2
    flip = lambda g: g[::-1, ::-1]
    sums = np.stack([np.concatenate(groups, axis=0), np.concatenate([flip(g) for g in groups], axis=0)])
    lvl = np.stack([np.stack(masks), np.stack([flip(m) for m in masks])])
    return sums.astype(np.float32), lvl.astype(np.float32)


def _gla_scan_kernel(q0_ref, k0_ref, v0_ref, la0_ref, q1_ref, k1_ref, v1_ref, la1_ref, sums_ref, masks_ref,
                     o0_ref, o1_ref, st_scr):
    @pl.when(pl.program_id(1) == 0)
    def _():
        st_scr[...] = jnp.zeros_like(st_scr)

    L = q0_ref.shape[0]
    dk, dv = q0_ref.shape[1] // G_HEADS, v0_ref.shape[1] // G_HEADS
    levels = masks_ref.shape[1]
    dirs = ((q0_ref, k0_ref, v0_ref, la0_ref, o0_ref), (q1_ref, k1_ref, v1_ref, la1_ref, o1_ref))
    chains = [(d, h) for d in range(2) for h in range(G_HEADS)]
    row = lax.broadcasted_iota(jnp.int32, (L, L), 0)
    col = lax.broadcasted_iota(jnp.int32, (L, L), 1)
    eye = row == col

    st_prev = {(d, h): st_scr[d, h] for d, h in chains}
    e = [jnp.exp2(_dot(sums_ref[d], jnp.concatenate(_split2(dirs[d][3][...]), axis=0))) for d in range(2)]
    q = [dirs[d][0][...] for d in range(2)]
    k = [dirs[d][1][...] for d in range(2)]
    qe = [(q[d] * e[d][0:L]).astype(MXU) for d in range(2)]
    kd = [(k[d] * e[d][L:2 * L]).astype(MXU) for d in range(2)]
    qk_diag = [q[d] * k[d] for d in range(2)]
    qt = [[(q[d] * e[d][(2 + lv) * L:(3 + lv) * L]).astype(MXU) for lv in range(levels)] for d in range(2)]
    kt = [[(k[d] * e[d][(2 + lv) * L:(3 + lv) * L]).astype(MXU) for lv in range(levels)] for d in range(2)]
    decay = [jnp.exp2(jnp.sum(dirs[d][3][...], axis=0, keepdims=True)) for d in range(2)]

    att = {}
    for d, h in chains:
        ks = slice(h * dk, (h + 1) * dk)
        a = jnp.where(eye, jnp.sum(qk_diag[d][:, ks], axis=1, keepdims=True), 0.0)
        for lv in range(levels):
            a = a + masks_ref[d, lv] * _dot_nt(qt[d][lv][:, ks], kt[d][lv][:, ks])
        att[d, h] = a.astype(MXU)
    for d, h in chains:
        ks, vs = slice(h * dk, (h + 1) * dk), slice(h * dv, (h + 1) * dv)
        v = dirs[d][2][:, vs]
        dirs[d][4][:, vs] = _dot_nt(qe[d][:, ks], st_prev[d, h].astype(MXU)) + _dot(att[d, h], v)
        st_scr[d, h] = st_prev[d, h] * decay[d][:, ks] + _dot_tn(v, kd[d][:, ks])


def _gla_scan(rs, q, k, v, la):
    b, L = rs.batch, G_CHUNK
    dk_all, dv_all = q.shape[1], v.shape[1]
    nc_ctx, nc_lat = rs.ctx_len // L, rs.seq // L
    rowblk = _scan_rowblock(b, nc_ctx, nc_lat)
    sums, masks = _gla_tables(L)
    sums, masks = jnp.asarray(np.concatenate([sums, sums], axis=2), MXU), jnp.asarray(masks, F32)

    def specs(d):
        rb = lambda b_, c: (rowblk(b_, d, c), 0)
        return [pl.BlockSpec((L, dk_all), rb), pl.BlockSpec((L, dk_all), rb), pl.BlockSpec((L, dv_all), rb),
                pl.BlockSpec((L, dk_all), lambda b_, c: (rowblk(b_, d, c), d))]

    out_shape = jax.ShapeDtypeStruct((rs.rows, dv_all), F32)
    return pl.pallas_call(
        _gla_scan_kernel, grid=(b, nc_ctx + nc_lat),
        in_specs=specs(0) + specs(1) + [_const_spec(sums.shape), _const_spec(masks.shape)],
        out_specs=[pl.BlockSpec((L, dv_all), lambda b_, c: (rowblk(b_, 0, c), 0)),
                   pl.BlockSpec((L, dv_all), lambda b_, c: (rowblk(b_, 1, c), 0))],
        out_shape=[out_shape, out_shape],
        scratch_shapes=[pltpu.VMEM((2, G_HEADS, dv_all // G_HEADS, dk_all // G_HEADS), F32)],
        compiler_params=pltpu.CompilerParams(dimension_semantics=("arbitrary",) * 2,
                                             vmem_limit_bytes=_vmem_limit(32 * 2**20)),
        name="gla_scan",
    )(q, k, v, la, q, k, v, la, sums, masks)


def _gla_post_kernel(x_ref, s0_ref, s1_ref, mod_ref, wr_ref, ng_ref, wout_ref, g_ref, b_ref, o_ref, *, alpha):
    d = x_ref.shape[1]
    chunks = [slice(r0, r0 + HEAD_ROWS) for r0 in range(0, TM, HEAD_ROWS)]
    h = [_mod(x_ref[r, :], mod_ref, 0, d).astype(MXU) for r in chunks]
    gate = [_silu(_dot(hc, wr_ref[...])) for hc in h]
    a = [(_head_rms(s0_ref[r, :] + s1_ref[r, :], ng_ref[...], G_HEADS) * gate[i]).astype(MXU) for i, r in enumerate(chunks)]
    _residual_ln(x_ref, mod_ref, jnp.concatenate(a, axis=0), wout_ref, g_ref, b_ref, o_ref, alpha)


def _gla_post(rs, n_tiles, x, s, mod, w_r, norm_g, w_out, ln_g, ln_b, alpha):
    d, dv_all = w_r.shape
    body = functools.partial(_gla_post_kernel, alpha=alpha)
    return _row_call(
        body, rs, n_tiles,
        [rs.tile(d), rs.tile(dv_all), rs.tile(dv_all), rs.mod(6 * d), _const_spec(w_r.shape), _const_spec((1, dv_all)), _const_spec(w_out.shape),
         _const_spec((1, d)), _const_spec((1, d))],
        rs.tile(d), jax.ShapeDtypeStruct((n_tiles * TM, d), F32), vmem=40 * 2**20, name="gla_post",
    )(x, s[0], s[1], mod, w_r, norm_g.reshape(1, dv_all), w_out, ln_g.reshape(1, d), ln_b.reshape(1, d))


def kernel(x, c, ctx, c_ctx, ada_w, ada_b, ln_g, ln_b, ffn_w_in, ffn_conv_w, ffn_conv_b, ffn_w_out, m_w_up, m_conv_w, m_conv_b, m_w_qk, m_w_v, m_w_gates, m_b_gates, m_w_og, m_norm_g, m_w_out, a_w_dq, a_q_norm, a_w_uq, a_w_dkv, a_kv_norm, a_w_ukv, a_w_out, g_w_qk, g_w_v, g_w_r, g_w_a1, g_w_a2, g_b_a, g_norm_g, g_w_out):
    batch, seq, d = x.shape
    depth = ada_w.shape[0]
    n_mixers = 3
    alpha = float((2 * depth) ** 0.25)
    rs = _Rows(batch, seq, ctx.shape[1], TM)
    rs_ffn = _Rows(batch, seq, ctx.shape[1], FFN_TM)
    bf = lambda w: w.astype(MXU)

    cond_rows = -(-(batch + 1) // SUBLANES) * SUBLANES
    cond = jnp.zeros((cond_rows, d), F32).at[:batch].set(c).at[batch].set(c_ctx)
    mods = _modulation(cond, ada_w, ada_b).reshape(depth, cond_rows, 1, 6 * d)

    xa = jnp.concatenate([x.reshape(batch * seq, d), ctx.reshape(-1, d)], axis=0)
    for i in range(depth):
        need_ctx = i < depth - 1
        n_tiles = rs.n_tiles if need_ctx else rs.n_lat_tiles
        kind, j = i % n_mixers, i // n_mixers
        mod = mods[i]
        if kind == 0:
            q, k, v, gates = _mlstm_pre(rs, xa, mod, bf(m_w_up[j]), m_conv_w[j], m_conv_b[j], bf(m_w_qk[j]),
                                        _weight_product(m_w_up[j], m_w_v[j]), m_w_gates[j], m_b_gates[j])
            s = _mlstm_scan(rs, q, k, v, gates)
            xa = _mlstm_post(rs, n_tiles, xa, s, mod, bf(m_w_og[j]), m_norm_g[j], bf(m_w_out[j]), ln_g[i, 0], ln_b[i, 0], alpha)
        elif kind == 1:
            q, k, v = _mla_pre(rs, xa, mod, a_w_dq[j], a_q_norm[j], a_w_uq[j], a_w_dkv[j], a_kv_norm[j], a_w_ukv[j])
            o_lat, o_ctx = _mla_attention(rs, q, k, v)
            xa = _proj_post(rs, n_tiles, xa, o_lat, o_ctx, mod, bf(a_w_out[j]), ln_g[i, 0], ln_b[i, 0], alpha)
        else:
            q, k, v, la = _gla_pre(rs, xa, mod, bf(g_w_qk[j]), bf(g_w_v[j]), g_w_a1[j], g_w_a2[j], g_b_a[j])
            s = _gla_scan(rs, q, k, v, la)
            xa = _gla_post(rs, n_tiles, xa, s, mod, bf(g_w_r[j]), g_norm_g[j], bf(g_w_out[j]), ln_g[i, 0], ln_b[i, 0], alpha)
        xa = _ffn(rs_ffn, rs_ffn.n_tiles if need_ctx else rs_ffn.n_lat_tiles, xa, mod, bf(ffn_w_in[i]), ffn_conv_w[i],
                  ffn_conv_b[i], bf(ffn_w_out[i]), ln_g[i, 1], ln_b[i, 1], alpha)
    return xa.reshape(batch, seq, d)
```

```python
import functools

import numpy as np
import jax
import jax.numpy as jnp
from jax import lax
from jax.experimental import pallas as pl
from jax.experimental.pallas import tpu as pltpu

F32 = jnp.float32
MXU = jnp.bfloat16

V7X_VMEM_BYTES = 64 * 2**20
SUBLANES = 8
LANES = 128

TM = 512
BIG_TM = 1024
ATTN_TQ = 1024
HEAD_ROWS = 256
TAIL_ROWS = 128
HALO = SUBLANES
GRID_W = 64
EPS = 1e-6
ROPE_BASE = 10000.0
G_TAU = 16.0
LOG2E = float(np.log2(np.e))

M_HEADS, A_HEADS, G_HEADS = 4, 8, 4
A_DNOPE, A_DROPE, A_DV, A_KVLORA = 128, 64, 128, 256
G_RANK = 16
M_CHUNK = 256
M_SCAN_GROUP = 8
G_CHUNK = 128


def _vmem_limit(nbytes):
    return int(min(max(nbytes, 16 * 2**20), V7X_VMEM_BYTES - 8 * 2**20))


def _const_spec(shape):
    nd = len(shape)
    return pl.BlockSpec(shape, lambda *_: (0,) * nd, pipeline_mode=pl.Buffered(1))


def _dot(a, b):
    return jnp.dot(a, b, preferred_element_type=F32)


def _dot_nt(a, b):
    return lax.dot_general(a, b, (((1,), (1,)), ((), ())), preferred_element_type=F32)


def _dot_tn(a, b):
    return lax.dot_general(a, b, (((0,), (0,)), ((), ())), preferred_element_type=F32)


def _split3(x):
    hi = x.astype(MXU)
    r1 = x - hi.astype(F32)
    mid = r1.astype(MXU)
    lo = (r1 - mid.astype(F32)).astype(MXU)
    return hi, mid, lo


def _split2(x):
    hi = x.astype(MXU)
    return hi, (x - hi.astype(F32)).astype(MXU)


def _sigmoid(x):
    return 1.0 / (1.0 + jnp.exp(-x))


def _silu(x):
    return x * _sigmoid(x)


def _log_sigmoid(x):
    return jnp.minimum(x, 0.0) - jnp.log1p(jnp.exp(-jnp.abs(x)))


def _layer_norm(z, g, b):
    mu = jnp.mean(z, -1, keepdims=True)
    zc = z - mu
    var = jnp.mean(zc * zc, -1, keepdims=True)
    return zc * lax.rsqrt(var + EPS) * g + b


def _rms(x, g):
    return x * lax.rsqrt(jnp.mean(x * x, -1, keepdims=True) + EPS) * g


def _head_rms(x, g, heads):
    d = x.shape[-1] // heads
    return jnp.concatenate([_rms(x[:, h * d:(h + 1) * d], g[:, h * d:(h + 1) * d]) for h in range(heads)], axis=-1)


def _mod(x, mod_ref, k, d):
    return x * (1.0 + mod_ref[:, (k + 1) * d:(k + 2) * d]) + mod_ref[:, k * d:(k + 1) * d]


def _halo_rows(xp_ref, x, xn_ref, mod_ref, k, d, geom):
    n_lat_tiles, tiles_per_seq, _, _ = geom
    i = pl.program_id(0)
    is_ctx = i >= n_lat_tiles
    pos = i % tiles_per_seq
    first = jnp.logical_or(is_ctx, pos == 0)
    last = jnp.logical_or(is_ctx, pos == tiles_per_seq - 1)
    hp = jnp.where(first, 0.0, _mod(xp_ref[...], mod_ref, k, d))
    hn = jnp.where(last, 0.0, _mod(xn_ref[...], mod_ref, k, d))
    return jnp.concatenate([hp, _mod(x, mod_ref, k, d), hn], axis=0)


def _tile_rows(xl_ref, xc_ref, geom):
    return jnp.where(pl.program_id(0) >= geom[0], xc_ref[...], xl_ref[...])


def _seq_edges(geom):
    n_lat_tiles, _, ctx_len, tm = geom
    is_ctx = pl.program_id(0) >= n_lat_tiles
    r = lax.broadcasted_iota(jnp.int32, (tm, 1), 0) % ctx_len
    return jnp.logical_and(is_ctx, r == 0), jnp.logical_and(is_ctx, r == ctx_len - 1)


def _dwconv3(g_ext, w_ref, b_ref, c0, c1, edges):
    n = g_ext.shape[0] - 2 * HALO
    starts, ends = edges
    prev = jnp.where(starts, 0.0, g_ext[HALO - 1:HALO - 1 + n])
    nxt = jnp.where(ends, 0.0, g_ext[HALO + 1:HALO + 1 + n])
    return (w_ref[0:1, c0:c1] * prev + w_ref[1:2, c0:c1] * g_ext[HALO:HALO + n] + w_ref[2:3, c0:c1] * nxt
            + b_ref[:, c0:c1])


class _Rows:
    def __init__(self, batch, seq, ctx_len, tm):
        assert tm % ctx_len == 0 and (batch * ctx_len) % tm == 0 and seq % tm == 0
        self.batch, self.seq, self.ctx_len, self.tm = batch, seq, ctx_len, tm
        self.tiles_per_seq = seq // tm
        self.n_lat_tiles = batch * self.tiles_per_seq
        self.n_tiles = self.n_lat_tiles + batch * ctx_len // tm
        self.rows = self.n_tiles * tm
        self.geom = (self.n_lat_tiles, self.tiles_per_seq, ctx_len, tm)

    def tile(self, width):
        return pl.BlockSpec((self.tm, width), lambda i: (i, 0))

    def tile_pair(self, width, ctx_blk0):
        nl = self.n_lat_tiles
        return [pl.BlockSpec((self.tm, width), lambda i: (jnp.minimum(i, nl - 1), 0)),
                pl.BlockSpec((self.tm, width), lambda i: (ctx_blk0 + jnp.maximum(i - nl, 0), 0))]

    def halo_prev(self, width, n_rows):
        per, nblk = self.tm // HALO, n_rows // HALO
        return pl.BlockSpec((HALO, width), lambda i: (jnp.clip(i * per - 1, 0, nblk - 1), 0))

    def halo_next(self, width, n_rows):
        per, nblk = self.tm // HALO, n_rows // HALO
        return pl.BlockSpec((HALO, width), lambda i: (jnp.minimum((i + 1) * per, nblk - 1), 0))

    def mod(self, width):
        nl, tps, b = self.n_lat_tiles, self.tiles_per_seq, self.batch
        return pl.BlockSpec((None, 1, width), lambda i: (jnp.where(i >= nl, b, i // tps), 0, 0))


def _row_call(body, rows, n_tiles, in_specs, out_specs, out_shape, scratch=(), vmem=0, name=None):
    return pl.pallas_call(
        body, grid=(n_tiles,), in_specs=in_specs, out_specs=out_specs, out_shape=out_shape,
        scratch_shapes=list(scratch), name=name,
        compiler_params=pltpu.CompilerParams(dimension_semantics=("arbitrary",), vmem_limit_bytes=_vmem_limit(vmem)))


def _modulation_kernel(c_ref, w_ref, b_ref, o_ref):
    o_ref[...] = _dot(_silu(c_ref[...]).astype(MXU), w_ref[...].astype(MXU)) + b_ref[...]


def _modulation(cond, ada_w, ada_b):
    depth, d, n = ada_w.shape
    tn = d
    return pl.pallas_call(
        _modulation_kernel, grid=(depth, n // tn),
        in_specs=[pl.BlockSpec(cond.shape, lambda l, j: (0, 0)),
                  pl.BlockSpec((None, d, tn), lambda l, j: (l, 0, j)),
                  pl.BlockSpec((None, 1, tn), lambda l, j: (l, 0, j))],
        out_specs=pl.BlockSpec((None, cond.shape[0], tn), lambda l, j: (l, 0, j)),
        out_shape=jax.ShapeDtypeStruct((depth, cond.shape[0], n), F32), name="modulation",
    )(cond, ada_w, ada_b.reshape(depth, 1, n))


def _ffn_kernel(xp_ref, x_ref, xn_ref, mod_ref, win_ref, cw_ref, cb_ref, wout_ref, g_ref, b_ref, o_ref, a_scr,
                *, geom, alpha, ffn, tf):
    tm, d = x_ref.shape
    hext = _halo_rows(xp_ref, x_ref[...], xn_ref, mod_ref, 3, d, geom).astype(MXU)
    edges = _seq_edges(geom)
    for f0 in range(0, ffn, tf):
        g_ext = _dot(hext, win_ref[:, f0:f0 + tf])
        up = _dot(hext, win_ref[:, ffn + f0:ffn + f0 + tf])[HALO:HALO + tm]
        a_scr[:, f0:f0 + tf] = (_silu(_dwconv3(g_ext, cw_ref, cb_ref, f0, f0 + tf, edges)) * up).astype(MXU)
    for r0 in range(0, tm, TAIL_ROWS):
        rows = slice(r0, r0 + TAIL_ROWS)
        f = _dot(a_scr[rows, :], wout_ref[...])
        z = alpha * x_ref[rows, :] + mod_ref[:, 5 * d:6 * d] * f
        o_ref[rows, :] = _layer_norm(z, g_ref[...], b_ref[...])


def _ffn(rs, n_tiles, x, mod, w_in, conv_w, conv_b, w_out, ln_g, ln_b, alpha):
    d, ffn = w_out.shape[1], w_out.shape[0]
    tf = 256
    assert ffn % tf == 0
    body = functools.partial(_ffn_kernel, geom=rs.geom, alpha=alpha, ffn=ffn, tf=tf)
    vmem = 2 * (w_in.size + w_out.size) + 6 * rs.tm * d * 4 + rs.tm * ffn * 2 + 24 * 2**20
    return _row_call(
        body, rs, n_tiles,
        [rs.halo_prev(d, x.shape[0]), rs.tile(d), rs.halo_next(d, x.shape[0]), rs.mod(6 * d), _const_spec(w_in.shape), _const_spec(conv_w.shape),
         _const_spec((1, ffn)), _const_spec(w_out.shape), _const_spec((1, d)), _const_spec((1, d))],
        rs.tile(d), jax.ShapeDtypeStruct((n_tiles * rs.tm, d), F32),
        scratch=[pltpu.VMEM((rs.tm, ffn), MXU)], vmem=vmem, name="conv_ffn",
    )(x, x, x, mod, w_in, conv_w, conv_b.reshape(1, ffn), w_out, ln_g.reshape(1, d), ln_b.reshape(1, d))


def _residual_ln(x_ref, mod_ref, a, wout_ref, g_ref, b_ref, o_ref, alpha):
    tm, d = x_ref.shape
    for r0 in range(0, tm, TAIL_ROWS):
        rows = slice(r0, r0 + TAIL_ROWS)
        z = alpha * x_ref[rows, :] + mod_ref[:, 2 * d:3 * d] * _dot(a[rows, :], wout_ref[...])
        o_ref[rows, :] = _layer_norm(z, g_ref[...], b_ref[...])


def _weight_product_kernel(a_ref, b_ref, o_ref):
    o_ref[...] = _dot(a_ref[...].astype(MXU), b_ref[...].astype(MXU)).astype(o_ref.dtype)


def _weight_product(a, b):
    m, kdim = a.shape
    n = b.shape[1]
    tn = 512
    assert n % tn == 0
    return pl.pallas_call(
        _weight_product_kernel, grid=(n // tn,),
        in_specs=[pl.BlockSpec((m, kdim), lambda j: (0, 0)), pl.BlockSpec((kdim, tn), lambda j: (0, j))],
        out_specs=pl.BlockSpec((m, tn), lambda j: (0, j)),
        out_shape=jax.ShapeDtypeStruct((m, n), MXU),
        compiler_params=pltpu.CompilerParams(dimension_semantics=("arbitrary",), vmem_limit_bytes=_vmem_limit(40 * 2**20)),
        name="weight_product",
    )(a, b)


def _mlstm_pre_kernel(xp_ref, xl_ref, xc_ref, xn_ref, mod_ref, wup_ref, cw_ref, cb_ref, wqk_ref, wv_ref, wg_ref, bg_ref,
                      q_ref, k_ref, v_ref, gates_ref, *, geom, k_scale):
    d = xl_ref.shape[1]
    x = _tile_rows(xl_ref, xc_ref, geom)
    hext = _halo_rows(xp_ref, x, xn_ref, mod_ref, 0, d, geom).astype(MXU)
    xm_ext = _dot(hext, wup_ref[...])
    inner = xm_ext.shape[1]
    xc = _silu(_dwconv3(xm_ext, cw_ref, cb_ref, 0, inner, _seq_edges(geom))).astype(MXU)
    qk = _dot(xc, wqk_ref[...])
    half = qk.shape[1] // 2
    q_ref[...] = qk[:, :half].astype(q_ref.dtype)
    k_ref[...] = (qk[:, half:] * k_scale).astype(k_ref.dtype)
    v_ref[...] = _dot(_mod(x, mod_ref, 0, d).astype(MXU), wv_ref[...]).astype(v_ref.dtype)
    two_h = 2 * M_HEADS
    gates = _dot(xc, wg_ref[...]) + bg_ref[...]
    lane = lax.broadcasted_iota(jnp.int32, gates.shape, 1)
    gates_ref[...] = jnp.where(lane % two_h >= M_HEADS, _log_sigmoid(gates), gates) * LOG2E


def _mlstm_pre(rs, x_lat, x_ctx, ctx_blk0, mod, w_up, conv_w, conv_b, w_qk, w_v, w_gates, b_gates):
    d, inner = w_up.shape
    dqk_all = w_qk.shape[1] // 2
    n_g = 2 * 2 * M_HEADS
    wg = jnp.concatenate([w_gates[0], w_gates[1]], axis=1)
    wg_pad = jnp.pad(wg, ((0, 0), (0, LANES - n_g))).astype(MXU)
    bg = jnp.concatenate([b_gates[0], b_gates[1]])
    bg_pad = jnp.pad(bg, (0, LANES - n_g)).reshape(1, LANES)
    body = functools.partial(_mlstm_pre_kernel, geom=rs.geom, k_scale=float((dqk_all // M_HEADS) ** -0.5))
    n, r = rs.n_tiles, rs.rows
    vmem = 2 * (w_up.size + w_qk.size + w_v.size) + 8 * (TM + 2 * HALO) * inner * 4 + 16 * 2**20
    return _row_call(
        body, rs, n,
        [rs.halo_prev(d, x_lat.shape[0])] + rs.tile_pair(d, ctx_blk0) + [rs.halo_next(d, x_lat.shape[0]), rs.mod(6 * d),
         _const_spec(w_up.shape), _const_spec(conv_w.shape), _const_spec((1, inner)), _const_spec(w_qk.shape),
         _const_spec(w_v.shape), _const_spec((inner, LANES)), _const_spec((1, LANES))],
        [rs.tile(dqk_all), rs.tile(dqk_all), rs.tile(inner), rs.tile(LANES)],
        [jax.ShapeDtypeStruct((r, dqk_all), MXU), jax.ShapeDtypeStruct((r, dqk_all), MXU),
         jax.ShapeDtypeStruct((r, inner), MXU), jax.ShapeDtypeStruct((r, LANES), F32)],
        vmem=vmem, name="mlstm_pre",
    )(x_lat, x_lat, x_ctx, x_lat, mod, w_up, conv_w, conv_b.reshape(1, inner), w_qk, w_v, wg_pad, bg_pad)


def _scan_rowblock(batch, nc_ctx, nc_lat):
    def rowblk(b, d, c):
        cc = jnp.where(d == 1, nc_ctx - 1 - c, c)
        lc = c - nc_ctx
        lc = jnp.where(d == 1, nc_lat - 1 - lc, lc)
        return jnp.where(c < nc_ctx, batch * nc_lat + b * nc_ctx + cc, b * nc_lat + lc)
    return rowblk


def _mlstm_scan_kernel(q0_ref, k0_ref, v0_ref, g0_ref, q1_ref, k1_ref, v1_ref, g1_ref,
                       o0_ref, o1_ref, c_scr, n_scr, m_scr):
    @pl.when(pl.program_id(1) == 0)
    def _():
        c_scr[...] = jnp.zeros_like(c_scr)
        n_scr[...] = jnp.zeros_like(n_scr)
        m_scr[...] = jnp.zeros_like(m_scr)

    L = q0_ref.shape[0]
    dqk, dv = q0_ref.shape[1] // M_HEADS, v0_ref.shape[1] // M_HEADS
    two_h = 2 * M_HEADS
    dirs = ((q0_ref, k0_ref, v0_ref, g0_ref, o0_ref), (q1_ref, k1_ref, v1_ref, g1_ref, o1_ref))
    chains = [(d, h) for d in range(2) for h in range(M_HEADS)]
    row = lax.broadcasted_iota(jnp.int32, (L, L), 0)
    col = lax.broadcasted_iota(jnp.int32, (L, L), 1)
    causal = [col <= row, col >= row]
    causal_t = [row <= col, row >= col]
    eye = jnp.where(row == col, 1.0, 0.0).astype(MXU)
    gates = [dirs[d][3][...] for d in range(2)]
    pieces = [_split3(gates[d]) for d in range(2)]
    cs = [sum(_dot(jnp.where(causal[d], 1.0, 0.0).astype(MXU), p) for p in pieces[d]) for d in range(2)]
    cs_t = [sum(_dot_tn(p, jnp.where(causal_t[d], 1.0, 0.0).astype(MXU)) for p in pieces[d]) for d in range(2)]
    gates_t = [sum(_dot_tn(p, eye) for p in pieces[d]) for d in range(2)]

    i_slot = lambda d, h: d * two_h + h
    f_slot = lambda d, h: d * two_h + M_HEADS + h
    qs = lambda h: slice(h * dqk, (h + 1) * dqk)
    vs = lambda h: slice(h * dv, (h + 1) * dv)
    for g0 in range(0, len(chains), M_SCAN_GROUP):
        _mlstm_advance(chains[g0:g0 + M_SCAN_GROUP], dirs, causal, gates, gates_t, cs, cs_t, i_slot, f_slot, qs, vs,
                       c_scr, n_scr, m_scr)


def _mlstm_advance(chains, dirs, causal, gates, gates_t, cs, cs_t, i_slot, f_slot, qs, vs, c_scr, n_scr, m_scr):
    def per_chain(fn):
        return {ch: fn(*ch) for ch in chains}

    c_prev = per_chain(lambda d, h: c_scr[d, h])
    n_prev = per_chain(lambda d, h: n_scr[d, h])
    m_prev = per_chain(lambda d, h: m_scr[d, h])
    q = per_chain(lambda d, h: dirs[d][0][:, qs(h)])
    k = per_chain(lambda d, h: dirs[d][1][:, qs(h)])
    v = per_chain(lambda d, h: dirs[d][2][:, vs(h)])
    li_col = per_chain(lambda d, h: gates[d][:, i_slot(d, h):i_slot(d, h) + 1])
    li_row = per_chain(lambda d, h: gates_t[d][i_slot(d, h):i_slot(d, h) + 1, :])
    b_col = per_chain(lambda d, h: cs[d][:, f_slot(d, h):f_slot(d, h) + 1])
    b_row = per_chain(lambda d, h: cs_t[d][f_slot(d, h):f_slot(d, h) + 1, :])
    bl = per_chain(lambda d, h: jnp.sum(gates_t[d][f_slot(d, h):f_slot(d, h) + 1, :], axis=1, keepdims=True))

    dmat = per_chain(lambda d, h: jnp.where(causal[d], b_col[d, h] - b_row[d, h] + li_row[d, h], -jnp.inf))
    inter = per_chain(lambda d, h: b_col[d, h] + m_prev[d, h])
    mj = per_chain(lambda d, h: jnp.maximum(inter[d, h], jnp.max(dmat[d, h], axis=1, keepdims=True)))
    qk = per_chain(lambda d, h: _dot_nt(q[d, h], k[d, h]))
    wmat = per_chain(lambda d, h: jnp.exp2(dmat[d, h] - mj[d, h]) * qk[d, h])
    g = per_chain(lambda d, h: jnp.exp2(inter[d, h] - mj[d, h]))
    qc = per_chain(lambda d, h: _dot(q[d, h], c_prev[d, h].astype(MXU)))
    wv = per_chain(lambda d, h: _dot(wmat[d, h].astype(MXU), v[d, h]))
    qn = per_chain(lambda d, h: jnp.sum(q[d, h].astype(F32) * n_prev[d, h], axis=1, keepdims=True))
    den = per_chain(lambda d, h: g[d, h] * qn[d, h] + jnp.sum(wmat[d, h], axis=1, keepdims=True))
    for d, h in chains:
        num = g[d, h] * qc[d, h] + wv[d, h]
        dirs[d][4][:, vs(h)] = num / jnp.maximum(jnp.abs(den[d, h]), jnp.exp2(-mj[d, h]))

    ds = per_chain(lambda d, h: bl[d, h] - b_col[d, h] + li_col[d, h])
    m_new = per_chain(lambda d, h: jnp.maximum(bl[d, h] + m_prev[d, h], jnp.max(ds[d, h], axis=0, keepdims=True)))
    kw = per_chain(lambda d, h: k[d, h].astype(F32) * jnp.exp2(ds[d, h] - m_new[d, h]))
    decay = per_chain(lambda d, h: jnp.exp2(bl[d, h] + m_prev[d, h] - m_new[d, h]))
    kv = per_chain(lambda d, h: _dot_tn(kw[d, h].astype(MXU), v[d, h]))
    for d, h in chains:
        c_scr[d, h] = decay[d, h] * c_prev[d, h] + kv[d, h]
        n_scr[d, h] = decay[d, h] * n_prev[d, h] + jnp.sum(kw[d, h], axis=0, keepdims=True)
        m_scr[d, h] = m_new[d, h]


def _mlstm_scan(rs, q, k, v, gates):
    b, L = rs.batch, M_CHUNK
    dqk_all, dv_all = q.shape[1], v.shape[1]
    nc_ctx, nc_lat = rs.ctx_len // L, rs.seq // L
    rowblk = _scan_rowblock(b, nc_ctx, nc_lat)

    def specs(d):
        rb = lambda b_, c: (rowblk(b_, d, c), 0)
        return [pl.BlockSpec((L, dqk_all), rb), pl.BlockSpec((L, dqk_all), rb), pl.BlockSpec((L, dv_all), rb),
                pl.BlockSpec((L, LANES), rb)]

    out_shape = jax.ShapeDtypeStruct((rs.rows, dv_all), F32)
    dqk, dv = dqk_all // M_HEADS, dv_all // M_HEADS
    return pl.pallas_call(
        _mlstm_scan_kernel, grid=(b, nc_ctx + nc_lat),
        in_specs=specs(0) + specs(1),
        out_specs=[pl.BlockSpec((L, dv_all), lambda b_, c: (rowblk(b_, 0, c), 0)),
                   pl.BlockSpec((L, dv_all), lambda b_, c: (rowblk(b_, 1, c), 0))],
        out_shape=[out_shape, out_shape],
        scratch_shapes=[pltpu.VMEM((2, M_HEADS, dqk, dv), F32), pltpu.VMEM((2, M_HEADS, 1, dqk), F32),
                        pltpu.VMEM((2, M_HEADS, 1, 1), F32)],
        compiler_params=pltpu.CompilerParams(dimension_semantics=("arbitrary",) * 2,
                                             vmem_limit_bytes=_vmem_limit(40 * 2**20)),
        name="mlstm_scan",
    )(q, k, v, gates, q, k, v, gates)


def _mlstm_post_kernel(xl_ref, xc_ref, s0_ref, s1_ref, mod_ref, wog_ref, ng_ref, wout_ref, g_ref, b_ref, o_ref,
                       *, alpha, geom):
    d = xl_ref.shape[1]
    x = _tile_rows(xl_ref, xc_ref, geom)
    chunks = [slice(r0, r0 + HEAD_ROWS) for r0 in range(0, x.shape[0], HEAD_ROWS)]
    h = [_mod(x[r, :], mod_ref, 0, d).astype(MXU) for r in chunks]
    og = [_sigmoid(_dot(hc, wog_ref[...])) for hc in h]
    a = [_head_rms(og[i] * (s0_ref[r, :] + s1_ref[r, :]), ng_ref[...], M_HEADS).astype(MXU) for i, r in enumerate(chunks)]
    _residual_ln(x, mod_ref, jnp.concatenate(a, axis=0), wout_ref, g_ref, b_ref, o_ref, alpha)


def _mlstm_post(rs, n_tiles, x_lat, x_ctx, ctx_blk0, s, mod, w_og, norm_g, w_out, ln_g, ln_b, alpha):
    d, inner = w_og.shape
    body = functools.partial(_mlstm_post_kernel, alpha=alpha, geom=rs.geom)
    vmem = 2 * (w_og.size + w_out.size) + 10 * TM * inner * 4 + 16 * 2**20
    return _row_call(
        body, rs, n_tiles,
        rs.tile_pair(d, ctx_blk0) + [rs.tile(inner), rs.tile(inner), rs.mod(6 * d), _const_spec(w_og.shape),
                                     _const_spec((1, inner)), _const_spec(w_out.shape), _const_spec((1, d)), _const_spec((1, d))],
        rs.tile(d), jax.ShapeDtypeStruct((n_tiles * rs.tm, d), F32), vmem=vmem, name="mlstm_post",
    )(x_lat, x_ctx, s[0], s[1], mod, w_og, norm_g.reshape(1, inner), w_out, ln_g.reshape(1, d), ln_b.reshape(1, d))


def _mla_pre_kernel(x_ref, mod_ref, cos_ref, sin_ref, wdq_ref, qn_ref, wuq_ref, wdkv_ref, kvn_ref, wk_ref, wv_ref,
                    q_ref, k_ref, v_ref, *, q_scale):
    d = x_ref.shape[1]
    hd = A_DNOPE + LANES
    h = _mod(x_ref[...], mod_ref, 0, d).astype(MXU)
    cos, sin = cos_ref[...], sin_ref[...]
    cq = _rms(_dot(h, wdq_ref[...]), qn_ref[...]).astype(MXU)
    qa = _dot(cq, wuq_ref[...])
    part0 = A_HEADS * hd
    cos_q, sin_q = cos * q_scale, sin * q_scale
    for hh in range(A_HEADS):
        q_ref[:, hh * hd:hh * hd + A_DNOPE] = (qa[:, hh * hd:hh * hd + A_DNOPE] * q_scale).astype(q_ref.dtype)
        rot = qa[:, hh * hd + A_DNOPE:(hh + 1) * hd] * cos_q + qa[:, part0 + hh * LANES:part0 + (hh + 1) * LANES] * sin_q
        q_ref[:, hh * hd + A_DNOPE:(hh + 1) * hd] = rot.astype(q_ref.dtype)
    dk = _dot(h, wdkv_ref[...])
    k_rope = (dk[:, A_KVLORA:A_KVLORA + LANES] * cos + dk[:, A_KVLORA + LANES:] * sin).astype(k_ref.dtype)
    ckv = _rms(dk[:, :A_KVLORA], kvn_ref[...]).astype(MXU)
    k_nope = _dot(ckv, wk_ref[...])
    for hh in range(A_HEADS):
        k_ref[:, hh * hd:hh * hd + A_DNOPE] = k_nope[:, hh * A_DNOPE:(hh + 1) * A_DNOPE].astype(k_ref.dtype)
        k_ref[:, hh * hd + A_DNOPE:(hh + 1) * hd] = k_rope
    v_ref[...] = _dot(ckv, wv_ref[...]).astype(v_ref.dtype)


def _rope_tables(seq, tm):
    n_freq = A_DROPE // 4
    inv_freq = ROPE_BASE ** (-jnp.arange(n_freq, dtype=F32) / n_freq)
    pos = jnp.arange(seq)
    ang_row = (pos // GRID_W).astype(F32)[:, None] * inv_freq
    ang_col = (pos % GRID_W).astype(F32)[:, None] * inv_freq
    cos = jnp.concatenate([jnp.cos(ang_row)] * 2 + [jnp.cos(ang_col)] * 2, axis=1)
    sin = jnp.concatenate([-jnp.sin(ang_row), jnp.sin(ang_row), -jnp.sin(ang_col), jnp.sin(ang_col)], axis=1)
    pad = LANES - A_DROPE
    cos = jnp.pad(cos, ((0, 0), (0, pad)), constant_values=1.0)
    sin = jnp.pad(sin, ((0, 0), (0, pad)))
    ident = (jnp.ones((tm, LANES), F32), jnp.zeros((tm, LANES), F32))
    return jnp.concatenate([ident[0], cos]), jnp.concatenate([ident[1], sin])


def _rope_partner_cols(w):
    idx = np.arange(A_DROPE)
    half = A_DROPE // 4
    partner = np.where((idx % (2 * half)) < half, idx + half, idx - half)
    return w[..., partner]


def _mla_pre(rs, x, mod, w_dq, q_norm, w_uq, w_dkv, kv_norm, w_ukv):
    d, qlora = w_dq.shape
    hd = A_DNOPE + LANES
    pad = LANES - A_DROPE
    wq = w_uq.reshape(qlora, A_HEADS, A_DNOPE + A_DROPE)
    wq_main = jnp.pad(wq, ((0, 0), (0, 0), (0, pad))).reshape(qlora, A_HEADS * hd)
    wq_part = jnp.pad(_rope_partner_cols(wq[..., A_DNOPE:]), ((0, 0), (0, 0), (0, pad))).reshape(qlora, A_HEADS * LANES)
    wuq_all = jnp.concatenate([wq_main, wq_part], axis=1).astype(MXU)
    w_kr = w_dkv[:, A_KVLORA:]
    wdkv_all = jnp.concatenate([w_dkv[:, :A_KVLORA], jnp.pad(w_kr, ((0, 0), (0, pad))),
                                jnp.pad(_rope_partner_cols(w_kr), ((0, 0), (0, pad)))], axis=1).astype(MXU)
    wkv = w_ukv.reshape(A_KVLORA, A_HEADS, A_DNOPE + A_DV)
    w_k = wkv[..., :A_DNOPE].reshape(A_KVLORA, A_HEADS * A_DNOPE).astype(MXU)
    w_v = wkv[..., A_DNOPE:].reshape(A_KVLORA, A_HEADS * A_DV).astype(MXU)
    cos, sin = _rope_tables(rs.seq, rs.tm)
    nl, tps = rs.n_lat_tiles, rs.tiles_per_seq
    tab = pl.BlockSpec((rs.tm, LANES), lambda i: (jnp.where(i >= nl, 0, 1 + i % tps), 0))
    n, r = rs.n_tiles, rs.rows
    vmem = 2 * 2 * (w_dq.size + wuq_all.size + wdkv_all.size + w_k.size + w_v.size) + 12 * rs.tm * A_HEADS * hd * 4
    return _row_call(
        functools.partial(_mla_pre_kernel, q_scale=float((A_DNOPE + A_DROPE) ** -0.5 * np.log2(np.e))), rs, n,
        [rs.tile(d), rs.mod(6 * d), tab, tab, _const_spec(w_dq.shape), _const_spec((1, qlora)), _const_spec(wuq_all.shape),
         _const_spec(wdkv_all.shape), _const_spec((1, A_KVLORA)), _const_spec(w_k.shape), _const_spec(w_v.shape)],
        [rs.tile(A_HEADS * hd), rs.tile(A_HEADS * hd), rs.tile(A_HEADS * A_DV)],
        [jax.ShapeDtypeStruct((r, A_HEADS * hd), MXU), jax.ShapeDtypeStruct((r, A_HEADS * hd), MXU),
         jax.ShapeDtypeStruct((r, A_HEADS * A_DV), MXU)],
        vmem=vmem, name="mla_pre",
    )(x, mod, cos, sin, w_dq.astype(MXU), q_norm.reshape(1, qlora), wuq_all, wdkv_all, kv_norm.reshape(1, A_KVLORA), w_k, w_v)


def _attn_kernel(*refs, with_latent, kv_chunk):
    if with_latent:
        q_ref, kc_ref, vc_ref, kl_ref, vl_ref, o_ref = refs
    else:
        q_ref, kc_ref, vc_ref, o_ref = refs
    hd, dv = A_DNOPE + LANES, A_DV
    heads = range(q_ref.shape[1] // hd)
    qs = lambda h: slice(h * hd, (h + 1) * hd)
    vs = lambda h: slice(h * dv, (h + 1) * dv)
    q = [q_ref[:, qs(h)] for h in heads]
    s = [_dot_nt(q[h], kc_ref[:, qs(h)]) for h in heads]
    m = [jnp.max(s[h], axis=1, keepdims=True) for h in heads]
    p = [jnp.exp2(s[h] - m[h]) for h in heads]
    l = [jnp.sum(p[h], axis=1, keepdims=True) for h in heads]
    acc = [_dot(p[h].astype(MXU), vc_ref[:, vs(h)]) for h in heads]
    if with_latent:
        for c0 in range(0, kl_ref.shape[0], kv_chunk):
            s = [_dot_nt(q[h], kl_ref[c0:c0 + kv_chunk, qs(h)]) for h in heads]
            m_new = [jnp.maximum(m[h], jnp.max(s[h], axis=1, keepdims=True)) for h in heads]
            corr = [jnp.exp2(m[h] - m_new[h]) for h in heads]
            p = [jnp.exp2(s[h] - m_new[h]) for h in heads]
            l = [l[h] * corr[h] + jnp.sum(p[h], axis=1, keepdims=True) for h in heads]
            acc = [acc[h] * corr[h] + _dot(p[h].astype(MXU), vl_ref[c0:c0 + kv_chunk, vs(h)]) for h in heads]
            m = m_new
    for h in heads:
        o_ref[:, vs(h)] = (acc[h] / l[h]).astype(o_ref.dtype)


def _mla_attention(rs, q, k, v):
    b, t = rs.batch, rs.seq
    hpb = 2
    hd, dv = hpb * (A_DNOPE + LANES), hpb * A_DV
    kv_chunk = min(1024, t)
    assert t % kv_chunk == 0 and A_HEADS % hpb == 0
    tc = rs.ctx_len
    ctx_blk0 = b * t // tc
    params = pltpu.CompilerParams(dimension_semantics=("arbitrary",) * 3, vmem_limit_bytes=_vmem_limit(48 * 2**20))
    o_ctx = pl.pallas_call(
        functools.partial(_attn_kernel, with_latent=False, kv_chunk=kv_chunk), grid=(b, A_HEADS // hpb, 1),
        in_specs=[pl.BlockSpec((tc, hd), lambda b_, h, i: (ctx_blk0 + b_, h)),
                  pl.BlockSpec((tc, hd), lambda b_, h, i: (ctx_blk0 + b_, h)),
                  pl.BlockSpec((tc, dv), lambda b_, h, i: (ctx_blk0 + b_, h))],
        out_specs=pl.BlockSpec((tc, dv), lambda b_, h, i: (b_, h)),
        out_shape=jax.ShapeDtypeStruct((b * tc, A_HEADS * A_DV), MXU), compiler_params=params, name="mla_attn_ctx",
    )(q, k, v)
    tq = ATTN_TQ
    nq = t // tq
    o_lat = pl.pallas_call(
        functools.partial(_attn_kernel, with_latent=True, kv_chunk=kv_chunk), grid=(b, A_HEADS // hpb, nq),
        in_specs=[pl.BlockSpec((tq, hd), lambda b_, h, i: (b_ * nq + i, h)),
                  pl.BlockSpec((tc, hd), lambda b_, h, i: (ctx_blk0 + b_, h)),
                  pl.BlockSpec((tc, dv), lambda b_, h, i: (ctx_blk0 + b_, h)),
                  pl.BlockSpec((t, hd), lambda b_, h, i: (b_, h)),
                  pl.BlockSpec((t, dv), lambda b_, h, i: (b_, h))],
        out_specs=pl.BlockSpec((tq, dv), lambda b_, h, i: (b_ * nq + i, h)),
        out_shape=jax.ShapeDtypeStruct((b * t, A_HEADS * A_DV), MXU), compiler_params=params, name="mla_attn_latent",
    )(q, k, v, k, v)
    return o_lat, o_ctx


def _proj_post_kernel(x_ref, al_ref, ac_ref, mod_ref, wout_ref, g_ref, b_ref, o_ref, *, alpha, n_lat_tiles):
    a = jnp.where(pl.program_id(0) >= n_lat_tiles, ac_ref[...], al_ref[...])
    _residual_ln(x_ref, mod_ref, a, wout_ref, g_ref, b_ref, o_ref, alpha)


def _proj_post(rs, n_tiles, x, a_lat, a_ctx, mod, w_out, ln_g, ln_b, alpha):
    kdim, d = w_out.shape
    nl = rs.n_lat_tiles
    body = functools.partial(_proj_post_kernel, alpha=alpha, n_lat_tiles=nl)
    return _row_call(
        body, rs, n_tiles,
        [rs.tile(d), pl.BlockSpec((rs.tm, kdim), lambda i: (jnp.minimum(i, nl - 1), 0)),
         pl.BlockSpec((rs.tm, kdim), lambda i: (jnp.maximum(i - nl, 0), 0)), rs.mod(6 * d), _const_spec(w_out.shape),
         _const_spec((1, d)), _const_spec((1, d))],
        rs.tile(d), jax.ShapeDtypeStruct((n_tiles * rs.tm, d), F32), vmem=32 * 2**20, name="proj_post",
    )(x, a_lat, a_ctx, mod, w_out, ln_g.reshape(1, d), ln_b.reshape(1, d))


def _gla_pre_kernel(x_ref, mod_ref, wqk_ref, wv_ref, wa1_ref, wa2_ref, ba_ref, q_ref, k_ref, v_ref, la_ref,
                    *, q_scale):
    d = x_ref.shape[1]
    half = q_ref.shape[1]
    chunks = [slice(r0, r0 + HEAD_ROWS) for r0 in range(0, x_ref.shape[0], HEAD_ROWS)]
    h = [_mod(x_ref[r, :], mod_ref, 0, d).astype(MXU) for r in chunks]
    a1 = [_dot(hc, wa1_ref[...]).astype(MXU) for hc in h]
    z = [_dot(a, wa2_ref[...]) + ba_ref[...] for a in a1]
    qk = [_dot(hc, wqk_ref[...]) for hc in h]
    for i, r in enumerate(chunks):
        la_ref[r, :] = _log_sigmoid(z[i]) * (LOG2E / G_TAU)
        q_ref[r, :] = qk[i][:, :half] * q_scale
        k_ref[r, :] = qk[i][:, half:]
    for i, r in enumerate(chunks):
        v_ref[r, :] = _dot(h[i], wv_ref[...]).astype(v_ref.dtype)


def _gla_pre(rs, x, mod, w_qk, w_v, w_a1, w_a2, b_a):
    d = w_qk.shape[0]
    dk_all = w_qk.shape[1] // 2
    wa1 = jnp.pad(jnp.concatenate([w_a1[0], w_a1[1]], axis=1), ((0, 0), (0, LANES - 2 * G_RANK))).astype(MXU)
    wa2 = jnp.zeros((LANES, 2 * dk_all), F32)
    wa2 = wa2.at[:G_RANK, :dk_all].set(w_a2[0]).at[G_RANK:2 * G_RANK, dk_all:].set(w_a2[1]).astype(MXU)
    ba = jnp.concatenate([b_a[0], b_a[1]]).reshape(1, 2 * dk_all)
    body = functools.partial(_gla_pre_kernel, q_scale=float((dk_all // G_HEADS) ** -0.5))
    n, r = rs.n_tiles, rs.rows
    return _row_call(
        body, rs, n,
        [rs.tile(d), rs.mod(6 * d), _const_spec(w_qk.shape), _const_spec(w_v.shape), _const_spec(wa1.shape),
         _const_spec(wa2.shape), _const_spec(ba.shape)],
        [rs.tile(dk_all), rs.tile(dk_all), rs.tile(w_v.shape[1]), rs.tile(2 * dk_all)],
        [jax.ShapeDtypeStruct((r, dk_all), F32), jax.ShapeDtypeStruct((r, dk_all), F32),
         jax.ShapeDtypeStruct((r, w_v.shape[1]), MXU), jax.ShapeDtypeStruct((r, 2 * dk_all), F32)],
        vmem=40 * 2**20, name="gla_pre",
    )(x, mod, w_qk, w_v, wa1, wa2, ba)


def _gla_tables(L):
    t = np.arange(L)
    tau, taup = t[:, None], t[None, :]
    groups = [taup <= tau, taup > tau]
    masks = []
    c = L // 2
    while c >= 1:
        blk = t // (2 * c)
        mid = blk * 2 * c + c
        second = (t % (2 * c)) >= c
        q_side = second[:, None] & (taup >= mid[:, None]) & (taup <= tau)
        k_side = (~second)[:, None] & (taup > tau) & (taup <= mid[:, None] - 1)
        groups.append(q_side | k_side)
        masks.append(second[:, None] & (~second)[None, :] & (blk[:, None] == blk[None, :]))
        c //= 2
    flip = lambda g: g[::-1, ::-1]
    sums = np.stack([np.concatenate(groups, axis=0), np.concatenate([flip(g) for g in groups], axis=0)])
    lvl = np.stack([np.stack(masks), np.stack([flip(m) for m in masks])])
    return sums.astype(np.float32), lvl.astype(np.float32)


def _gla_scan_kernel(q0_ref, k0_ref, v0_ref, la0_ref, q1_ref, k1_ref, v1_ref, la1_ref, sums_ref, masks_ref,
                     o0_ref, o1_ref, st_scr):
    @pl.when(pl.program_id(1) == 0)
    def _():
        st_scr[...] = jnp.zeros_like(st_scr)

    L = q0_ref.shape[0]
    dk, dv = q0_ref.shape[1] // G_HEADS, v0_ref.shape[1] // G_HEADS
    levels = masks_ref.shape[1]
    dirs = ((q0_ref, k0_ref, v0_ref, la0_ref, o0_ref), (q1_ref, k1_ref, v1_ref, la1_ref, o1_ref))
    chains = [(d, h) for d in range(2) for h in range(G_HEADS)]
    row = lax.broadcasted_iota(jnp.int32, (L, L), 0)
    col = lax.broadcasted_iota(jnp.int32, (L, L), 1)
    eye = row == col

    st_prev = {(d, h): st_scr[d, h] for d, h in chains}
    e = [jnp.exp2(_dot(sums_ref[d], jnp.concatenate(_split2(dirs[d][3][...]), axis=0))) for d in range(2)]
    q = [dirs[d][0][...] for d in range(2)]
    k = [dirs[d][1][...] for d in range(2)]
    qe = [(q[d] * e[d][0:L]).astype(MXU) for d in range(2)]
    kd = [(k[d] * e[d][L:2 * L]).astype(MXU) for d in range(2)]
    qk_diag = [q[d] * k[d] for d in range(2)]
    qt = [[(q[d] * e[d][(2 + lv) * L:(3 + lv) * L]).astype(MXU) for lv in range(levels)] for d in range(2)]
    kt = [[(k[d] * e[d][(2 + lv) * L:(3 + lv) * L]).astype(MXU) for lv in range(levels)] for d in range(2)]
    decay = [jnp.exp2(jnp.sum(dirs[d][3][...], axis=0, keepdims=True)) for d in range(2)]

    att = {}
    for d, h in chains:
        ks = slice(h * dk, (h + 1) * dk)
        a = jnp.where(eye, jnp.sum(qk_diag[d][:, ks], axis=1, keepdims=True), 0.0)
        for lv in range(levels):
            a = a + masks_ref[d, lv] * _dot_nt(qt[d][lv][:, ks], kt[d][lv][:, ks])
        att[d, h] = a.astype(MXU)
    for d, h in chains:
        ks, vs = slice(h * dk, (h + 1) * dk), slice(h * dv, (h + 1) * dv)
        v = dirs[d][2][:, vs]
        dirs[d][4][:, vs] = _dot_nt(qe[d][:, ks], st_prev[d, h].astype(MXU)) + _dot(att[d, h], v)
        st_scr[d, h] = st_prev[d, h] * decay[d][:, ks] + _dot_tn(v, kd[d][:, ks])


def _gla_scan(rs, q, k, v, la):
    b, L = rs.batch, G_CHUNK
    dk_all, dv_all = q.shape[1], v.shape[1]
    nc_ctx, nc_lat = rs.ctx_len // L, rs.seq // L
    rowblk = _scan_rowblock(b, nc_ctx, nc_lat)
    sums, masks = _gla_tables(L)
    sums, masks = jnp.asarray(np.concatenate([sums, sums], axis=2), MXU), jnp.asarray(masks, F32)

    def specs(d):
        rb = lambda b_, c: (rowblk(b_, d, c), 0)
        return [pl.BlockSpec((L, dk_all), rb), pl.BlockSpec((L, dk_all), rb), pl.BlockSpec((L, dv_all), rb),
                pl.BlockSpec((L, dk_all), lambda b_, c: (rowblk(b_, d, c), d))]

    out_shape = jax.ShapeDtypeStruct((rs.rows, dv_all), F32)
    return pl.pallas_call(
        _gla_scan_kernel, grid=(b, nc_ctx + nc_lat),
        in_specs=specs(0) + specs(1) + [_const_spec(sums.shape), _const_spec(masks.shape)],
        out_specs=[pl.BlockSpec((L, dv_all), lambda b_, c: (rowblk(b_, 0, c), 0)),
                   pl.BlockSpec((L, dv_all), lambda b_, c: (rowblk(b_, 1, c), 0))],
        out_shape=[out_shape, out_shape],
        scratch_shapes=[pltpu.VMEM((2, G_HEADS, dv_all // G_HEADS, dk_all // G_HEADS), F32)],
        compiler_params=pltpu.CompilerParams(dimension_semantics=("arbitrary",) * 2,
                                             vmem_limit_bytes=_vmem_limit(32 * 2**20)),
        name="gla_scan",
    )(q, k, v, la, q, k, v, la, sums, masks)


def _gla_post_kernel(x_ref, s0_ref, s1_ref, mod_ref, wr_ref, ng_ref, wout_ref, g_ref, b_ref, o_ref, *, alpha):
    d = x_ref.shape[1]
    chunks = [slice(r0, r0 + HEAD_ROWS) for r0 in range(0, x_ref.shape[0], HEAD_ROWS)]
    h = [_mod(x_ref[r, :], mod_ref, 0, d).astype(MXU) for r in chunks]
    gate = [_silu(_dot(hc, wr_ref[...])) for hc in h]
    a = [(_head_rms(s0_ref[r, :] + s1_ref[r, :], ng_ref[...], G_HEADS) * gate[i]).astype(MXU) for i, r in enumerate(chunks)]
    _residual_ln(x_ref, mod_ref, jnp.concatenate(a, axis=0), wout_ref, g_ref, b_ref, o_ref, alpha)


def _gla_post(rs, n_tiles, x, s, mod, w_r, norm_g, w_out, ln_g, ln_b, alpha):
    d, dv_all = w_r.shape
    body = functools.partial(_gla_post_kernel, alpha=alpha)
    return _row_call(
        body, rs, n_tiles,
        [rs.tile(d), rs.tile(dv_all), rs.tile(dv_all), rs.mod(6 * d), _const_spec(w_r.shape), _const_spec((1, dv_all)), _const_spec(w_out.shape),
         _const_spec((1, d)), _const_spec((1, d))],
        rs.tile(d), jax.ShapeDtypeStruct((n_tiles * rs.tm, d), F32), vmem=40 * 2**20, name="gla_post",
    )(x, s[0], s[1], mod, w_r, norm_g.reshape(1, dv_all), w_out, ln_g.reshape(1, d), ln_b.reshape(1, d))


def kernel(x, c, ctx, c_ctx, ada_w, ada_b, ln_g, ln_b, ffn_w_in, ffn_conv_w, ffn_conv_b, ffn_w_out, m_w_up, m_conv_w, m_conv_b, m_w_qk, m_w_v, m_w_gates, m_b_gates, m_w_og, m_norm_g, m_w_out, a_w_dq, a_q_norm, a_w_uq, a_w_dkv, a_kv_norm, a_w_ukv, a_w_out, g_w_qk, g_w_v, g_w_r, g_w_a1, g_w_a2, g_b_a, g_norm_g, g_w_out):
    batch, seq, d = x.shape
    depth = ada_w.shape[0]
    n_mixers = 3
    alpha = float((2 * depth) ** 0.25)
    rs = _Rows(batch, seq, ctx.shape[1], TM)
    rs_big = _Rows(batch, seq, ctx.shape[1], BIG_TM)
    bf = lambda w: w.astype(MXU)

    cond_rows = -(-(batch + 1) // SUBLANES) * SUBLANES
    cond = jnp.zeros((cond_rows, d), F32).at[:batch].set(c).at[batch].set(c_ctx)
    mods = _modulation(cond, ada_w, ada_b).reshape(depth, cond_rows, 1, 6 * d)

    x_lat, x_ctx = x.reshape(batch * seq, d), ctx.reshape(-1, d)
    xa = None
    for i in range(depth):
        need_ctx = i < depth - 1
        n_tiles = rs.n_tiles if need_ctx else rs.n_lat_tiles
        n_big = rs_big.n_tiles if need_ctx else rs_big.n_lat_tiles
        kind, j = i % n_mixers, i // n_mixers
        mod = mods[i]
        if kind == 0:
            src = (x_lat, x_ctx, 0) if xa is None else (xa, xa, rs.n_lat_tiles)
            q, k, v, gates = _mlstm_pre(rs, *src, mod, bf(m_w_up[j]), m_conv_w[j], m_conv_b[j], bf(m_w_qk[j]),
                                        _weight_product(m_w_up[j], m_w_v[j]), m_w_gates[j], m_b_gates[j])
            s = _mlstm_scan(rs, q, k, v, gates)
            xa = _mlstm_post(rs, n_tiles, *src, s, mod, bf(m_w_og[j]), m_norm_g[j], bf(m_w_out[j]), ln_g[i, 0],
                             ln_b[i, 0], alpha)
        elif xa is None:
            xa = jnp.concatenate([x_lat, x_ctx], axis=0)
        if kind == 1:
            q, k, v = _mla_pre(rs_big, xa, mod, a_w_dq[j], a_q_norm[j], a_w_uq[j], a_w_dkv[j], a_kv_norm[j], a_w_ukv[j])
            o_lat, o_ctx = _mla_attention(rs_big, q, k, v)
            xa = _proj_post(rs_big, n_big, xa, o_lat, o_ctx, mod, bf(a_w_out[j]), ln_g[i, 0], ln_b[i, 0], alpha)
        elif kind == 2:
            q, k, v, la = _gla_pre(rs_big, xa, mod, bf(g_w_qk[j]), bf(g_w_v[j]), g_w_a1[j], g_w_a2[j], g_b_a[j])
            s = _gla_scan(rs_big, q, k, v, la)
            xa = _gla_post(rs_big, n_big, xa, s, mod, bf(g_w_r[j]), g_norm_g[j], bf(g_w_out[j]), ln_g[i, 0], ln_b[i, 0], alpha)
        xa = _ffn(rs_big, n_big, xa, mod, bf(ffn_w_in[i]), ffn_conv_w[i], ffn_conv_b[i], bf(ffn_w_out[i]),
                  ln_g[i, 1], ln_b[i, 1], alpha)
    return xa.reshape(batch, seq, d)
```

```python
import functools

import numpy as np
import jax
import jax.numpy as jnp
from jax import lax
from jax.experimental import pallas as pl
from jax.experimental.pallas import tpu as pltpu

F32 = jnp.float32
MXU = jnp.bfloat16

V7X_VMEM_BYTES = 64 * 2**20
SUBLANES = 8
LANES = 128

TM = 512
BIG_TM = 1024
ATTN_TQ = 1024
HEAD_ROWS = 256
TAIL_ROWS = 128
HALO = SUBLANES
GRID_W = 64
EPS = 1e-6
ROPE_BASE = 10000.0
G_TAU = 16.0
LOG2E = float(np.log2(np.e))

M_HEADS, A_HEADS, G_HEADS = 4, 8, 4
A_DNOPE, A_DROPE, A_DV, A_KVLORA = 128, 64, 128, 256
G_RANK = 16
M_CHUNK = 256
M_SCAN_GROUP = 8
G_CHUNK = 128


def _vmem_limit(nbytes):
    return int(min(max(nbytes, 16 * 2**20), V7X_VMEM_BYTES - 8 * 2**20))


def _const_spec(shape):
    nd = len(shape)
    return pl.BlockSpec(shape, lambda *_: (0,) * nd, pipeline_mode=pl.Buffered(1))


def _dot(a, b):
    return jnp.dot(a, b, preferred_element_type=F32)


def _dot_nt(a, b):
    return lax.dot_general(a, b, (((1,), (1,)), ((), ())), preferred_element_type=F32)


def _dot_tn(a, b):
    return lax.dot_general(a, b, (((0,), (0,)), ((), ())), preferred_element_type=F32)


def _split3(x):
    hi = x.astype(MXU)
    r1 = x - hi.astype(F32)
    mid = r1.astype(MXU)
    lo = (r1 - mid.astype(F32)).astype(MXU)
    return hi, mid, lo


def _split2(x):
    hi = x.astype(MXU)
    return hi, (x - hi.astype(F32)).astype(MXU)


def _sigmoid(x):
    return 1.0 / (1.0 + jnp.exp(-x))


def _silu(x):
    return x * _sigmoid(x)


def _log_sigmoid(x):
    return jnp.minimum(x, 0.0) - jnp.log1p(jnp.exp(-jnp.abs(x)))


def _layer_norm(z, g, b):
    mu = jnp.mean(z, -1, keepdims=True)
    zc = z - mu
    var = jnp.mean(zc * zc, -1, keepdims=True)
    return zc * lax.rsqrt(var + EPS) * g + b


def _rms(x, g):
    return x * lax.rsqrt(jnp.mean(x * x, -1, keepdims=True) + EPS) * g


def _head_rms(x, g, heads):
    d = x.shape[-1] // heads
    return jnp.concatenate([_rms(x[:, h * d:(h + 1) * d], g[:, h * d:(h + 1) * d]) for h in range(heads)], axis=-1)


def _mod(x, mod_ref, k, d):
    return x * (1.0 + mod_ref[:, (k + 1) * d:(k + 2) * d]) + mod_ref[:, k * d:(k + 1) * d]


def _halo_rows(xp_ref, x, xn_ref, mod_ref, k, d, geom):
    n_lat_tiles, tiles_per_seq, _, _ = geom
    i = pl.program_id(0)
    is_ctx = i >= n_lat_tiles
    pos = i % tiles_per_seq
    first = jnp.logical_or(is_ctx, pos == 0)
    last = jnp.logical_or(is_ctx, pos == tiles_per_seq - 1)
    hp = jnp.where(first, 0.0, _mod(xp_ref[...], mod_ref, k, d))
    hn = jnp.where(last, 0.0, _mod(xn_ref[...], mod_ref, k, d))
    return jnp.concatenate([hp, _mod(x, mod_ref, k, d), hn], axis=0)


def _tile_rows(xl_ref, xc_ref, geom):
    return jnp.where(pl.program_id(0) >= geom[0], xc_ref[...], xl_ref[...])


def _seq_edges(geom):
    n_lat_tiles, _, ctx_len, tm = geom
    is_ctx = pl.program_id(0) >= n_lat_tiles
    r = lax.broadcasted_iota(jnp.int32, (tm, 1), 0) % ctx_len
    return jnp.logical_and(is_ctx, r == 0), jnp.logical_and(is_ctx, r == ctx_len - 1)


def _dwconv3(g_ext, w_ref, b_ref, c0, c1, edges):
    n = g_ext.shape[0] - 2 * HALO
    starts, ends = edges
    prev = jnp.where(starts, 0.0, g_ext[HALO - 1:HALO - 1 + n])
    nxt = jnp.where(ends, 0.0, g_ext[HALO + 1:HALO + 1 + n])
    return (w_ref[0:1, c0:c1] * prev + w_ref[1:2, c0:c1] * g_ext[HALO:HALO + n] + w_ref[2:3, c0:c1] * nxt
            + b_ref[:, c0:c1])


class _Rows:
    def __init__(self, batch, seq, ctx_len, tm):
        assert tm % ctx_len == 0 and (batch * ctx_len) % tm == 0 and seq % tm == 0
        self.batch, self.seq, self.ctx_len, self.tm = batch, seq, ctx_len, tm
        self.tiles_per_seq = seq // tm
        self.n_lat_tiles = batch * self.tiles_per_seq
        self.n_tiles = self.n_lat_tiles + batch * ctx_len // tm
        self.rows = self.n_tiles * tm
        self.geom = (self.n_lat_tiles, self.tiles_per_seq, ctx_len, tm)

    def tile(self, width):
        return pl.BlockSpec((self.tm, width), lambda i: (i, 0))

    def tile_pair(self, width, ctx_blk0):
        nl = self.n_lat_tiles
        return [pl.BlockSpec((self.tm, width), lambda i: (jnp.minimum(i, nl - 1), 0)),
                pl.BlockSpec((self.tm, width), lambda i: (ctx_blk0 + jnp.maximum(i - nl, 0), 0))]

    def halo_prev(self, width, n_rows):
        per, nblk = self.tm // HALO, n_rows // HALO
        return pl.BlockSpec((HALO, width), lambda i: (jnp.clip(i * per - 1, 0, nblk - 1), 0))

    def halo_next(self, width, n_rows):
        per, nblk = self.tm // HALO, n_rows // HALO
        return pl.BlockSpec((HALO, width), lambda i: (jnp.minimum((i + 1) * per, nblk - 1), 0))

    def mod(self, width):
        nl, tps, b = self.n_lat_tiles, self.tiles_per_seq, self.batch
        return pl.BlockSpec((None, 1, width), lambda i: (jnp.where(i >= nl, b, i // tps), 0, 0))


def _row_call(body, rows, n_tiles, in_specs, out_specs, out_shape, scratch=(), vmem=0, name=None):
    return pl.pallas_call(
        body, grid=(n_tiles,), in_specs=in_specs, out_specs=out_specs, out_shape=out_shape,
        scratch_shapes=list(scratch), name=name,
        compiler_params=pltpu.CompilerParams(dimension_semantics=("arbitrary",), vmem_limit_bytes=_vmem_limit(vmem)))


def _modulation_kernel(c_ref, w_ref, b_ref, o_ref):
    o_ref[...] = _dot(_silu(c_ref[...]).astype(MXU), w_ref[...].astype(MXU)) + b_ref[...]


def _modulation(cond, ada_w, ada_b):
    depth, d, n = ada_w.shape
    tn = d
    return pl.pallas_call(
        _modulation_kernel, grid=(depth, n // tn),
        in_specs=[pl.BlockSpec(cond.shape, lambda l, j: (0, 0)),
                  pl.BlockSpec((None, d, tn), lambda l, j: (l, 0, j)),
                  pl.BlockSpec((None, 1, tn), lambda l, j: (l, 0, j))],
        out_specs=pl.BlockSpec((None, cond.shape[0], tn), lambda l, j: (l, 0, j)),
        out_shape=jax.ShapeDtypeStruct((depth, cond.shape[0], n), F32), name="modulation",
    )(cond, ada_w, ada_b.reshape(depth, 1, n))


def _ffn_kernel(xp_ref, x_ref, xn_ref, mod_ref, win_ref, cw_ref, cb_ref, wout_ref, g_ref, b_ref, o_ref, a_scr,
                *, geom, alpha, ffn, tf):
    tm, d = x_ref.shape
    hext = _halo_rows(xp_ref, x_ref[...], xn_ref, mod_ref, 3, d, geom).astype(MXU)
    edges = _seq_edges(geom)
    for f0 in range(0, ffn, tf):
        g_ext = _dot(hext, win_ref[:, f0:f0 + tf])
        up = _dot(hext, win_ref[:, ffn + f0:ffn + f0 + tf])[HALO:HALO + tm]
        a_scr[:, f0:f0 + tf] = (_silu(_dwconv3(g_ext, cw_ref, cb_ref, f0, f0 + tf, edges)) * up).astype(MXU)
    for r0 in range(0, tm, TAIL_ROWS):
        rows = slice(r0, r0 + TAIL_ROWS)
        f = _dot(a_scr[rows, :], wout_ref[...])
        z = alpha * x_ref[rows, :] + mod_ref[:, 5 * d:6 * d] * f
        o_ref[rows, :] = _layer_norm(z, g_ref[...], b_ref[...])


def _ffn(rs, n_tiles, x, mod, w_in, conv_w, conv_b, w_out, ln_g, ln_b, alpha):
    d, ffn = w_out.shape[1], w_out.shape[0]
    tf = 256
    assert ffn % tf == 0
    body = functools.partial(_ffn_kernel, geom=rs.geom, alpha=alpha, ffn=ffn, tf=tf)
    vmem = 2 * (w_in.size + w_out.size) + 6 * rs.tm * d * 4 + rs.tm * ffn * 2 + 24 * 2**20
    return _row_call(
        body, rs, n_tiles,
        [rs.halo_prev(d, x.shape[0]), rs.tile(d), rs.halo_next(d, x.shape[0]), rs.mod(6 * d), _const_spec(w_in.shape), _const_spec(conv_w.shape),
         _const_spec((1, ffn)), _const_spec(w_out.shape), _const_spec((1, d)), _const_spec((1, d))],
        rs.tile(d), jax.ShapeDtypeStruct((n_tiles * rs.tm, d), F32),
        scratch=[pltpu.VMEM((rs.tm, ffn), MXU)], vmem=vmem, name="conv_ffn",
    )(x, x, x, mod, w_in, conv_w, conv_b.reshape(1, ffn), w_out, ln_g.reshape(1, d), ln_b.reshape(1, d))


def _residual_ln(x_ref, mod_ref, a, wout_ref, g_ref, b_ref, o_ref, alpha):
    tm, d = x_ref.shape
    for r0 in range(0, tm, TAIL_ROWS):
        rows = slice(r0, r0 + TAIL_ROWS)
        z = alpha * x_ref[rows, :] + mod_ref[:, 2 * d:3 * d] * _dot(a[rows, :], wout_ref[...])
        o_ref[rows, :] = _layer_norm(z, g_ref[...], b_ref[...])


def _weight_product_kernel(a_ref, b_ref, o_ref):
    o_ref[...] = _dot(a_ref[...].astype(MXU), b_ref[...].astype(MXU)).astype(o_ref.dtype)


def _weight_product(a, b):
    m, kdim = a.shape
    n = b.shape[1]
    tn = 512
    assert n % tn == 0
    return pl.pallas_call(
        _weight_product_kernel, grid=(n // tn,),
        in_specs=[pl.BlockSpec((m, kdim), lambda j: (0, 0)), pl.BlockSpec((kdim, tn), lambda j: (0, j))],
        out_specs=pl.BlockSpec((m, tn), lambda j: (0, j)),
        out_shape=jax.ShapeDtypeStruct((m, n), MXU),
        compiler_params=pltpu.CompilerParams(dimension_semantics=("arbitrary",), vmem_limit_bytes=_vmem_limit(40 * 2**20)),
        name="weight_product",
    )(a, b)


def _mlstm_pre_kernel(xp_ref, xl_ref, xc_ref, xn_ref, mod_ref, wup_ref, cw_ref, cb_ref, wqk_ref, wv_ref, wg_ref, bg_ref,
                      q_ref, k_ref, v_ref, gates_ref, *, geom, k_scale):
    d = xl_ref.shape[1]
    x = _tile_rows(xl_ref, xc_ref, geom)
    hext = _halo_rows(xp_ref, x, xn_ref, mod_ref, 0, d, geom).astype(MXU)
    xm_ext = _dot(hext, wup_ref[...])
    inner = xm_ext.shape[1]
    xc = _silu(_dwconv3(xm_ext, cw_ref, cb_ref, 0, inner, _seq_edges(geom))).astype(MXU)
    qk = _dot(xc, wqk_ref[...])
    half = qk.shape[1] // 2
    q_ref[...] = qk[:, :half].astype(q_ref.dtype)
    k_ref[...] = (qk[:, half:] * k_scale).astype(k_ref.dtype)
    v_ref[...] = _dot(_mod(x, mod_ref, 0, d).astype(MXU), wv_ref[...]).astype(v_ref.dtype)
    two_h = 2 * M_HEADS
    gates = _dot(xc, wg_ref[...]) + bg_ref[...]
    lane = lax.broadcasted_iota(jnp.int32, gates.shape, 1)
    gates_ref[...] = jnp.where(lane % two_h >= M_HEADS, _log_sigmoid(gates), gates) * LOG2E


def _mlstm_pre(rs, x_lat, x_ctx, ctx_blk0, mod, w_up, conv_w, conv_b, w_qk, w_v, w_gates, b_gates):
    d, inner = w_up.shape
    dqk_all = w_qk.shape[1] // 2
    n_g = 2 * 2 * M_HEADS
    wg = jnp.concatenate([w_gates[0], w_gates[1]], axis=1)
    wg_pad = jnp.pad(wg, ((0, 0), (0, LANES - n_g))).astype(MXU)
    bg = jnp.concatenate([b_gates[0], b_gates[1]])
    bg_pad = jnp.pad(bg, (0, LANES - n_g)).reshape(1, LANES)
    body = functools.partial(_mlstm_pre_kernel, geom=rs.geom, k_scale=float((dqk_all // M_HEADS) ** -0.5))
    n, r = rs.n_tiles, rs.rows
    vmem = 2 * (w_up.size + w_qk.size + w_v.size) + 8 * (TM + 2 * HALO) * inner * 4 + 16 * 2**20
    return _row_call(
        body, rs, n,
        [rs.halo_prev(d, x_lat.shape[0])] + rs.tile_pair(d, ctx_blk0) + [rs.halo_next(d, x_lat.shape[0]), rs.mod(6 * d),
         _const_spec(w_up.shape), _const_spec(conv_w.shape), _const_spec((1, inner)), _const_spec(w_qk.shape),
         _const_spec(w_v.shape), _const_spec((inner, LANES)), _const_spec((1, LANES))],
        [rs.tile(dqk_all), rs.tile(dqk_all), rs.tile(inner), rs.tile(LANES)],
        [jax.ShapeDtypeStruct((r, dqk_all), MXU), jax.ShapeDtypeStruct((r, dqk_all), MXU),
         jax.ShapeDtypeStruct((r, inner), MXU), jax.ShapeDtypeStruct((r, LANES), F32)],
        vmem=vmem, name="mlstm_pre",
    )(x_lat, x_lat, x_ctx, x_lat, mod, w_up, conv_w, conv_b.reshape(1, inner), w_qk, w_v, wg_pad, bg_pad)


def _scan_rowblock(batch, nc_ctx, nc_lat):
    def rowblk(b, d, c):
        cc = jnp.where(d == 1, nc_ctx - 1 - c, c)
        lc = c - nc_ctx
        lc = jnp.where(d == 1, nc_lat - 1 - lc, lc)
        return jnp.where(c < nc_ctx, batch * nc_lat + b * nc_ctx + cc, b * nc_lat + lc)
    return rowblk


def _mlstm_scan_kernel(q0_ref, k0_ref, v0_ref, g0_ref, q1_ref, k1_ref, v1_ref, g1_ref,
                       o0_ref, o1_ref, c_scr, n_scr, m_scr):
    @pl.when(pl.program_id(1) == 0)
    def _():
        c_scr[...] = jnp.zeros_like(c_scr)
        n_scr[...] = jnp.zeros_like(n_scr)
        m_scr[...] = jnp.zeros_like(m_scr)

    L = q0_ref.shape[0]
    dqk, dv = q0_ref.shape[1] // M_HEADS, v0_ref.shape[1] // M_HEADS
    two_h = 2 * M_HEADS
    dirs = ((q0_ref, k0_ref, v0_ref, g0_ref, o0_ref), (q1_ref, k1_ref, v1_ref, g1_ref, o1_ref))
    chains = [(d, h) for d in range(2) for h in range(M_HEADS)]
    row = lax.broadcasted_iota(jnp.int32, (L, L), 0)
    col = lax.broadcasted_iota(jnp.int32, (L, L), 1)
    causal = [col <= row, col >= row]
    causal_t = [row <= col, row >= col]
    eye = jnp.where(row == col, 1.0, 0.0).astype(MXU)
    gates = [dirs[d][3][...] for d in range(2)]
    pieces = [_split3(gates[d]) for d in range(2)]
    cs = [sum(_dot(jnp.where(causal[d], 1.0, 0.0).astype(MXU), p) for p in pieces[d]) for d in range(2)]
    cs_t = [sum(_dot_tn(p, jnp.where(causal_t[d], 1.0, 0.0).astype(MXU)) for p in pieces[d]) for d in range(2)]
    gates_t = [sum(_dot_tn(p, eye) for p in pieces[d]) for d in range(2)]

    i_slot = lambda d, h: d * two_h + h
    f_slot = lambda d, h: d * two_h + M_HEADS + h
    qs = lambda h: slice(h * dqk, (h + 1) * dqk)
    vs = lambda h: slice(h * dv, (h + 1) * dv)
    for g0 in range(0, len(chains), M_SCAN_GROUP):
        _mlstm_advance(chains[g0:g0 + M_SCAN_GROUP], dirs, causal, gates, gates_t, cs, cs_t, i_slot, f_slot, qs, vs,
                       c_scr, n_scr, m_scr)


def _mlstm_advance(chains, dirs, causal, gates, gates_t, cs, cs_t, i_slot, f_slot, qs, vs, c_scr, n_scr, m_scr):
    def per_chain(fn):
        return {ch: fn(*ch) for ch in chains}

    c_prev = per_chain(lambda d, h: c_scr[d, h])
    n_prev = per_chain(lambda d, h: n_scr[d, h])
    m_prev = per_chain(lambda d, h: m_scr[d, h])
    q = per_chain(lambda d, h: dirs[d][0][:, qs(h)])
    k = per_chain(lambda d, h: dirs[d][1][:, qs(h)])
    v = per_chain(lambda d, h: dirs[d][2][:, vs(h)])
    li_col = per_chain(lambda d, h: gates[d][:, i_slot(d, h):i_slot(d, h) + 1])
    li_row = per_chain(lambda d, h: gates_t[d][i_slot(d, h):i_slot(d, h) + 1, :])
    b_col = per_chain(lambda d, h: cs[d][:, f_slot(d, h):f_slot(d, h) + 1])
    b_row = per_chain(lambda d, h: cs_t[d][f_slot(d, h):f_slot(d, h) + 1, :])
    bl = per_chain(lambda d, h: jnp.sum(gates_t[d][f_slot(d, h):f_slot(d, h) + 1, :], axis=1, keepdims=True))

    dmat = per_chain(lambda d, h: jnp.where(causal[d], b_col[d, h] - b_row[d, h] + li_row[d, h], -jnp.inf))
    inter = per_chain(lambda d, h: b_col[d, h] + m_prev[d, h])
    mj = per_chain(lambda d, h: jnp.maximum(inter[d, h], jnp.max(dmat[d, h], axis=1, keepdims=True)))
    qk = per_chain(lambda d, h: _dot_nt(q[d, h], k[d, h]))
    wmat = per_chain(lambda d, h: jnp.exp2(dmat[d, h] - mj[d, h]) * qk[d, h])
    g = per_chain(lambda d, h: jnp.exp2(inter[d, h] - mj[d, h]))
    qc = per_chain(lambda d, h: _dot(q[d, h], c_prev[d, h].astype(MXU)))
    wv = per_chain(lambda d, h: _dot(wmat[d, h].astype(MXU), v[d, h]))
    qn = per_chain(lambda d, h: jnp.sum(q[d, h].astype(F32) * n_prev[d, h], axis=1, keepdims=True))
    den = per_chain(lambda d, h: g[d, h] * qn[d, h] + jnp.sum(wmat[d, h], axis=1, keepdims=True))
    for d, h in chains:
        num = g[d, h] * qc[d, h] + wv[d, h]
        dirs[d][4][:, vs(h)] = num / jnp.maximum(jnp.abs(den[d, h]), jnp.exp2(-mj[d, h]))

    ds = per_chain(lambda d, h: bl[d, h] - b_col[d, h] + li_col[d, h])
    m_new = per_chain(lambda d, h: jnp.maximum(bl[d, h] + m_prev[d, h], jnp.max(ds[d, h], axis=0, keepdims=True)))
    kw = per_chain(lambda d, h: k[d, h].astype(F32) * jnp.exp2(ds[d, h] - m_new[d, h]))
    decay = per_chain(lambda d, h: jnp.exp2(bl[d, h] + m_prev[d, h] - m_new[d, h]))
    kv = per_chain(lambda d, h: _dot_tn(kw[d, h].astype(MXU), v[d, h]))
    for d, h in chains:
        c_scr[d, h] = decay[d, h] * c_prev[d, h] + kv[d, h]
        n_scr[d, h] = decay[d, h] * n_prev[d, h] + jnp.sum(kw[d, h], axis=0, keepdims=True)
        m_scr[d, h] = m_new[d, h]


def _mlstm_scan(rs, q, k, v, gates):
    b, L = rs.batch, M_CHUNK
    dqk_all, dv_all = q.shape[1], v.shape[1]
    nc_ctx, nc_lat = rs.ctx_len // L, rs.seq // L
    rowblk = _scan_rowblock(b, nc_ctx, nc_lat)

    def specs(d):
        rb = lambda b_, c: (rowblk(b_, d, c), 0)
        return [pl.BlockSpec((L, dqk_all), rb), pl.BlockSpec((L, dqk_all), rb), pl.BlockSpec((L, dv_all), rb),
                pl.BlockSpec((L, LANES), rb)]

    out_shape = jax.ShapeDtypeStruct((rs.rows, dv_all), F32)
    dqk, dv = dqk_all // M_HEADS, dv_all // M_HEADS
    return pl.pallas_call(
        _mlstm_scan_kernel, grid=(b, nc_ctx + nc_lat),
        in_specs=specs(0) + specs(1),
        out_specs=[pl.BlockSpec((L, dv_all), lambda b_, c: (rowblk(b_, 0, c), 0)),
                   pl.BlockSpec((L, dv_all), lambda b_, c: (rowblk(b_, 1, c), 0))],
        out_shape=[out_shape, out_shape],
        scratch_shapes=[pltpu.VMEM((2, M_HEADS, dqk, dv), F32), pltpu.VMEM((2, M_HEADS, 1, dqk), F32),
                        pltpu.VMEM((2, M_HEADS, 1, 1), F32)],
        compiler_params=pltpu.CompilerParams(dimension_semantics=("arbitrary",) * 2,
                                             vmem_limit_bytes=_vmem_limit(40 * 2**20)),
        name="mlstm_scan",
    )(q, k, v, gates, q, k, v, gates)


def _mlstm_post_kernel(xl_ref, xc_ref, s0_ref, s1_ref, mod_ref, wog_ref, ng_ref, wout_ref, g_ref, b_ref, o_ref,
                       *, alpha, geom):
    d = xl_ref.shape[1]
    x = _tile_rows(xl_ref, xc_ref, geom)
    chunks = [slice(r0, r0 + HEAD_ROWS) for r0 in range(0, x.shape[0], HEAD_ROWS)]
    h = [_mod(x[r, :], mod_ref, 0, d).astype(MXU) for r in chunks]
    og = [_sigmoid(_dot(hc, wog_ref[...])) for hc in h]
    a = [_head_rms(og[i] * (s0_ref[r, :] + s1_ref[r, :]), ng_ref[...], M_HEADS).astype(MXU) for i, r in enumerate(chunks)]
    _residual_ln(x, mod_ref, jnp.concatenate(a, axis=0), wout_ref, g_ref, b_ref, o_ref, alpha)


def _mlstm_post(rs, n_tiles, x_lat, x_ctx, ctx_blk0, s, mod, w_og, norm_g, w_out, ln_g, ln_b, alpha):
    d, inner = w_og.shape
    body = functools.partial(_mlstm_post_kernel, alpha=alpha, geom=rs.geom)
    vmem = 2 * (w_og.size + w_out.size) + 10 * TM * inner * 4 + 16 * 2**20
    return _row_call(
        body, rs, n_tiles,
        rs.tile_pair(d, ctx_blk0) + [rs.tile(inner), rs.tile(inner), rs.mod(6 * d), _const_spec(w_og.shape),
                                     _const_spec((1, inner)), _const_spec(w_out.shape), _const_spec((1, d)), _const_spec((1, d))],
        rs.tile(d), jax.ShapeDtypeStruct((n_tiles * rs.tm, d), F32), vmem=vmem, name="mlstm_post",
    )(x_lat, x_ctx, s[0], s[1], mod, w_og, norm_g.reshape(1, inner), w_out, ln_g.reshape(1, d), ln_b.reshape(1, d))


def _mla_pre_kernel(x_ref, mod_ref, cos_ref, sin_ref, wdq_ref, qn_ref, wuq_ref, wdkv_ref, kvn_ref, wk_ref, wv_ref,
                    q_ref, k_ref, v_ref, *, q_scale):
    d = x_ref.shape[1]
    hd = A_DNOPE + LANES
    h = _mod(x_ref[...], mod_ref, 0, d).astype(MXU)
    cos, sin = cos_ref[...], sin_ref[...]
    cq = _rms(_dot(h, wdq_ref[...]), qn_ref[...]).astype(MXU)
    qa = _dot(cq, wuq_ref[...])
    part0 = A_HEADS * hd
    cos_q, sin_q = cos * q_scale, sin * q_scale
    for hh in range(A_HEADS):
        q_ref[:, hh * hd:hh * hd + A_DNOPE] = (qa[:, hh * hd:hh * hd + A_DNOPE] * q_scale).astype(q_ref.dtype)
        rot = qa[:, hh * hd + A_DNOPE:(hh + 1) * hd] * cos_q + qa[:, part0 + hh * LANES:part0 + (hh + 1) * LANES] * sin_q
        q_ref[:, hh * hd + A_DNOPE:(hh + 1) * hd] = rot.astype(q_ref.dtype)
    dk = _dot(h, wdkv_ref[...])
    k_rope = (dk[:, A_KVLORA:A_KVLORA + LANES] * cos + dk[:, A_KVLORA + LANES:] * sin).astype(k_ref.dtype)
    ckv = _rms(dk[:, :A_KVLORA], kvn_ref[...]).astype(MXU)
    k_nope = _dot(ckv, wk_ref[...])
    for hh in range(A_HEADS):
        k_ref[:, hh * hd:hh * hd + A_DNOPE] = k_nope[:, hh * A_DNOPE:(hh + 1) * A_DNOPE].astype(k_ref.dtype)
        k_ref[:, hh * hd + A_DNOPE:(hh + 1) * hd] = k_rope
    v_ref[...] = _dot(ckv, wv_ref[...]).astype(v_ref.dtype)


def _rope_tables(seq, tm):
    n_freq = A_DROPE // 4
    inv_freq = ROPE_BASE ** (-jnp.arange(n_freq, dtype=F32) / n_freq)
    pos = jnp.arange(seq)
    ang_row = (pos // GRID_W).astype(F32)[:, None] * inv_freq
    ang_col = (pos % GRID_W).astype(F32)[:, None] * inv_freq
    cos = jnp.concatenate([jnp.cos(ang_row)] * 2 + [jnp.cos(ang_col)] * 2, axis=1)
    sin = jnp.concatenate([-jnp.sin(ang_row), jnp.sin(ang_row), -jnp.sin(ang_col), jnp.sin(ang_col)], axis=1)
    pad = LANES - A_DROPE
    cos = jnp.pad(cos, ((0, 0), (0, pad)), constant_values=1.0)
    sin = jnp.pad(sin, ((0, 0), (0, pad)))
    ident = (jnp.ones((tm, LANES), F32), jnp.zeros((tm, LANES), F32))
    return jnp.concatenate([ident[0], cos]), jnp.concatenate([ident[1], sin])


def _rope_partner_cols(w):
    idx = np.arange(A_DROPE)
    half = A_DROPE // 4
    partner = np.where((idx % (2 * half)) < half, idx + half, idx - half)
    return w[..., partner]


def _mla_pre(rs, x, mod, w_dq, q_norm, w_uq, w_dkv, kv_norm, w_ukv):
    d, qlora = w_dq.shape
    hd = A_DNOPE + LANES
    pad = LANES - A_DROPE
    wq = w_uq.reshape(qlora, A_HEADS, A_DNOPE + A_DROPE)
    wq_main = jnp.pad(wq, ((0, 0), (0, 0), (0, pad))).reshape(qlora, A_HEADS * hd)
    wq_part = jnp.pad(_rope_partner_cols(wq[..., A_DNOPE:]), ((0, 0), (0, 0), (0, pad))).reshape(qlora, A_HEADS * LANES)
    wuq_all = jnp.concatenate([wq_main, wq_part], axis=1).astype(MXU)
    w_kr = w_dkv[:, A_KVLORA:]
    wdkv_all = jnp.concatenate([w_dkv[:, :A_KVLORA], jnp.pad(w_kr, ((0, 0), (0, pad))),
                                jnp.pad(_rope_partner_cols(w_kr), ((0, 0), (0, pad)))], axis=1).astype(MXU)
    wkv = w_ukv.reshape(A_KVLORA, A_HEADS, A_DNOPE + A_DV)
    w_k = wkv[..., :A_DNOPE].reshape(A_KVLORA, A_HEADS * A_DNOPE).astype(MXU)
    w_v = wkv[..., A_DNOPE:].reshape(A_KVLORA, A_HEADS * A_DV).astype(MXU)
    cos, sin = _rope_tables(rs.seq, rs.tm)
    nl, tps = rs.n_lat_tiles, rs.tiles_per_seq
    tab = pl.BlockSpec((rs.tm, LANES), lambda i: (jnp.where(i >= nl, 0, 1 + i % tps), 0))
    n, r = rs.n_tiles, rs.rows
    vmem = 2 * 2 * (w_dq.size + wuq_all.size + wdkv_all.size + w_k.size + w_v.size) + 12 * rs.tm * A_HEADS * hd * 4
    return _row_call(
        functools.partial(_mla_pre_kernel, q_scale=float((A_DNOPE + A_DROPE) ** -0.5 * np.log2(np.e))), rs, n,
        [rs.tile(d), rs.mod(6 * d), tab, tab, _const_spec(w_dq.shape), _const_spec((1, qlora)), _const_spec(wuq_all.shape),
         _const_spec(wdkv_all.shape), _const_spec((1, A_KVLORA)), _const_spec(w_k.shape), _const_spec(w_v.shape)],
        [rs.tile(A_HEADS * hd), rs.tile(A_HEADS * hd), rs.tile(A_HEADS * A_DV)],
        [jax.ShapeDtypeStruct((r, A_HEADS * hd), MXU), jax.ShapeDtypeStruct((r, A_HEADS * hd), MXU),
         jax.ShapeDtypeStruct((r, A_HEADS * A_DV), MXU)],
        vmem=vmem, name="mla_pre",
    )(x, mod, cos, sin, w_dq.astype(MXU), q_norm.reshape(1, qlora), wuq_all, wdkv_all, kv_norm.reshape(1, A_KVLORA), w_k, w_v)


def _attn_kernel(*refs, with_latent, kv_chunk):
    if with_latent:
        q_ref, kc_ref, vc_ref, kl_ref, vl_ref, o_ref = refs
    else:
        q_ref, kc_ref, vc_ref, o_ref = refs
    hd, dv = A_DNOPE + LANES, A_DV
    heads = range(q_ref.shape[1] // hd)
    qs = lambda h: slice(h * hd, (h + 1) * hd)
    vs = lambda h: slice(h * dv, (h + 1) * dv)
    q = [q_ref[:, qs(h)] for h in heads]
    s = [_dot_nt(q[h], kc_ref[:, qs(h)]) for h in heads]
    m = [jnp.max(s[h], axis=1, keepdims=True) for h in heads]
    p = [jnp.exp2((s[h] - m[h]).astype(MXU)) for h in heads]
    l = [jnp.sum(p[h].astype(F32), axis=1, keepdims=True) for h in heads]
    acc = [_dot(p[h], vc_ref[:, vs(h)]) for h in heads]
    if with_latent:
        for c0 in range(0, kl_ref.shape[0], kv_chunk):
            s = [_dot_nt(q[h], kl_ref[c0:c0 + kv_chunk, qs(h)]) for h in heads]
            m_new = [jnp.maximum(m[h], jnp.max(s[h], axis=1, keepdims=True)) for h in heads]
            corr = [jnp.exp2(m[h] - m_new[h]) for h in heads]
            p = [jnp.exp2((s[h] - m_new[h]).astype(MXU)) for h in heads]
            l = [l[h] * corr[h] + jnp.sum(p[h].astype(F32), axis=1, keepdims=True) for h in heads]
            acc = [acc[h] * corr[h] + _dot(p[h], vl_ref[c0:c0 + kv_chunk, vs(h)]) for h in heads]
            m = m_new
    for h in heads:
        o_ref[:, vs(h)] = (acc[h] / l[h]).astype(o_ref.dtype)


def _mla_attention(rs, q, k, v):
    b, t = rs.batch, rs.seq
    hpb = 2
    hd, dv = hpb * (A_DNOPE + LANES), hpb * A_DV
    kv_chunk = min(1024, t)
    assert t % kv_chunk == 0 and A_HEADS % hpb == 0
    tc = rs.ctx_len
    ctx_blk0 = b * t // tc
    params = pltpu.CompilerParams(dimension_semantics=("arbitrary",) * 3, vmem_limit_bytes=_vmem_limit(48 * 2**20))
    o_ctx = pl.pallas_call(
        functools.partial(_attn_kernel, with_latent=False, kv_chunk=kv_chunk), grid=(b, A_HEADS // hpb, 1),
        in_specs=[pl.BlockSpec((tc, hd), lambda b_, h, i: (ctx_blk0 + b_, h)),
                  pl.BlockSpec((tc, hd), lambda b_, h, i: (ctx_blk0 + b_, h)),
                  pl.BlockSpec((tc, dv), lambda b_, h, i: (ctx_blk0 + b_, h))],
        out_specs=pl.BlockSpec((tc, dv), lambda b_, h, i: (b_, h)),
        out_shape=jax.ShapeDtypeStruct((b * tc, A_HEADS * A_DV), MXU), compiler_params=params, name="mla_attn_ctx",
    )(q, k, v)
    tq = ATTN_TQ
    nq = t // tq
    o_lat = pl.pallas_call(
        functools.partial(_attn_kernel, with_latent=True, kv_chunk=kv_chunk), grid=(b, A_HEADS // hpb, nq),
        in_specs=[pl.BlockSpec((tq, hd), lambda b_, h, i: (b_ * nq + i, h)),
                  pl.BlockSpec((tc, hd), lambda b_, h, i: (ctx_blk0 + b_, h)),
                  pl.BlockSpec((tc, dv), lambda b_, h, i: (ctx_blk0 + b_, h)),
                  pl.BlockSpec((t, hd), lambda b_, h, i: (b_, h)),
                  pl.BlockSpec((t, dv), lambda b_, h, i: (b_, h))],
        out_specs=pl.BlockSpec((tq, dv), lambda b_, h, i: (b_ * nq + i, h)),
        out_shape=jax.ShapeDtypeStruct((b * t, A_HEADS * A_DV), MXU), compiler_params=params, name="mla_attn_latent",
    )(q, k, v, k, v)
    return o_lat, o_ctx


def _proj_post_kernel(x_ref, al_ref, ac_ref, mod_ref, wout_ref, g_ref, b_ref, o_ref, *, alpha, n_lat_tiles):
    a = jnp.where(pl.program_id(0) >= n_lat_tiles, ac_ref[...], al_ref[...])
    _residual_ln(x_ref, mod_ref, a, wout_ref, g_ref, b_ref, o_ref, alpha)


def _proj_post(rs, n_tiles, x, a_lat, a_ctx, mod, w_out, ln_g, ln_b, alpha):
    kdim, d = w_out.shape
    nl = rs.n_lat_tiles
    body = functools.partial(_proj_post_kernel, alpha=alpha, n_lat_tiles=nl)
    return _row_call(
        body, rs, n_tiles,
        [rs.tile(d), pl.BlockSpec((rs.tm, kdim), lambda i: (jnp.minimum(i, nl - 1), 0)),
         pl.BlockSpec((rs.tm, kdim), lambda i: (jnp.maximum(i - nl, 0), 0)), rs.mod(6 * d), _const_spec(w_out.shape),
         _const_spec((1, d)), _const_spec((1, d))],
        rs.tile(d), jax.ShapeDtypeStruct((n_tiles * rs.tm, d), F32), vmem=32 * 2**20, name="proj_post",
    )(x, a_lat, a_ctx, mod, w_out, ln_g.reshape(1, d), ln_b.reshape(1, d))


def _gla_pre_kernel(x_ref, mod_ref, wqk_ref, wv_ref, wa1_ref, wa2_ref, ba_ref, q_ref, k_ref, v_ref, la_ref,
                    *, q_scale):
    d = x_ref.shape[1]
    half = q_ref.shape[1]
    chunks = [slice(r0, r0 + HEAD_ROWS) for r0 in range(0, x_ref.shape[0], HEAD_ROWS)]
    h = [_mod(x_ref[r, :], mod_ref, 0, d).astype(MXU) for r in chunks]
    a1 = [_dot(hc, wa1_ref[...]).astype(MXU) for hc in h]
    z = [_dot(a, wa2_ref[...]) + ba_ref[...] for a in a1]
    qk = [_dot(hc, wqk_ref[...]) for hc in h]
    for i, r in enumerate(chunks):
        la_ref[r, :] = _log_sigmoid(z[i]) * (LOG2E / G_TAU)
        q_ref[r, :] = qk[i][:, :half] * q_scale
        k_ref[r, :] = qk[i][:, half:]
    for i, r in enumerate(chunks):
        v_ref[r, :] = _dot(h[i], wv_ref[...]).astype(v_ref.dtype)


def _gla_pre(rs, x, mod, w_qk, w_v, w_a1, w_a2, b_a):
    d = w_qk.shape[0]
    dk_all = w_qk.shape[1] // 2
    wa1 = jnp.pad(jnp.concatenate([w_a1[0], w_a1[1]], axis=1), ((0, 0), (0, LANES - 2 * G_RANK))).astype(MXU)
    wa2 = jnp.zeros((LANES, 2 * dk_all), F32)
    wa2 = wa2.at[:G_RANK, :dk_all].set(w_a2[0]).at[G_RANK:2 * G_RANK, dk_all:].set(w_a2[1]).astype(MXU)
    ba = jnp.concatenate([b_a[0], b_a[1]]).reshape(1, 2 * dk_all)
    body = functools.partial(_gla_pre_kernel, q_scale=float((dk_all // G_HEADS) ** -0.5))
    n, r = rs.n_tiles, rs.rows
    return _row_call(
        body, rs, n,
        [rs.tile(d), rs.mod(6 * d), _const_spec(w_qk.shape), _const_spec(w_v.shape), _const_spec(wa1.shape),
         _const_spec(wa2.shape), _const_spec(ba.shape)],
        [rs.tile(dk_all), rs.tile(dk_all), rs.tile(w_v.shape[1]), rs.tile(2 * dk_all)],
        [jax.ShapeDtypeStruct((r, dk_all), F32), jax.ShapeDtypeStruct((r, dk_all), F32),
         jax.ShapeDtypeStruct((r, w_v.shape[1]), MXU), jax.ShapeDtypeStruct((r, 2 * dk_all), F32)],
        vmem=40 * 2**20, name="gla_pre",
    )(x, mod, w_qk, w_v, wa1, wa2, ba)


def _gla_tables(L):
    t = np.arange(L)
    tau, taup = t[:, None], t[None, :]
    groups = [taup <= tau, taup > tau]
    masks = []
    c = L // 2
    while c >= 1:
        blk = t // (2 * c)
        mid = blk * 2 * c + c
        second = (t % (2 * c)) >= c
        q_side = second[:, None] & (taup >= mid[:, None]) & (taup <= tau)
        k_side = (~second)[:, None] & (taup > tau) & (taup <= mid[:, None] - 1)
        groups.append(q_side | k_side)
        masks.append(second[:, None] & (~second)[None, :] & (blk[:, None] == blk[None, :]))
        c //= 2
    flip = lambda g: g[::-1, ::-1]
    sums = np.stack([np.concatenate(groups, axis=0), np.concatenate([flip(g) for g in groups], axis=0)])
    lvl = np.stack([np.stack(masks), np.stack([flip(m) for m in masks])])
    return sums.astype(np.float32), lvl.astype(np.float32)


def _gla_scan_kernel(q0_ref, k0_ref, v0_ref, la0_ref, q1_ref, k1_ref, v1_ref, la1_ref, sums_ref, masks_ref,
                     o0_ref, o1_ref, st_scr):
    @pl.when(pl.program_id(1) == 0)
    def _():
        st_scr[...] = jnp.zeros_like(st_scr)

    L = q0_ref.shape[0]
    dk, dv = q0_ref.shape[1] // G_HEADS, v0_ref.shape[1] // G_HEADS
    levels = masks_ref.shape[1]
    dirs = ((q0_ref, k0_ref, v0_ref, la0_ref, o0_ref), (q1_ref, k1_ref, v1_ref, la1_ref, o1_ref))
    chains = [(d, h) for d in range(2) for h in range(G_HEADS)]
    row = lax.broadcasted_iota(jnp.int32, (L, L), 0)
    col = lax.broadcasted_iota(jnp.int32, (L, L), 1)
    eye = row == col

    st_prev = {(d, h): st_scr[d, h] for d, h in chains}
    e = [jnp.exp2(_dot(sums_ref[d], jnp.concatenate(_split2(dirs[d][3][...]), axis=0))) for d in range(2)]
    q = [dirs[d][0][...] for d in range(2)]
    k = [dirs[d][1][...] for d in range(2)]
    qe = [(q[d] * e[d][0:L]).astype(MXU) for d in range(2)]
    kd = [(k[d] * e[d][L:2 * L]).astype(MXU) for d in range(2)]
    qk_diag = [q[d] * k[d] for d in range(2)]
    qt = [[(q[d] * e[d][(2 + lv) * L:(3 + lv) * L]).astype(MXU) for lv in range(levels)] for d in range(2)]
    kt = [[(k[d] * e[d][(2 + lv) * L:(3 + lv) * L]).astype(MXU) for lv in range(levels)] for d in range(2)]
    decay = [jnp.exp2(jnp.sum(dirs[d][3][...], axis=0, keepdims=True)) for d in range(2)]

    att = {}
    for d, h in chains:
        ks = slice(h * dk, (h + 1) * dk)
        a = jnp.where(eye, jnp.sum(qk_diag[d][:, ks], axis=1, keepdims=True), 0.0)
        for lv in range(levels):
            a = a + masks_ref[d, lv] * _dot_nt(qt[d][lv][:, ks], kt[d][lv][:, ks])
        att[d, h] = a.astype(MXU)
    for d, h in chains:
        ks, vs = slice(h * dk, (h + 1) * dk), slice(h * dv, (h + 1) * dv)
        v = dirs[d][2][:, vs]
        dirs[d][4][:, vs] = _dot_nt(qe[d][:, ks], st_prev[d, h].astype(MXU)) + _dot(att[d, h], v)
        st_scr[d, h] = st_prev[d, h] * decay[d][:, ks] + _dot_tn(v, kd[d][:, ks])


def _gla_scan(rs, q, k, v, la):
    b, L = rs.batch, G_CHUNK
    dk_all, dv_all = q.shape[1], v.shape[1]
    nc_ctx, nc_lat = rs.ctx_len // L, rs.seq // L
    rowblk = _scan_rowblock(b, nc_ctx, nc_lat)
    sums, masks = _gla_tables(L)
    sums, masks = jnp.asarray(np.concatenate([sums, sums], axis=2), MXU), jnp.asarray(masks, F32)

    def specs(d):
        rb = lambda b_, c: (rowblk(b_, d, c), 0)
        return [pl.BlockSpec((L, dk_all), rb), pl.BlockSpec((L, dk_all), rb), pl.BlockSpec((L, dv_all), rb),
                pl.BlockSpec((L, dk_all), lambda b_, c: (rowblk(b_, d, c), d))]

    out_shape = jax.ShapeDtypeStruct((rs.rows, dv_all), F32)
    return pl.pallas_call(
        _gla_scan_kernel, grid=(b, nc_ctx + nc_lat),
        in_specs=specs(0) + specs(1) + [_const_spec(sums.shape), _const_spec(masks.shape)],
        out_specs=[pl.BlockSpec((L, dv_all), lambda b_, c: (rowblk(b_, 0, c), 0)),
                   pl.BlockSpec((L, dv_all), lambda b_, c: (rowblk(b_, 1, c), 0))],
        out_shape=[out_shape, out_shape],
        scratch_shapes=[pltpu.VMEM((2, G_HEADS, dv_all // G_HEADS, dk_all // G_HEADS), F32)],
        compiler_params=pltpu.CompilerParams(dimension_semantics=("arbitrary",) * 2,
                                             vmem_limit_bytes=_vmem_limit(32 * 2**20)),
        name="gla_scan",
    )(q, k, v, la, q, k, v, la, sums, masks)


def _gla_post_kernel(x_ref, s0_ref, s1_ref, mod_ref, wr_ref, ng_ref, wout_ref, g_ref, b_ref, o_ref, *, alpha):
    d = x_ref.shape[1]
    chunks = [slice(r0, r0 + HEAD_ROWS) for r0 in range(0, x_ref.shape[0], HEAD_ROWS)]
    h = [_mod(x_ref[r, :], mod_ref, 0, d).astype(MXU) for r in chunks]
    gate = [_silu(_dot(hc, wr_ref[...])) for hc in h]
    a = [(_head_rms(s0_ref[r, :] + s1_ref[r, :], ng_ref[...], G_HEADS) * gate[i]).astype(MXU) for i, r in enumerate(chunks)]
    _residual_ln(x_ref, mod_ref, jnp.concatenate(a, axis=0), wout_ref, g_ref, b_ref, o_ref, alpha)


def _gla_post(rs, n_tiles, x, s, mod, w_r, norm_g, w_out, ln_g, ln_b, alpha):
    d, dv_all = w_r.shape
    body = functools.partial(_gla_post_kernel, alpha=alpha)
    return _row_call(
        body, rs, n_tiles,
        [rs.tile(d), rs.tile(dv_all), rs.tile(dv_all), rs.mod(6 * d), _const_spec(w_r.shape), _const_spec((1, dv_all)), _const_spec(w_out.shape),
         _const_spec((1, d)), _const_spec((1, d))],
        rs.tile(d), jax.ShapeDtypeStruct((n_tiles * rs.tm, d), F32), vmem=40 * 2**20, name="gla_post",
    )(x, s[0], s[1], mod, w_r, norm_g.reshape(1, dv_all), w_out, ln_g.reshape(1, d), ln_b.reshape(1, d))


def kernel(x, c, ctx, c_ctx, ada_w, ada_b, ln_g, ln_b, ffn_w_in, ffn_conv_w, ffn_conv_b, ffn_w_out, m_w_up, m_conv_w, m_conv_b, m_w_qk, m_w_v, m_w_gates, m_b_gates, m_w_og, m_norm_g, m_w_out, a_w_dq, a_q_norm, a_w_uq, a_w_dkv, a_kv_norm, a_w_ukv, a_w_out, g_w_qk, g_w_v, g_w_r, g_w_a1, g_w_a2, g_b_a, g_norm_g, g_w_out):
    batch, seq, d = x.shape
    depth = ada_w.shape[0]
    n_mixers = 3
    alpha = float((2 * depth) ** 0.25)
    rs = _Rows(batch, seq, ctx.shape[1], TM)
    rs_big = _Rows(batch, seq, ctx.shape[1], BIG_TM)
    bf = lambda w: w.astype(MXU)

    cond_rows = -(-(batch + 1) // SUBLANES) * SUBLANES
    cond = jnp.zeros((cond_rows, d), F32).at[:batch].set(c).at[batch].set(c_ctx)
    mods = _modulation(cond, ada_w, ada_b).reshape(depth, cond_rows, 1, 6 * d)

    x_lat, x_ctx = x.reshape(batch * seq, d), ctx.reshape(-1, d)
    xa = None
    for i in range(depth):
        need_ctx = i < depth - 1
        n_tiles = rs.n_tiles if need_ctx else rs.n_lat_tiles
        n_big = rs_big.n_tiles if need_ctx else rs_big.n_lat_tiles
        kind, j = i % n_mixers, i // n_mixers
        mod = mods[i]
        if kind == 0:
            src = (x_lat, x_ctx, 0) if xa is None else (xa, xa, rs.n_lat_tiles)
            q, k, v, gates = _mlstm_pre(rs, *src, mod, bf(m_w_up[j]), m_conv_w[j], m_conv_b[j], bf(m_w_qk[j]),
                                        _weight_product(m_w_up[j], m_w_v[j]), m_w_gates[j], m_b_gates[j])
            s = _mlstm_scan(rs, q, k, v, gates)
            xa = _mlstm_post(rs, n_tiles, *src, s, mod, bf(m_w_og[j]), m_norm_g[j], bf(m_w_out[j]), ln_g[i, 0],
                             ln_b[i, 0], alpha)
        elif xa is None:
            xa = jnp.concatenate([x_lat, x_ctx], axis=0)
        if kind == 1:
            q, k, v = _mla_pre(rs_big, xa, mod, a_w_dq[j], a_q_norm[j], a_w_uq[j], a_w_dkv[j], a_kv_norm[j], a_w_ukv[j])
            o_lat, o_ctx = _mla_attention(rs_big, q, k, v)
            xa = _proj_post(rs_big, n_big, xa, o_lat, o_ctx, mod, bf(a_w_out[j]), ln_g[i, 0], ln_b[i, 0], alpha)
        elif kind == 2:
            q, k, v, la = _gla_pre(rs_big, xa, mod, bf(g_w_qk[j]), bf(g_w_v[j]), g_w_a1[j], g_w_a2[j], g_b_a[j])
            s = _gla_scan(rs_big, q, k, v, la)
            xa = _gla_post(rs_big, n_big, xa, s, mod, bf(g_w_r[j]), g_norm_g[j], bf(g_w_out[j]), ln_g[i, 0], ln_b[i, 0], alpha)
        xa = _ffn(rs_big, n_big, xa, mod, bf(ffn_w_in[i]), ffn_conv_w[i], ffn_conv_b[i], bf(ffn_w_out[i]),
                  ln_g[i, 1], ln_b[i, 1], alpha)
    return xa.reshape(batch, seq, d)
```

```python
import functools

import numpy as np
import jax
import jax.numpy as jnp
from jax import lax
from jax.experimental import pallas as pl
from jax.experimental.pallas import tpu as pltpu

F32 = jnp.float32
MXU = jnp.bfloat16

V7X_VMEM_BYTES = 64 * 2**20
SUBLANES = 8
LANES = 128

TM = 512
BIG_TM = 1024
ATTN_TQ = 1024
HEAD_ROWS = 256
TAIL_ROWS = 128
HALO = SUBLANES
GRID_W = 64
EPS = 1e-6
ROPE_BASE = 10000.0
G_TAU = 16.0
LOG2E = float(np.log2(np.e))

M_HEADS, A_HEADS, G_HEADS = 4, 8, 4
A_DNOPE, A_DROPE, A_DV, A_KVLORA = 128, 64, 128, 256
G_RANK = 16
M_CHUNK = 256
M_SCAN_GROUP = 8
G_CHUNK = 128


def _vmem_limit(nbytes):
    return int(min(max(nbytes, 16 * 2**20), V7X_VMEM_BYTES - 8 * 2**20))


def _const_spec(shape):
    nd = len(shape)
    return pl.BlockSpec(shape, lambda *_: (0,) * nd, pipeline_mode=pl.Buffered(1))


def _dot(a, b):
    return jnp.dot(a, b, preferred_element_type=F32)


def _dot_nt(a, b):
    return lax.dot_general(a, b, (((1,), (1,)), ((), ())), preferred_element_type=F32)


def _dot_tn(a, b):
    return lax.dot_general(a, b, (((0,), (0,)), ((), ())), preferred_element_type=F32)


def _split3(x):
    hi = x.astype(MXU)
    r1 = x - hi.astype(F32)
    mid = r1.astype(MXU)
    lo = (r1 - mid.astype(F32)).astype(MXU)
    return hi, mid, lo


def _split2(x):
    hi = x.astype(MXU)
    return hi, (x - hi.astype(F32)).astype(MXU)


def _sigmoid(x):
    return 1.0 / (1.0 + jnp.exp(-x))


def _silu(x):
    return x * _sigmoid(x)


def _log_sigmoid(x):
    return jnp.minimum(x, 0.0) - jnp.log1p(jnp.exp(-jnp.abs(x)))


def _layer_norm(z, g, b):
    mu = jnp.mean(z, -1, keepdims=True)
    zc = z - mu
    var = jnp.mean(zc * zc, -1, keepdims=True)
    return zc * lax.rsqrt(var + EPS) * g + b


def _rms(x, g):
    return x * lax.rsqrt(jnp.mean(x * x, -1, keepdims=True) + EPS) * g


def _head_rms(x, g, heads):
    d = x.shape[-1] // heads
    return jnp.concatenate([_rms(x[:, h * d:(h + 1) * d], g[:, h * d:(h + 1) * d]) for h in range(heads)], axis=-1)


def _mod(x, mod_ref, k, d):
    return x * (1.0 + mod_ref[:, (k + 1) * d:(k + 2) * d]) + mod_ref[:, k * d:(k + 1) * d]


def _halo_rows(xp_ref, x, xn_ref, mod_ref, k, d, geom):
    n_lat_tiles, tiles_per_seq, _, _ = geom
    i = pl.program_id(0)
    is_ctx = i >= n_lat_tiles
    pos = i % tiles_per_seq
    first = jnp.logical_or(is_ctx, pos == 0)
    last = jnp.logical_or(is_ctx, pos == tiles_per_seq - 1)
    hp = jnp.where(first, 0.0, _mod(xp_ref[...], mod_ref, k, d))
    hn = jnp.where(last, 0.0, _mod(xn_ref[...], mod_ref, k, d))
    return jnp.concatenate([hp, _mod(x, mod_ref, k, d), hn], axis=0)


def _tile_rows(xl_ref, xc_ref, geom):
    return jnp.where(pl.program_id(0) >= geom[0], xc_ref[...], xl_ref[...])


def _seq_edges(geom):
    n_lat_tiles, _, ctx_len, tm = geom
    is_ctx = pl.program_id(0) >= n_lat_tiles
    r = lax.broadcasted_iota(jnp.int32, (tm, 1), 0) % ctx_len
    return jnp.logical_and(is_ctx, r == 0), jnp.logical_and(is_ctx, r == ctx_len - 1)


def _dwconv3(g_ext, w_ref, b_ref, c0, c1, edges):
    n = g_ext.shape[0] - 2 * HALO
    starts, ends = edges
    prev = jnp.where(starts, 0.0, g_ext[HALO - 1:HALO - 1 + n])
    nxt = jnp.where(ends, 0.0, g_ext[HALO + 1:HALO + 1 + n])
    return (w_ref[0:1, c0:c1] * prev + w_ref[1:2, c0:c1] * g_ext[HALO:HALO + n] + w_ref[2:3, c0:c1] * nxt
            + b_ref[:, c0:c1])


class _Rows:
    def __init__(self, batch, seq, ctx_len, tm):
        assert tm % ctx_len == 0 and (batch * ctx_len) % tm == 0 and seq % tm == 0
        self.batch, self.seq, self.ctx_len, self.tm = batch, seq, ctx_len, tm
        self.tiles_per_seq = seq // tm
        self.n_lat_tiles = batch * self.tiles_per_seq
        self.n_tiles = self.n_lat_tiles + batch * ctx_len // tm
        self.rows = self.n_tiles * tm
        self.geom = (self.n_lat_tiles, self.tiles_per_seq, ctx_len, tm)

    def tile(self, width):
        return pl.BlockSpec((self.tm, width), lambda i: (i, 0))

    def tile_pair(self, width, ctx_blk0):
        nl = self.n_lat_tiles
        return [pl.BlockSpec((self.tm, width), lambda i: (jnp.minimum(i, nl - 1), 0)),
                pl.BlockSpec((self.tm, width), lambda i: (ctx_blk0 + jnp.maximum(i - nl, 0), 0))]

    def halo_prev(self, width, n_rows):
        per, nblk = self.tm // HALO, n_rows // HALO
        return pl.BlockSpec((HALO, width), lambda i: (jnp.clip(i * per - 1, 0, nblk - 1), 0))

    def halo_next(self, width, n_rows):
        per, nblk = self.tm // HALO, n_rows // HALO
        return pl.BlockSpec((HALO, width), lambda i: (jnp.minimum((i + 1) * per, nblk - 1), 0))

    def mod(self, width):
        nl, tps, b = self.n_lat_tiles, self.tiles_per_seq, self.batch
        return pl.BlockSpec((None, 1, width), lambda i: (jnp.where(i >= nl, b, i // tps), 0, 0))


def _row_call(body, rows, n_tiles, in_specs, out_specs, out_shape, scratch=(), vmem=0, name=None):
    return pl.pallas_call(
        body, grid=(n_tiles,), in_specs=in_specs, out_specs=out_specs, out_shape=out_shape,
        scratch_shapes=list(scratch), name=name,
        compiler_params=pltpu.CompilerParams(dimension_semantics=("arbitrary",), vmem_limit_bytes=_vmem_limit(vmem)))


def _modulation_kernel(c_ref, w_ref, b_ref, o_ref):
    o_ref[...] = _dot(_silu(c_ref[...]).astype(MXU), w_ref[...].astype(MXU)) + b_ref[...]


def _modulation(cond, ada_w, ada_b):
    depth, d, n = ada_w.shape
    tn = d
    return pl.pallas_call(
        _modulation_kernel, grid=(depth, n // tn),
        in_specs=[pl.BlockSpec(cond.shape, lambda l, j: (0, 0)),
                  pl.BlockSpec((None, d, tn), lambda l, j: (l, 0, j)),
                  pl.BlockSpec((None, 1, tn), lambda l, j: (l, 0, j))],
        out_specs=pl.BlockSpec((None, cond.shape[0], tn), lambda l, j: (l, 0, j)),
        out_shape=jax.ShapeDtypeStruct((depth, cond.shape[0], n), F32), name="modulation",
    )(cond, ada_w, ada_b.reshape(depth, 1, n))


def _ffn_kernel(xp_ref, x_ref, xn_ref, mod_ref, win_ref, cw_ref, cb_ref, wout_ref, g_ref, b_ref, o_ref, a_scr,
                *, geom, alpha, ffn, tf):
    tm, d = x_ref.shape
    hext = _halo_rows(xp_ref, x_ref[...], xn_ref, mod_ref, 3, d, geom).astype(MXU)
    edges = _seq_edges(geom)
    for f0 in range(0, ffn, tf):
        g_ext = _dot(hext, win_ref[:, f0:f0 + tf])
        up = _dot(hext, win_ref[:, ffn + f0:ffn + f0 + tf])[HALO:HALO + tm]
        a_scr[:, f0:f0 + tf] = (_silu(_dwconv3(g_ext, cw_ref, cb_ref, f0, f0 + tf, edges)) * up).astype(MXU)
    for r0 in range(0, tm, TAIL_ROWS):
        rows = slice(r0, r0 + TAIL_ROWS)
        f = _dot(a_scr[rows, :], wout_ref[...])
        z = alpha * x_ref[rows, :] + mod_ref[:, 5 * d:6 * d] * f
        o_ref[rows, :] = _layer_norm(z, g_ref[...], b_ref[...])


def _ffn(rs, n_tiles, x, mod, w_in, conv_w, conv_b, w_out, ln_g, ln_b, alpha):
    d, ffn = w_out.shape[1], w_out.shape[0]
    tf = 256
    assert ffn % tf == 0
    body = functools.partial(_ffn_kernel, geom=rs.geom, alpha=alpha, ffn=ffn, tf=tf)
    vmem = 2 * (w_in.size + w_out.size) + 6 * rs.tm * d * 4 + rs.tm * ffn * 2 + 24 * 2**20
    return _row_call(
        body, rs, n_tiles,
        [rs.halo_prev(d, x.shape[0]), rs.tile(d), rs.halo_next(d, x.shape[0]), rs.mod(6 * d), _const_spec(w_in.shape), _const_spec(conv_w.shape),
         _const_spec((1, ffn)), _const_spec(w_out.shape), _const_spec((1, d)), _const_spec((1, d))],
        rs.tile(d), jax.ShapeDtypeStruct((n_tiles * rs.tm, d), F32),
        scratch=[pltpu.VMEM((rs.tm, ffn), MXU)], vmem=vmem, name="conv_ffn",
    )(x, x, x, mod, w_in, conv_w, conv_b.reshape(1, ffn), w_out, ln_g.reshape(1, d), ln_b.reshape(1, d))


def _residual_ln(x_ref, mod_ref, a, wout_ref, g_ref, b_ref, o_ref, alpha):
    tm, d = x_ref.shape
    for r0 in range(0, tm, TAIL_ROWS):
        rows = slice(r0, r0 + TAIL_ROWS)
        z = alpha * x_ref[rows, :] + mod_ref[:, 2 * d:3 * d] * _dot(a[rows, :], wout_ref[...])
        o_ref[rows, :] = _layer_norm(z, g_ref[...], b_ref[...])


def _weight_product_kernel(a_ref, b_ref, o_ref):
    o_ref[...] = _dot(a_ref[...].astype(MXU), b_ref[...].astype(MXU)).astype(o_ref.dtype)


def _weight_product(a, b):
    m, kdim = a.shape
    n = b.shape[1]
    tn = 512
    assert n % tn == 0
    return pl.pallas_call(
        _weight_product_kernel, grid=(n // tn,),
        in_specs=[pl.BlockSpec((m, kdim), lambda j: (0, 0)), pl.BlockSpec((kdim, tn), lambda j: (0, j))],
        out_specs=pl.BlockSpec((m, tn), lambda j: (0, j)),
        out_shape=jax.ShapeDtypeStruct((m, n), MXU),
        compiler_params=pltpu.CompilerParams(dimension_semantics=("arbitrary",), vmem_limit_bytes=_vmem_limit(40 * 2**20)),
        name="weight_product",
    )(a, b)


def _mlstm_pre_kernel(xp_ref, xl_ref, xc_ref, xn_ref, mod_ref, wup_ref, cw_ref, cb_ref, wqk_ref, wv_ref, wg_ref, bg_ref,
                      q_ref, k_ref, v_ref, gates_ref, *, geom, k_scale):
    d = xl_ref.shape[1]
    x = _tile_rows(xl_ref, xc_ref, geom)
    hext = _halo_rows(xp_ref, x, xn_ref, mod_ref, 0, d, geom).astype(MXU)
    xm_ext = _dot(hext, wup_ref[...])
    inner = xm_ext.shape[1]
    xc = _silu(_dwconv3(xm_ext, cw_ref, cb_ref, 0, inner, _seq_edges(geom))).astype(MXU)
    qk = _dot(xc, wqk_ref[...])
    half = qk.shape[1] // 2
    q_ref[...] = qk[:, :half].astype(q_ref.dtype)
    k_ref[...] = (qk[:, half:] * k_scale).astype(k_ref.dtype)
    v_ref[...] = _dot(_mod(x, mod_ref, 0, d).astype(MXU), wv_ref[...]).astype(v_ref.dtype)
    two_h = 2 * M_HEADS
    gates = _dot(xc, wg_ref[...]) + bg_ref[...]
    lane = lax.broadcasted_iota(jnp.int32, gates.shape, 1)
    gates_ref[...] = jnp.where(lane % two_h >= M_HEADS, _log_sigmoid(gates), gates) * LOG2E


def _mlstm_pre(rs, x_lat, x_ctx, ctx_blk0, mod, w_up, conv_w, conv_b, w_qk, w_v, w_gates, b_gates):
    d, inner = w_up.shape
    dqk_all = w_qk.shape[1] // 2
    n_g = 2 * 2 * M_HEADS
    wg = jnp.concatenate([w_gates[0], w_gates[1]], axis=1)
    wg_pad = jnp.pad(wg, ((0, 0), (0, LANES - n_g))).astype(MXU)
    bg = jnp.concatenate([b_gates[0], b_gates[1]])
    bg_pad = jnp.pad(bg, (0, LANES - n_g)).reshape(1, LANES)
    body = functools.partial(_mlstm_pre_kernel, geom=rs.geom, k_scale=float((dqk_all // M_HEADS) ** -0.5))
    n, r = rs.n_tiles, rs.rows
    vmem = 2 * (w_up.size + w_qk.size + w_v.size) + 8 * (TM + 2 * HALO) * inner * 4 + 16 * 2**20
    return _row_call(
        body, rs, n,
        [rs.halo_prev(d, x_lat.shape[0])] + rs.tile_pair(d, ctx_blk0) + [rs.halo_next(d, x_lat.shape[0]), rs.mod(6 * d),
         _const_spec(w_up.shape), _const_spec(conv_w.shape), _const_spec((1, inner)), _const_spec(w_qk.shape),
         _const_spec(w_v.shape), _const_spec((inner, LANES)), _const_spec((1, LANES))],
        [rs.tile(dqk_all), rs.tile(dqk_all), rs.tile(inner), rs.tile(LANES)],
        [jax.ShapeDtypeStruct((r, dqk_all), MXU), jax.ShapeDtypeStruct((r, dqk_all), MXU),
         jax.ShapeDtypeStruct((r, inner), MXU), jax.ShapeDtypeStruct((r, LANES), F32)],
        vmem=vmem, name="mlstm_pre",
    )(x_lat, x_lat, x_ctx, x_lat, mod, w_up, conv_w, conv_b.reshape(1, inner), w_qk, w_v, wg_pad, bg_pad)


def _scan_rowblock(batch, nc_ctx, nc_lat):
    def rowblk(b, d, c):
        cc = jnp.where(d == 1, nc_ctx - 1 - c, c)
        lc = c - nc_ctx
        lc = jnp.where(d == 1, nc_lat - 1 - lc, lc)
        return jnp.where(c < nc_ctx, batch * nc_lat + b * nc_ctx + cc, b * nc_lat + lc)
    return rowblk


def _mlstm_scan_kernel(q0_ref, k0_ref, v0_ref, g0_ref, q1_ref, k1_ref, v1_ref, g1_ref,
                       o0_ref, o1_ref, c_scr, n_scr, m_scr):
    @pl.when(pl.program_id(1) == 0)
    def _():
        c_scr[...] = jnp.zeros_like(c_scr)
        n_scr[...] = jnp.zeros_like(n_scr)
        m_scr[...] = jnp.zeros_like(m_scr)

    L = q0_ref.shape[0]
    dqk, dv = q0_ref.shape[1] // M_HEADS, v0_ref.shape[1] // M_HEADS
    two_h = 2 * M_HEADS
    dirs = ((q0_ref, k0_ref, v0_ref, g0_ref, o0_ref), (q1_ref, k1_ref, v1_ref, g1_ref, o1_ref))
    chains = [(d, h) for d in range(2) for h in range(M_HEADS)]
    row = lax.broadcasted_iota(jnp.int32, (L, L), 0)
    col = lax.broadcasted_iota(jnp.int32, (L, L), 1)
    causal = [col <= row, col >= row]
    causal_t = [row <= col, row >= col]
    eye = jnp.where(row == col, 1.0, 0.0).astype(MXU)
    gates = [dirs[d][3][...] for d in range(2)]
    pieces = [_split3(gates[d]) for d in range(2)]
    cs = [sum(_dot(jnp.where(causal[d], 1.0, 0.0).astype(MXU), p) for p in pieces[d]) for d in range(2)]
    cs_t = [sum(_dot_tn(p, jnp.where(causal_t[d], 1.0, 0.0).astype(MXU)) for p in pieces[d]) for d in range(2)]
    gates_t = [sum(_dot_tn(p, eye) for p in pieces[d]) for d in range(2)]

    i_slot = lambda d, h: d * two_h + h
    f_slot = lambda d, h: d * two_h + M_HEADS + h
    qs = lambda h: slice(h * dqk, (h + 1) * dqk)
    vs = lambda h: slice(h * dv, (h + 1) * dv)
    for g0 in range(0, len(chains), M_SCAN_GROUP):
        _mlstm_advance(chains[g0:g0 + M_SCAN_GROUP], dirs, causal, gates, gates_t, cs, cs_t, i_slot, f_slot, qs, vs,
                       c_scr, n_scr, m_scr)


def _mlstm_advance(chains, dirs, causal, gates, gates_t, cs, cs_t, i_slot, f_slot, qs, vs, c_scr, n_scr, m_scr):
    def per_chain(fn):
        return {ch: fn(*ch) for ch in chains}

    c_prev = per_chain(lambda d, h: c_scr[d, h])
    n_prev = per_chain(lambda d, h: n_scr[d, h])
    m_prev = per_chain(lambda d, h: m_scr[d, h])
    q = per_chain(lambda d, h: dirs[d][0][:, qs(h)])
    k = per_chain(lambda d, h: dirs[d][1][:, qs(h)])
    v = per_chain(lambda d, h: dirs[d][2][:, vs(h)])
    li_col = per_chain(lambda d, h: gates[d][:, i_slot(d, h):i_slot(d, h) + 1])
    li_row = per_chain(lambda d, h: gates_t[d][i_slot(d, h):i_slot(d, h) + 1, :])
    b_col = per_chain(lambda d, h: cs[d][:, f_slot(d, h):f_slot(d, h) + 1])
    b_row = per_chain(lambda d, h: cs_t[d][f_slot(d, h):f_slot(d, h) + 1, :])
    bl = per_chain(lambda d, h: jnp.sum(gates_t[d][f_slot(d, h):f_slot(d, h) + 1, :], axis=1, keepdims=True))

    dmat = per_chain(lambda d, h: jnp.where(causal[d], b_col[d, h] - b_row[d, h] + li_row[d, h], -jnp.inf))
    inter = per_chain(lambda d, h: b_col[d, h] + m_prev[d, h])
    mj = per_chain(lambda d, h: jnp.maximum(inter[d, h], jnp.max(dmat[d, h], axis=1, keepdims=True)))
    qk = per_chain(lambda d, h: _dot_nt(q[d, h], k[d, h]))
    wmat = per_chain(lambda d, h: jnp.exp2(dmat[d, h] - mj[d, h]) * qk[d, h])
    g = per_chain(lambda d, h: jnp.exp2(inter[d, h] - mj[d, h]))
    qc = per_chain(lambda d, h: _dot(q[d, h], c_prev[d, h].astype(MXU)))
    wv = per_chain(lambda d, h: _dot(wmat[d, h].astype(MXU), v[d, h]))
    qn = per_chain(lambda d, h: jnp.sum(q[d, h].astype(F32) * n_prev[d, h], axis=1, keepdims=True))
    den = per_chain(lambda d, h: g[d, h] * qn[d, h] + jnp.sum(wmat[d, h], axis=1, keepdims=True))
    for d, h in chains:
        num = g[d, h] * qc[d, h] + wv[d, h]
        dirs[d][4][:, vs(h)] = num / jnp.maximum(jnp.abs(den[d, h]), jnp.exp2(-mj[d, h]))

    ds = per_chain(lambda d, h: bl[d, h] - b_col[d, h] + li_col[d, h])
    m_new = per_chain(lambda d, h: jnp.maximum(bl[d, h] + m_prev[d, h], jnp.max(ds[d, h], axis=0, keepdims=True)))
    kw = per_chain(lambda d, h: k[d, h].astype(F32) * jnp.exp2(ds[d, h] - m_new[d, h]))
    decay = per_chain(lambda d, h: jnp.exp2(bl[d, h] + m_prev[d, h] - m_new[d, h]))
    kv = per_chain(lambda d, h: _dot_tn(kw[d, h].astype(MXU), v[d, h]))
    for d, h in chains:
        c_scr[d, h] = decay[d, h] * c_prev[d, h] + kv[d, h]
        n_scr[d, h] = decay[d, h] * n_prev[d, h] + jnp.sum(kw[d, h], axis=0, keepdims=True)
        m_scr[d, h] = m_new[d, h]


def _mlstm_scan(rs, q, k, v, gates):
    b, L = rs.batch, M_CHUNK
    dqk_all, dv_all = q.shape[1], v.shape[1]
    nc_ctx, nc_lat = rs.ctx_len // L, rs.seq // L
    rowblk = _scan_rowblock(b, nc_ctx, nc_lat)

    def specs(d):
        rb = lambda b_, c: (rowblk(b_, d, c), 0)
        return [pl.BlockSpec((L, dqk_all), rb), pl.BlockSpec((L, dqk_all), rb), pl.BlockSpec((L, dv_all), rb),
                pl.BlockSpec((L, LANES), rb)]

    out_shape = jax.ShapeDtypeStruct((rs.rows, dv_all), F32)
    dqk, dv = dqk_all // M_HEADS, dv_all // M_HEADS
    return pl.pallas_call(
        _mlstm_scan_kernel, grid=(b, nc_ctx + nc_lat),
        in_specs=specs(0) + specs(1),
        out_specs=[pl.BlockSpec((L, dv_all), lambda b_, c: (rowblk(b_, 0, c), 0)),
                   pl.BlockSpec((L, dv_all), lambda b_, c: (rowblk(b_, 1, c), 0))],
        out_shape=[out_shape, out_shape],
        scratch_shapes=[pltpu.VMEM((2, M_HEADS, dqk, dv), F32), pltpu.VMEM((2, M_HEADS, 1, dqk), F32),
                        pltpu.VMEM((2, M_HEADS, 1, 1), F32)],
        compiler_params=pltpu.CompilerParams(dimension_semantics=("arbitrary",) * 2,
                                             vmem_limit_bytes=_vmem_limit(40 * 2**20)),
        name="mlstm_scan",
    )(q, k, v, gates, q, k, v, gates)


def _mlstm_post_kernel(xl_ref, xc_ref, s0_ref, s1_ref, mod_ref, wog_ref, ng_ref, wout_ref, g_ref, b_ref, o_ref,
                       *, alpha, geom):
    d = xl_ref.shape[1]
    x = _tile_rows(xl_ref, xc_ref, geom)
    chunks = [slice(r0, r0 + HEAD_ROWS) for r0 in range(0, x.shape[0], HEAD_ROWS)]
    h = [_mod(x[r, :], mod_ref, 0, d).astype(MXU) for r in chunks]
    og = [_sigmoid(_dot(hc, wog_ref[...])) for hc in h]
    a = [_head_rms(og[i] * (s0_ref[r, :] + s1_ref[r, :]), ng_ref[...], M_HEADS).astype(MXU) for i, r in enumerate(chunks)]
    _residual_ln(x, mod_ref, jnp.concatenate(a, axis=0), wout_ref, g_ref, b_ref, o_ref, alpha)


def _mlstm_post(rs, n_tiles, x_lat, x_ctx, ctx_blk0, s, mod, w_og, norm_g, w_out, ln_g, ln_b, alpha):
    d, inner = w_og.shape
    body = functools.partial(_mlstm_post_kernel, alpha=alpha, geom=rs.geom)
    vmem = 2 * (w_og.size + w_out.size) + 10 * TM * inner * 4 + 16 * 2**20
    return _row_call(
        body, rs, n_tiles,
        rs.tile_pair(d, ctx_blk0) + [rs.tile(inner), rs.tile(inner), rs.mod(6 * d), _const_spec(w_og.shape),
                                     _const_spec((1, inner)), _const_spec(w_out.shape), _const_spec((1, d)), _const_spec((1, d))],
        rs.tile(d), jax.ShapeDtypeStruct((n_tiles * rs.tm, d), F32), vmem=vmem, name="mlstm_post",
    )(x_lat, x_ctx, s[0], s[1], mod, w_og, norm_g.reshape(1, inner), w_out, ln_g.reshape(1, d), ln_b.reshape(1, d))


def _mla_pre_kernel(x_ref, mod_ref, cos_ref, sin_ref, wdq_ref, qn_ref, wuq_ref, wdkv_ref, kvn_ref, wk_ref, wv_ref,
                    q_ref, k_ref, v_ref, *, q_scale):
    d = x_ref.shape[1]
    hd = A_DNOPE + LANES
    h = _mod(x_ref[...], mod_ref, 0, d).astype(MXU)
    cos, sin = cos_ref[...], sin_ref[...]
    cq = _rms(_dot(h, wdq_ref[...]), qn_ref[...]).astype(MXU)
    qa = _dot(cq, wuq_ref[...])
    part0 = A_HEADS * hd
    cos_q, sin_q = cos * q_scale, sin * q_scale
    for hh in range(A_HEADS):
        q_ref[:, hh * hd:hh * hd + A_DNOPE] = (qa[:, hh * hd:hh * hd + A_DNOPE] * q_scale).astype(q_ref.dtype)
        rot = qa[:, hh * hd + A_DNOPE:(hh + 1) * hd] * cos_q + qa[:, part0 + hh * LANES:part0 + (hh + 1) * LANES] * sin_q
        q_ref[:, hh * hd + A_DNOPE:(hh + 1) * hd] = rot.astype(q_ref.dtype)
    dk = _dot(h, wdkv_ref[...])
    k_rope = (dk[:, A_KVLORA:A_KVLORA + LANES] * cos + dk[:, A_KVLORA + LANES:] * sin).astype(k_ref.dtype)
    ckv = _rms(dk[:, :A_KVLORA], kvn_ref[...]).astype(MXU)
    k_nope = _dot(ckv, wk_ref[...])
    for hh in range(A_HEADS):
        k_ref[:, hh * hd:hh * hd + A_DNOPE] = k_nope[:, hh * A_DNOPE:(hh + 1) * A_DNOPE].astype(k_ref.dtype)
        k_ref[:, hh * hd + A_DNOPE:(hh + 1) * hd] = k_rope
    v_ref[...] = _dot(ckv, wv_ref[...]).astype(v_ref.dtype)


def _rope_tables(seq, tm):
    n_freq = A_DROPE // 4
    inv_freq = ROPE_BASE ** (-jnp.arange(n_freq, dtype=F32) / n_freq)
    pos = jnp.arange(seq)
    ang_row = (pos // GRID_W).astype(F32)[:, None] * inv_freq
    ang_col = (pos % GRID_W).astype(F32)[:, None] * inv_freq
    cos = jnp.concatenate([jnp.cos(ang_row)] * 2 + [jnp.cos(ang_col)] * 2, axis=1)
    sin = jnp.concatenate([-jnp.sin(ang_row), jnp.sin(ang_row), -jnp.sin(ang_col), jnp.sin(ang_col)], axis=1)
    pad = LANES - A_DROPE
    cos = jnp.pad(cos, ((0, 0), (0, pad)), constant_values=1.0)
    sin = jnp.pad(sin, ((0, 0), (0, pad)))
    ident = (jnp.ones((tm, LANES), F32), jnp.zeros((tm, LANES), F32))
    return jnp.concatenate([ident[0], cos]), jnp.concatenate([ident[1], sin])


def _rope_partner_cols(w):
    idx = np.arange(A_DROPE)
    half = A_DROPE // 4
    partner = np.where((idx % (2 * half)) < half, idx + half, idx - half)
    return w[..., partner]


def _mla_pre(rs, x, mod, w_dq, q_norm, w_uq, w_dkv, kv_norm, w_ukv):
    d, qlora = w_dq.shape
    hd = A_DNOPE + LANES
    pad = LANES - A_DROPE
    wq = w_uq.reshape(qlora, A_HEADS, A_DNOPE + A_DROPE)
    wq_main = jnp.pad(wq, ((0, 0), (0, 0), (0, pad))).reshape(qlora, A_HEADS * hd)
    wq_part = jnp.pad(_rope_partner_cols(wq[..., A_DNOPE:]), ((0, 0), (0, 0), (0, pad))).reshape(qlora, A_HEADS * LANES)
    wuq_all = jnp.concatenate([wq_main, wq_part], axis=1).astype(MXU)
    w_kr = w_dkv[:, A_KVLORA:]
    wdkv_all = jnp.concatenate([w_dkv[:, :A_KVLORA], jnp.pad(w_kr, ((0, 0), (0, pad))),
                                jnp.pad(_rope_partner_cols(w_kr), ((0, 0), (0, pad)))], axis=1).astype(MXU)
    wkv = w_ukv.reshape(A_KVLORA, A_HEADS, A_DNOPE + A_DV)
    w_k = wkv[..., :A_DNOPE].reshape(A_KVLORA, A_HEADS * A_DNOPE).astype(MXU)
    w_v = wkv[..., A_DNOPE:].reshape(A_KVLORA, A_HEADS * A_DV).astype(MXU)
    cos, sin = _rope_tables(rs.seq, rs.tm)
    nl, tps = rs.n_lat_tiles, rs.tiles_per_seq
    tab = pl.BlockSpec((rs.tm, LANES), lambda i: (jnp.where(i >= nl, 0, 1 + i % tps), 0))
    n, r = rs.n_tiles, rs.rows
    vmem = 2 * 2 * (w_dq.size + wuq_all.size + wdkv_all.size + w_k.size + w_v.size) + 12 * rs.tm * A_HEADS * hd * 4
    return _row_call(
        functools.partial(_mla_pre_kernel, q_scale=float((A_DNOPE + A_DROPE) ** -0.5 * np.log2(np.e))), rs, n,
        [rs.tile(d), rs.mod(6 * d), tab, tab, _const_spec(w_dq.shape), _const_spec((1, qlora)), _const_spec(wuq_all.shape),
         _const_spec(wdkv_all.shape), _const_spec((1, A_KVLORA)), _const_spec(w_k.shape), _const_spec(w_v.shape)],
        [rs.tile(A_HEADS * hd), rs.tile(A_HEADS * hd), rs.tile(A_HEADS * A_DV)],
        [jax.ShapeDtypeStruct((r, A_HEADS * hd), MXU), jax.ShapeDtypeStruct((r, A_HEADS * hd), MXU),
         jax.ShapeDtypeStruct((r, A_HEADS * A_DV), MXU)],
        vmem=vmem, name="mla_pre",
    )(x, mod, cos, sin, w_dq.astype(MXU), q_norm.reshape(1, qlora), wuq_all, wdkv_all, kv_norm.reshape(1, A_KVLORA), w_k, w_v)


def _attn_kernel(*refs, with_latent, kv_chunk):
    if with_latent:
        q_ref, kc_ref, vc_ref, kl_ref, vl_ref, o_ref = refs
    else:
        q_ref, kc_ref, vc_ref, o_ref = refs
    hd, dv = A_DNOPE + LANES, A_DV
    heads = range(q_ref.shape[1] // hd)
    qs = lambda h: slice(h * hd, (h + 1) * hd)
    vs = lambda h: slice(h * dv, (h + 1) * dv)
    q = [q_ref[:, qs(h)] for h in heads]
    s = [_dot_nt(q[h], kc_ref[:, qs(h)]) for h in heads]
    m = [jnp.max(s[h], axis=1, keepdims=True) for h in heads]
    p = [jnp.exp2((s[h] - m[h]).astype(MXU)) for h in heads]
    l = [jnp.sum(p[h].astype(F32), axis=1, keepdims=True) for h in heads]
    acc = [_dot(p[h], vc_ref[:, vs(h)]) for h in heads]
    if with_latent:
        for c0 in range(0, kl_ref.shape[0], kv_chunk):
            s = [_dot_nt(q[h], kl_ref[c0:c0 + kv_chunk, qs(h)]) for h in heads]
            m_new = [jnp.maximum(m[h], jnp.max(s[h], axis=1, keepdims=True)) for h in heads]
            corr = [jnp.exp2(m[h] - m_new[h]) for h in heads]
            p = [jnp.exp2((s[h] - m_new[h]).astype(MXU)) for h in heads]
            l = [l[h] * corr[h] + jnp.sum(p[h].astype(F32), axis=1, keepdims=True) for h in heads]
            acc = [acc[h] * corr[h] + _dot(p[h], vl_ref[c0:c0 + kv_chunk, vs(h)]) for h in heads]
            m = m_new
    for h in heads:
        o_ref[:, vs(h)] = (acc[h] / l[h]).astype(o_ref.dtype)


def _mla_attention(rs, q, k, v):
    b, t = rs.batch, rs.seq
    hpb = 2
    hd, dv = hpb * (A_DNOPE + LANES), hpb * A_DV
    kv_chunk = min(1024, t)
    assert t % kv_chunk == 0 and A_HEADS % hpb == 0
    tc = rs.ctx_len
    ctx_blk0 = b * t // tc
    params = pltpu.CompilerParams(dimension_semantics=("arbitrary",) * 3, vmem_limit_bytes=_vmem_limit(48 * 2**20))
    o_ctx = pl.pallas_call(
        functools.partial(_attn_kernel, with_latent=False, kv_chunk=kv_chunk), grid=(b, A_HEADS // hpb, 1),
        in_specs=[pl.BlockSpec((tc, hd), lambda b_, h, i: (ctx_blk0 + b_, h)),
                  pl.BlockSpec((tc, hd), lambda b_, h, i: (ctx_blk0 + b_, h)),
                  pl.BlockSpec((tc, dv), lambda b_, h, i: (ctx_blk0 + b_, h))],
        out_specs=pl.BlockSpec((tc, dv), lambda b_, h, i: (b_, h)),
        out_shape=jax.ShapeDtypeStruct((b * tc, A_HEADS * A_DV), MXU), compiler_params=params, name="mla_attn_ctx",
    )(q, k, v)
    tq = ATTN_TQ
    nq = t // tq
    o_lat = pl.pallas_call(
        functools.partial(_attn_kernel, with_latent=True, kv_chunk=kv_chunk), grid=(b, A_HEADS // hpb, nq),
        in_specs=[pl.BlockSpec((tq, hd), lambda b_, h, i: (b_ * nq + i, h)),
                  pl.BlockSpec((tc, hd), lambda b_, h, i: (ctx_blk0 + b_, h)),
                  pl.BlockSpec((tc, dv), lambda b_, h, i: (ctx_blk0 + b_, h)),
                  pl.BlockSpec((t, hd), lambda b_, h, i: (b_, h)),
                  pl.BlockSpec((t, dv), lambda b_, h, i: (b_, h))],
        out_specs=pl.BlockSpec((tq, dv), lambda b_, h, i: (b_ * nq + i, h)),
        out_shape=jax.ShapeDtypeStruct((b * t, A_HEADS * A_DV), MXU), compiler_params=params, name="mla_attn_latent",
    )(q, k, v, k, v)
    return o_lat, o_ctx


def _proj_post_kernel(x_ref, al_ref, ac_ref, mod_ref, wout_ref, g_ref, b_ref, o_ref, *, alpha, n_lat_tiles):
    a = jnp.where(pl.program_id(0) >= n_lat_tiles, ac_ref[...], al_ref[...])
    _residual_ln(x_ref, mod_ref, a, wout_ref, g_ref, b_ref, o_ref, alpha)


def _proj_post(rs, n_tiles, x, a_lat, a_ctx, mod, w_out, ln_g, ln_b, alpha):
    kdim, d = w_out.shape
    nl = rs.n_lat_tiles
    body = functools.partial(_proj_post_kernel, alpha=alpha, n_lat_tiles=nl)
    return _row_call(
        body, rs, n_tiles,
        [rs.tile(d), pl.BlockSpec((rs.tm, kdim), lambda i: (jnp.minimum(i, nl - 1), 0)),
         pl.BlockSpec((rs.tm, kdim), lambda i: (jnp.maximum(i - nl, 0), 0)), rs.mod(6 * d), _const_spec(w_out.shape),
         _const_spec((1, d)), _const_spec((1, d))],
        rs.tile(d), jax.ShapeDtypeStruct((n_tiles * rs.tm, d), F32), vmem=32 * 2**20, name="proj_post",
    )(x, a_lat, a_ctx, mod, w_out, ln_g.reshape(1, d), ln_b.reshape(1, d))


def _gla_pre_kernel(x_ref, mod_ref, wqk_ref, wv_ref, wa1_ref, wa2_ref, ba_ref, q_ref, k_ref, v_ref, la_ref,
                    *, q_scale):
    d = x_ref.shape[1]
    half = q_ref.shape[1]
    chunks = [slice(r0, r0 + HEAD_ROWS) for r0 in range(0, x_ref.shape[0], HEAD_ROWS)]
    h = [_mod(x_ref[r, :], mod_ref, 0, d).astype(MXU) for r in chunks]
    a1 = [_dot(hc, wa1_ref[...]).astype(MXU) for hc in h]
    z = [_dot(a, wa2_ref[...]) + ba_ref[...] for a in a1]
    qk = [_dot(hc, wqk_ref[...]) for hc in h]
    for i, r in enumerate(chunks):
        la_ref[r, :] = _log_sigmoid(z[i]) * (LOG2E / G_TAU)
        q_ref[r, :] = qk[i][:, :half] * q_scale
        k_ref[r, :] = qk[i][:, half:]
    for i, r in enumerate(chunks):
        v_ref[r, :] = _dot(h[i], wv_ref[...]).astype(v_ref.dtype)


def _gla_pre(rs, x, mod, w_qk, w_v, w_a1, w_a2, b_a):
    d = w_qk.shape[0]
    dk_all = w_qk.shape[1] // 2
    wa1 = jnp.pad(jnp.concatenate([w_a1[0], w_a1[1]], axis=1), ((0, 0), (0, LANES - 2 * G_RANK))).astype(MXU)
    wa2 = jnp.zeros((LANES, 2 * dk_all), F32)
    wa2 = wa2.at[:G_RANK, :dk_all].set(w_a2[0]).at[G_RANK:2 * G_RANK, dk_all:].set(w_a2[1]).astype(MXU)
    ba = jnp.concatenate([b_a[0], b_a[1]]).reshape(1, 2 * dk_all)
    body = functools.partial(_gla_pre_kernel, q_scale=float((dk_all // G_HEADS) ** -0.5))
    n, r = rs.n_tiles, rs.rows
    return _row_call(
        body, rs, n,
        [rs.tile(d), rs.mod(6 * d), _const_spec(w_qk.shape), _const_spec(w_v.shape), _const_spec(wa1.shape),
         _const_spec(wa2.shape), _const_spec(ba.shape)],
        [rs.tile(dk_all), rs.tile(dk_all), rs.tile(w_v.shape[1]), rs.tile(2 * dk_all)],
        [jax.ShapeDtypeStruct((r, dk_all), F32), jax.ShapeDtypeStruct((r, dk_all), F32),
         jax.ShapeDtypeStruct((r, w_v.shape[1]), MXU), jax.ShapeDtypeStruct((r, 2 * dk_all), F32)],
        vmem=40 * 2**20, name="gla_pre",
    )(x, mod, w_qk, w_v, wa1, wa2, ba)


def _gla_tables(L):
    t = np.arange(L)
    tau, taup = t[:, None], t[None, :]
    groups = [taup <= tau, taup > tau]
    masks = []
    c = L // 2
    while c >= 1:
        blk = t // (2 * c)
        mid = blk * 2 * c + c
        second = (t % (2 * c)) >= c
        q_side = second[:, None] & (taup >= mid[:, None]) & (taup <= tau)
        k_side = (~second)[:, None] & (taup > tau) & (taup <= mid[:, None] - 1)
        groups.append(q_side | k_side)
        masks.append(second[:, None] & (~second)[None, :] & (blk[:, None] == blk[None, :]))
        c //= 2
    flip = lambda g: g[::-1, ::-1]
    sums = np.stack([np.concatenate(groups, axis=0), np.concatenate([flip(g) for g in groups], axis=0)])
    lvl = np.stack([np.stack(masks), np.stack([flip(m) for m in masks])])
    return sums.astype(np.float32), lvl.astype(np.float32)


def _gla_scan_kernel(q0_ref, k0_ref, v0_ref, la0_ref, q1_ref, k1_ref, v1_ref, la1_ref, sums_ref, masks_ref,
                     o0_ref, o1_ref, st_scr):
    @pl.when(pl.program_id(1) == 0)
    def _():
        st_scr[...] = jnp.zeros_like(st_scr)

    L = q0_ref.shape[0]
    dk, dv = q0_ref.shape[1] // G_HEADS, v0_ref.shape[1] // G_HEADS
    levels = masks_ref.shape[1]
    dirs = ((q0_ref, k0_ref, v0_ref, la0_ref, o0_ref), (q1_ref, k1_ref, v1_ref, la1_ref, o1_ref))
    chains = [(d, h) for d in range(2) for h in range(G_HEADS)]
    row = lax.broadcasted_iota(jnp.int32, (L, L), 0)
    col = lax.broadcasted_iota(jnp.int32, (L, L), 1)
    eye = row == col

    st_prev = {(d, h): st_scr[d, h] for d, h in chains}
    e = [jnp.exp2(_dot(sums_ref[d], jnp.concatenate(_split2(dirs[d][3][...]), axis=0)).astype(MXU)) for d in range(2)]
    q = [dirs[d][0][...] for d in range(2)]
    k = [dirs[d][1][...] for d in range(2)]
    q16 = [q[d].astype(MXU) for d in range(2)]
    k16 = [k[d].astype(MXU) for d in range(2)]
    qe = [q16[d] * e[d][0:L] for d in range(2)]
    kd = [k16[d] * e[d][L:2 * L] for d in range(2)]
    qk_diag = [q[d] * k[d] for d in range(2)]
    qt = [[q16[d] * e[d][(2 + lv) * L:(3 + lv) * L] for lv in range(levels)] for d in range(2)]
    kt = [[k16[d] * e[d][(2 + lv) * L:(3 + lv) * L] for lv in range(levels)] for d in range(2)]
    decay = [jnp.exp2(jnp.sum(dirs[d][3][...], axis=0, keepdims=True)) for d in range(2)]

    att = {}
    for d, h in chains:
        ks = slice(h * dk, (h + 1) * dk)
        a = jnp.where(eye, jnp.sum(qk_diag[d][:, ks], axis=1, keepdims=True), 0.0)
        for lv in range(levels):
            a = a + masks_ref[d, lv] * _dot_nt(qt[d][lv][:, ks], kt[d][lv][:, ks])
        att[d, h] = a.astype(MXU)
    for d, h in chains:
        ks, vs = slice(h * dk, (h + 1) * dk), slice(h * dv, (h + 1) * dv)
        v = dirs[d][2][:, vs]
        dirs[d][4][:, vs] = _dot_nt(qe[d][:, ks], st_prev[d, h].astype(MXU)) + _dot(att[d, h], v)
        st_scr[d, h] = st_prev[d, h] * decay[d][:, ks] + _dot_tn(v, kd[d][:, ks])


def _gla_scan(rs, q, k, v, la):
    b, L = rs.batch, G_CHUNK
    dk_all, dv_all = q.shape[1], v.shape[1]
    nc_ctx, nc_lat = rs.ctx_len // L, rs.seq // L
    rowblk = _scan_rowblock(b, nc_ctx, nc_lat)
    sums, masks = _gla_tables(L)
    sums, masks = jnp.asarray(np.concatenate([sums, sums], axis=2), MXU), jnp.asarray(masks, F32)

    def specs(d):
        rb = lambda b_, c: (rowblk(b_, d, c), 0)
        return [pl.BlockSpec((L, dk_all), rb), pl.BlockSpec((L, dk_all), rb), pl.BlockSpec((L, dv_all), rb),
                pl.BlockSpec((L, dk_all), lambda b_, c: (rowblk(b_, d, c), d))]

    out_shape = jax.ShapeDtypeStruct((rs.rows, dv_all), F32)
    return pl.pallas_call(
        _gla_scan_kernel, grid=(b, nc_ctx + nc_lat),
        in_specs=specs(0) + specs(1) + [_const_spec(sums.shape), _const_spec(masks.shape)],
        out_specs=[pl.BlockSpec((L, dv_all), lambda b_, c: (rowblk(b_, 0, c), 0)),
                   pl.BlockSpec((L, dv_all), lambda b_, c: (rowblk(b_, 1, c), 0))],
        out_shape=[out_shape, out_shape],
        scratch_shapes=[pltpu.VMEM((2, G_HEADS, dv_all // G_HEADS, dk_all // G_HEADS), F32)],
        compiler_params=pltpu.CompilerParams(dimension_semantics=("arbitrary",) * 2,
                                             vmem_limit_bytes=_vmem_limit(32 * 2**20)),
        name="gla_scan",
    )(q, k, v, la, q, k, v, la, sums, masks)


def _gla_post_kernel(x_ref, s0_ref, s1_ref, mod_ref, wr_ref, ng_ref, wout_ref, g_ref, b_ref, o_ref, *, alpha):
    d = x_ref.shape[1]
    chunks = [slice(r0, r0 + HEAD_ROWS) for r0 in range(0, x_ref.shape[0], HEAD_ROWS)]
    h = [_mod(x_ref[r, :], mod_ref, 0, d).astype(MXU) for r in chunks]
    gate = [_silu(_dot(hc, wr_ref[...])) for hc in h]
    a = [(_head_rms(s0_ref[r, :] + s1_ref[r, :], ng_ref[...], G_HEADS) * gate[i]).astype(MXU) for i, r in enumerate(chunks)]
    _residual_ln(x_ref, mod_ref, jnp.concatenate(a, axis=0), wout_ref, g_ref, b_ref, o_ref, alpha)


def _gla_post(rs, n_tiles, x, s, mod, w_r, norm_g, w_out, ln_g, ln_b, alpha):
    d, dv_all = w_r.shape
    body = functools.partial(_gla_post_kernel, alpha=alpha)
    return _row_call(
        body, rs, n_tiles,
        [rs.tile(d), rs.tile(dv_all), rs.tile(dv_all), rs.mod(6 * d), _const_spec(w_r.shape), _const_spec((1, dv_all)), _const_spec(w_out.shape),
         _const_spec((1, d)), _const_spec((1, d))],
        rs.tile(d), jax.ShapeDtypeStruct((n_tiles * rs.tm, d), F32), vmem=40 * 2**20, name="gla_post",
    )(x, s[0], s[1], mod, w_r, norm_g.reshape(1, dv_all), w_out, ln_g.reshape(1, d), ln_b.reshape(1, d))


def kernel(x, c, ctx, c_ctx, ada_w, ada_b, ln_g, ln_b, ffn_w_in, ffn_conv_w, ffn_conv_b, ffn_w_out, m_w_up, m_conv_w, m_conv_b, m_w_qk, m_w_v, m_w_gates, m_b_gates, m_w_og, m_norm_g, m_w_out, a_w_dq, a_q_norm, a_w_uq, a_w_dkv, a_kv_norm, a_w_ukv, a_w_out, g_w_qk, g_w_v, g_w_r, g_w_a1, g_w_a2, g_b_a, g_norm_g, g_w_out):
    batch, seq, d = x.shape
    depth = ada_w.shape[0]
    n_mixers = 3
    alpha = float((2 * depth) ** 0.25)
    rs = _Rows(batch, seq, ctx.shape[1], TM)
    rs_big = _Rows(batch, seq, ctx.shape[1], BIG_TM)
    bf = lambda w: w.astype(MXU)

    cond_rows = -(-(batch + 1) // SUBLANES) * SUBLANES
    cond = jnp.zeros((cond_rows, d), F32).at[:batch].set(c).at[batch].set(c_ctx)
    mods = _modulation(cond, ada_w, ada_b).reshape(depth, cond_rows, 1, 6 * d)

    x_lat, x_ctx = x.reshape(batch * seq, d), ctx.reshape(-1, d)
    xa = None
    for i in range(depth):
        need_ctx = i < depth - 1
        n_tiles = rs.n_tiles if need_ctx else rs.n_lat_tiles
        n_big = rs_big.n_tiles if need_ctx else rs_big.n_lat_tiles
        kind, j = i % n_mixers, i // n_mixers
        mod = mods[i]
        if kind == 0:
            src = (x_lat, x_ctx, 0) if xa is None else (xa, xa, rs.n_lat_tiles)
            q, k, v, gates = _mlstm_pre(rs, *src, mod, bf(m_w_up[j]), m_conv_w[j], m_conv_b[j], bf(m_w_qk[j]),
                                        _weight_product(m_w_up[j], m_w_v[j]), m_w_gates[j], m_b_gates[j])
            s = _mlstm_scan(rs, q, k, v, gates)
            xa = _mlstm_post(rs, n_tiles, *src, s, mod, bf(m_w_og[j]), m_norm_g[j], bf(m_w_out[j]), ln_g[i, 0],
                             ln_b[i, 0], alpha)
        elif xa is None:
            xa = jnp.concatenate([x_lat, x_ctx], axis=0)
        if kind == 1:
            q, k, v = _mla_pre(rs_big, xa, mod, a_w_dq[j], a_q_norm[j], a_w_uq[j], a_w_dkv[j], a_kv_norm[j], a_w_ukv[j])
            o_lat, o_ctx = _mla_attention(rs_big, q, k, v)
            xa = _proj_post(rs_big, n_big, xa, o_lat, o_ctx, mod, bf(a_w_out[j]), ln_g[i, 0], ln_b[i, 0], alpha)
        elif kind == 2:
            q, k, v, la = _gla_pre(rs_big, xa, mod, bf(g_w_qk[j]), bf(g_w_v[j]), g_w_a1[j], g_w_a2[j], g_b_a[j])
            s = _gla_scan(rs_big, q, k, v, la)
            xa = _gla_post(rs_big, n_big, xa, s, mod, bf(g_w_r[j]), g_norm_g[j], bf(g_w_out[j]), ln_g[i, 0], ln_b[i, 0], alpha)
        xa = _ffn(rs_big, n_big, xa, mod, bf(ffn_w_in[i]), ffn_conv_w[i], ffn_conv_b[i], bf(ffn_w_out[i]),
                  ln_g[i, 1], ln_b[i, 1], alpha)
    return xa.reshape(batch, seq, d)
```
